```python
import math
import jax
import jax.numpy as jnp
from jax import lax
import numpy as np

D_MODEL = 4096
BATCH = 4
SEQ = 2048
DEPTH = 2
DEC_BATCH = 8
DEC_SEQ = 4
PAST_LEN = 16384
PAGE_SIZE = 128

N_HEADS = 32
HEAD_DIM = D_MODEL // N_HEADS
BRANCH = N_HEADS * HEAD_DIM
KV_HEADS_A = 8
GQA_GROUP = N_HEADS // KV_HEADS_A
IDX_HEADS = 64
IDX_DIM = 128
TOPK_MAX = 256
DIL_WINDOWS = (128, 512, 2048)
DIL_RATES = (1, 4, 16)
N_DIL = len(DIL_WINDOWS)
W_MAX = max(DIL_WINDOWS)
BLK = 128
N_BUCKETS = 32
REL_MAX_DIST = 2048
ALPHA = (2 * DEPTH) ** 0.25
BETA = (8 * DEPTH) ** -0.25
LN_EPS = 1e-5
N_LAYERS_A = (DEPTH + 1) // 2
N_LAYERS_B = DEPTH // 2
A_SIZES = (BRANCH, KV_HEADS_A * HEAD_DIM, KV_HEADS_A * HEAD_DIM, BRANCH, IDX_HEADS * IDX_DIM, IDX_DIM, IDX_HEADS)
B_SIZES = (N_DIL * BRANCH, BRANCH, BRANCH, BRANCH)
COLS_A = sum(A_SIZES)
COLS_B = sum(B_SIZES)

kernel_name = 'dsa_dilated_hybrid_step'


def _split(z, sizes):
    offs = np.cumsum(np.asarray(sizes))[:-1].tolist()
    return jnp.split(z, offs, axis=-1)


def _to_blocks(a):
    b, s = a.shape[:2]
    return jnp.moveaxis(a.reshape(b, s // BLK, BLK, *a.shape[2:]), 1, 0)


def rel_bucket(n):
    max_exact = N_BUCKETS // 2
    nf = jnp.maximum(n, 1).astype(jnp.float32)
    large = max_exact + (jnp.log(nf / max_exact) / math.log(REL_MAX_DIST / max_exact)
                         * (N_BUCKETS - max_exact)).astype(jnp.int32)
    large = jnp.minimum(large, N_BUCKETS - 1)
    return jnp.where(n < max_exact, n, large)


def layer_norm(x, g, b):
    xf = x.astype(jnp.float32)
    mu = jnp.mean(xf, axis=-1, keepdims=True)
    var = jnp.mean(jnp.square(xf - mu), axis=-1, keepdims=True)
    return ((xf - mu) * lax.rsqrt(var + LN_EPS) * g.astype(jnp.float32) + b.astype(jnp.float32)).astype(x.dtype)


def index_scores(qi, wi, ki):
    s = jnp.einsum('bthe,ble->bthl', qi, ki).astype(jnp.float32) * (IDX_DIM ** -0.5)
    return jnp.einsum('bthl,bth->btl', jax.nn.relu(s), wi.astype(jnp.float32) * (IDX_HEADS ** -0.5))


def sparse_attend(q, kg, vg, dist, rel_bias):
    b, t, s = dist.shape
    logits = jnp.einsum('btkgd,btskd->btkgs', q, kg).astype(jnp.float32) * (HEAD_DIM ** -0.5)
    bias = rel_bias[rel_bucket(jnp.maximum(dist, 0))].astype(jnp.float32)
    bias = bias.reshape(b, t, s, KV_HEADS_A, GQA_GROUP).transpose(0, 1, 3, 4, 2)
    logits = jnp.where((dist >= 0)[:, :, None, None, :], logits + bias, -jnp.inf)
    p = jax.nn.softmax(logits, axis=-1)
    return jnp.einsum('btkgs,btskd->btkgd', p.astype(vg.dtype), vg)


def mixer_a_prompt(h, w_in, w_out, rel_bias):
    b, s, _ = h.shape
    q, k, v, gate, qi, ki, wi = _split(h @ w_in, A_SIZES)
    q = q.reshape(b, s, KV_HEADS_A, GQA_GROUP, HEAD_DIM)
    k = k.reshape(b, s, KV_HEADS_A, HEAD_DIM)
    v = v.reshape(b, s, KV_HEADS_A, HEAD_DIM)
    qi = qi.reshape(b, s, IDX_HEADS, IDX_DIM)
    n_sel = min(TOPK_MAX, s // 4)
    bidx = jnp.arange(b)[:, None, None]
    key_pos = jnp.arange(s)

    def attend_block(args):
        q_blk, qi_blk, wi_blk, start = args
        t = start + jnp.arange(BLK)
        scores = index_scores(qi_blk, wi_blk, ki)
        scores = jnp.where((key_pos[None, :] <= t[:, None])[None], scores, -jnp.inf)
        _, sel = lax.top_k(scores, n_sel)
        dist = t[None, :, None] - sel
        return sparse_attend(q_blk, k[bidx, sel], v[bidx, sel], dist, rel_bias)

    starts = jnp.arange(s // BLK, dtype=jnp.int32) * BLK
    o = lax.map(attend_block, (_to_blocks(q), _to_blocks(qi), _to_blocks(wi), starts))
    o = jnp.moveaxis(o, 0, 1).reshape(b, s, BRANCH)
    y = (o * jax.nn.silu(gate)) @ w_out
    return y, k, v, ki


def mixer_a_sample(h, cache_k, cache_v, cache_ki, page_table, w_in, w_out, rel_bias):
    b, t_new, _ = h.shape
    n_pages = page_table.shape[1]
    past = n_pages * PAGE_SIZE
    n_keys = past + t_new
    q, k, v, gate, qi, ki, wi = _split(h @ w_in, A_SIZES)
    q = q.reshape(b, t_new, KV_HEADS_A, GQA_GROUP, HEAD_DIM)
    k = k.reshape(b, t_new, KV_HEADS_A, HEAD_DIM)
    v = v.reshape(b, t_new, KV_HEADS_A, HEAD_DIM)
    qi = qi.reshape(b, t_new, IDX_HEADS, IDX_DIM)
    ki_all = jnp.concatenate([cache_ki[page_table].reshape(b, past, IDX_DIM), ki], axis=1)
    t = past + jnp.arange(t_new)
    scores = index_scores(qi, wi, ki_all)
    scores = jnp.where((jnp.arange(n_keys)[None, :] <= t[:, None])[None], scores, -jnp.inf)
    _, sel = lax.top_k(scores, min(TOPK_MAX, n_keys // 4))
    bidx = jnp.arange(b)[:, None, None]
    in_past = (sel < past)[..., None, None]
    phys = page_table[bidx, jnp.minimum(sel // PAGE_SIZE, n_pages - 1)]
    off = sel % PAGE_SIZE
    new_i = jnp.clip(sel - past, 0, t_new - 1)
    k_g = jnp.where(in_past, cache_k[phys, off], k[bidx, new_i])
    v_g = jnp.where(in_past, cache_v[phys, off], v[bidx, new_i])
    o = sparse_attend(q, k_g, v_g, t[None, :, None] - sel, rel_bias).reshape(b, t_new, BRANCH)
    y = (o * jax.nn.silu(gate)) @ w_out
    return y, k, v, ki


def _to_residues(a, d):
    b, s = a.shape[:2]
    a = jnp.moveaxis(a.reshape(b, s // d, d, *a.shape[2:]), 2, 1)
    return a.reshape(b * d, s // d, *a.shape[3:])


def _from_residues(a, b, d):
    n = a.shape[1]
    a = jnp.moveaxis(a.reshape(b, d, n, *a.shape[2:]), 1, 2)
    return a.reshape(b, n * d, *a.shape[3:])


def band_attend(q, k, v, d, nback, rel_bias):
    n, l, nh, dh = q.shape
    nb = -(-l // BLK)
    lp = nb * BLK
    pad_end = lp - l
    qb = jnp.pad(q, ((0, 0), (0, pad_end), (0, 0), (0, 0))).reshape(n, nb, BLK, nh, dh)

    def band(a):
        ap = jnp.pad(a, ((0, 0), (BLK, pad_end), (0, 0), (0, 0)))
        return jnp.concatenate([ap[:, :lp].reshape(n, nb, BLK, nh, dh),
                                ap[:, BLK:].reshape(n, nb, BLK, nh, dh)], axis=2)

    kb, vb = band(k), band(v)
    i = jnp.arange(BLK)[:, None]
    j = jnp.arange(2 * BLK)[None, :]
    m = i + BLK - j
    key_pos = jnp.arange(nb)[:, None, None] * BLK - BLK + j[None]
    valid = (m >= 0) & (m <= nback) & (key_pos >= 0)
    bias = jnp.moveaxis(rel_bias[rel_bucket(d * jnp.clip(m, 0, nback))], -1, 0).astype(jnp.float32)
    logits = jnp.einsum('nbqhd,nbkhd->nbhqk', qb, kb).astype(jnp.float32) * (HEAD_DIM ** -0.5) + bias
    logits = jnp.where(valid[None, :, None], logits, -jnp.inf)
    lse = jax.nn.logsumexp(logits, axis=-1)
    p = jnp.exp(logits - lse[..., None])
    o = jnp.einsum('nbhqk,nbkhd->nbqhd', p.astype(v.dtype), vb).reshape(n, lp, nh, dh)[:, :l]
    lse = jnp.moveaxis(lse, 2, 3).reshape(n, lp, nh)[:, :l]
    return o, lse


def merge_dilations(outs, lses):
    wts = jax.nn.softmax(jnp.stack(lses, axis=0), axis=0)
    return jnp.einsum('gbthd,gbth->bthd', jnp.stack(outs, axis=0).astype(jnp.float32), wts)


def mixer_b_prompt(h, w_in, w_out, rel_bias):
    b, s, _ = h.shape
    q, k, v, gate = _split(h @ w_in, B_SIZES)
    q = q.reshape(b, s, N_DIL, N_HEADS, HEAD_DIM)
    k = k.reshape(b, s, N_HEADS, HEAD_DIM)
    v = v.reshape(b, s, N_HEADS, HEAD_DIM)
    outs, lses = [], []
    for g in range(N_DIL):
        d = DIL_RATES[g]
        o_g, lse_g = band_attend(_to_residues(q[:, :, g], d), _to_residues(k, d), _to_residues(v, d),
                                 d, DIL_WINDOWS[g] // d, rel_bias)
        outs.append(_from_residues(o_g, b, d))
        lses.append(_from_residues(lse_g, b, d))
    o = merge_dilations(outs, lses).astype(h.dtype).reshape(b, s, BRANCH)
    y = (o * jax.nn.silu(gate)) @ w_out
    wb = min(W_MAX, s)
    return y, k[:, s - wb:], v[:, s - wb:]


def mixer_b_sample(h, buf_k, buf_v, w_in, w_out, rel_bias):
    b, t_new, _ = h.shape
    wb = buf_k.shape[1]
    q, k, v, gate = _split(h @ w_in, B_SIZES)
    q = q.reshape(b, t_new, N_DIL, N_HEADS, HEAD_DIM)
    k = k.reshape(b, t_new, N_HEADS, HEAD_DIM)
    v = v.reshape(b, t_new, N_HEADS, HEAD_DIM)
    k_all = jnp.concatenate([buf_k, k], axis=1)
    v_all = jnp.concatenate([buf_v, v], axis=1)
    jq = jnp.arange(t_new)
    outs, lses = [], []
    for g in range(N_DIL):
        d = DIL_RATES[g]
        m = jnp.arange(DIL_WINDOWS[g] // d + 1)
        idx = wb + jq[:, None] - d * m[None, :]
        valid = idx >= 0
        idxc = jnp.clip(idx, 0, wb + t_new - 1)
        kg, vg = k_all[:, idxc], v_all[:, idxc]
        bias = rel_bias[rel_bucket(d * m)].T.astype(jnp.float32)
        logits = jnp.einsum('bthd,btmhd->bthm', q[:, :, g], kg).astype(jnp.float32) * (HEAD_DIM ** -0.5) + bias
        logits = jnp.where(valid[None, :, None, :], logits, -jnp.inf)
        lse = jax.nn.logsumexp(logits, axis=-1)
        p = jnp.exp(logits - lse[..., None])
        outs.append(jnp.einsum('bthm,btmhd->bthd', p.astype(vg.dtype), vg))
        lses.append(lse)
    o = merge_dilations(outs, lses).astype(h.dtype).reshape(b, t_new, BRANCH)
    y = (o * jax.nn.silu(gate)) @ w_out
    return y, k_all[:, t_new:], v_all[:, t_new:]


def _col_scale(sizes, slot, value):
    scale = np.ones((sum(sizes),), np.float32)
    start = int(sum(sizes[:slot]))
    scale[start:start + sizes[slot]] = value
    return jnp.asarray(scale)


def setup_inputs(seed: int = 0) -> dict:
    key = jax.random.key(seed)
    ks = jax.random.split(key, 20)
    nrm = jax.random.normal
    f32 = jnp.float32
    n_pages = PAST_LEN // PAGE_SIZE
    n_used = DEC_BATCH * n_pages
    n_pool = n_used + max(1, n_used // 4)
    wb = min(W_MAX, PAST_LEN)
    x_prompt = nrm(ks[0], (BATCH, SEQ, D_MODEL), f32)
    x_sample = nrm(ks[1], (DEC_BATCH, DEC_SEQ, D_MODEL), f32)
    cache_k_a = nrm(ks[2], (N_LAYERS_A, n_pool, PAGE_SIZE, KV_HEADS_A, HEAD_DIM), f32)
    cache_v_a = BETA * nrm(ks[3], (N_LAYERS_A, n_pool, PAGE_SIZE, KV_HEADS_A, HEAD_DIM), f32)
    cache_kidx_a = nrm(ks[4], (N_LAYERS_A, n_pool, PAGE_SIZE, IDX_DIM), f32)
    cache_k_b = nrm(ks[5], (N_LAYERS_B, DEC_BATCH, wb, N_HEADS, HEAD_DIM), f32)
    cache_v_b = BETA * nrm(ks[6], (N_LAYERS_B, DEC_BATCH, wb, N_HEADS, HEAD_DIM), f32)
    page_table = jax.random.permutation(ks[7], n_pool)[:n_used].reshape(DEC_BATCH, n_pages).astype(jnp.int32)
    rel_bias = 0.1 * nrm(ks[8], (N_BUCKETS, N_HEADS), f32)
    w_in_a = nrm(ks[9], (N_LAYERS_A, D_MODEL, COLS_A), f32) * (D_MODEL ** -0.5) * _col_scale(A_SIZES, 2, BETA)
    w_out_a = nrm(ks[10], (N_LAYERS_A, BRANCH, D_MODEL), f32) * (BETA * BRANCH ** -0.5)
    ln_g_a = 1.0 + 0.02 * nrm(ks[11], (N_LAYERS_A, D_MODEL), f32)
    ln_b_a = 0.02 * nrm(ks[12], (N_LAYERS_A, D_MODEL), f32)
    w_in_b = nrm(ks[13], (N_LAYERS_B, D_MODEL, COLS_B), f32) * (D_MODEL ** -0.5) * _col_scale(B_SIZES, 2, BETA)
    w_out_b = nrm(ks[14], (N_LAYERS_B, BRANCH, D_MODEL), f32) * (BETA * BRANCH ** -0.5)
    ln_g_b = 1.0 + 0.02 * nrm(ks[15], (N_LAYERS_B, D_MODEL), f32)
    ln_b_b = 0.02 * nrm(ks[16], (N_LAYERS_B, D_MODEL), f32)
    return {'x_prompt': x_prompt, 'x_sample': x_sample,
            'cache_k_a': cache_k_a, 'cache_v_a': cache_v_a, 'cache_kidx_a': cache_kidx_a,
            'cache_k_b': cache_k_b, 'cache_v_b': cache_v_b, 'page_table': page_table,
            'rel_bias': rel_bias,
            'w_in_a': w_in_a, 'w_out_a': w_out_a, 'ln_g_a': ln_g_a, 'ln_b_a': ln_b_a,
            'w_in_b': w_in_b, 'w_out_b': w_out_b, 'ln_g_b': ln_g_b, 'ln_b_b': ln_b_b}


def reference(x_prompt, x_sample, cache_k_a, cache_v_a, cache_kidx_a, cache_k_b, cache_v_b, page_table,
              rel_bias, w_in_a, w_out_a, ln_g_a, ln_b_a, w_in_b, w_out_b, ln_g_b, ln_b_b):
    xp, xs = x_prompt, x_sample
    ka_p, va_p, ia_p, ka_s, va_s, ia_s = [], [], [], [], [], []
    kb_p, vb_p, kb_s, vb_s = [], [], [], []
    for layer in range(DEPTH):
        j = layer // 2
        if layer % 2 == 0:
            yp, kp, vp, ip = mixer_a_prompt(xp, w_in_a[j], w_out_a[j], rel_bias)
            ys, k_s, v_s, i_s = mixer_a_sample(xs, cache_k_a[j], cache_v_a[j], cache_kidx_a[j], page_table,
                                               w_in_a[j], w_out_a[j], rel_bias)
            xp = layer_norm(ALPHA * xp + yp, ln_g_a[j], ln_b_a[j])
            xs = layer_norm(ALPHA * xs + ys, ln_g_a[j], ln_b_a[j])
            ka_p.append(kp); va_p.append(vp); ia_p.append(ip)
            ka_s.append(k_s); va_s.append(v_s); ia_s.append(i_s)
        else:
            yp, kp, vp = mixer_b_prompt(xp, w_in_b[j], w_out_b[j], rel_bias)
            ys, k_s, v_s = mixer_b_sample(xs, cache_k_b[j], cache_v_b[j], w_in_b[j], w_out_b[j], rel_bias)
            xp = layer_norm(ALPHA * xp + yp, ln_g_b[j], ln_b_b[j])
            xs = layer_norm(ALPHA * xs + ys, ln_g_b[j], ln_b_b[j])
            kb_p.append(kp); vb_p.append(vp); kb_s.append(k_s); vb_s.append(v_s)
    return (xp, xs, jnp.stack(ka_p), jnp.stack(va_p), jnp.stack(ia_p), jnp.stack(ka_s), jnp.stack(va_s),
            jnp.stack(ia_s), jnp.stack(kb_p), jnp.stack(vb_p), jnp.stack(kb_s), jnp.stack(vb_s))
```

```python
import functools
import math

import jax
import jax.numpy as jnp
import numpy as np
from jax import lax
from jax.experimental import pallas as pl
from jax.experimental.pallas import tpu as pltpu

HEAD_DIM = 128
KV_HEADS_A = 8
IDX_HEADS = 64
IDX_DIM = 128
TOPK_MAX = 256
DIL_WINDOWS = (128, 512, 2048)
DIL_RATES = (1, 4, 16)
W_MAX = max(DIL_WINDOWS)
BLK = 128
N_BUCKETS = 32
REL_MAX_DIST = 2048
DEPTH = 2
ALPHA = (2 * DEPTH) ** 0.25
LN_EPS = 1e-5
PAGE_SIZE = 128

LANE = 128
NEG_INF = float("-inf")
INT_MIN = -(2 ** 31)
M_INIT = -1e30
BUCKET_ZERO = -1
BUCKET_MASKED = -2
VMEM_LIMIT = 56 * 1024 * 1024

f32 = jnp.float32
bf16 = jnp.bfloat16
i32 = jnp.int32


def _params(n_axes, vmem=VMEM_LIMIT):
    return pltpu.CompilerParams(dimension_semantics=("arbitrary",) * n_axes, vmem_limit_bytes=vmem)


def _nt_dot(a, b):
    return lax.dot_general(a, b, (((1,), (1,)), ((), ())), preferred_element_type=f32)


def _silu(x):
    return x * (1.0 / (1.0 + jnp.exp(-x)))


def _mm_kernel(x_ref, w_ref, o_ref, wbf_ref, *, k_chunk):
    @pl.when(pl.program_id(1) == 0)
    def _():
        def body(r, c):
            rows = pl.ds(pl.multiple_of(r * k_chunk, k_chunk), k_chunk)
            wbf_ref[rows, :] = w_ref[rows, :].astype(bf16)
            return c
        lax.fori_loop(0, w_ref.shape[0] // k_chunk, body, 0)

    o_ref[...] = jnp.dot(x_ref[...], wbf_ref[...], preferred_element_type=f32).astype(o_ref.dtype)


def _pick_tile(n, candidates):
    for c in candidates:
        if n % c == 0:
            return c
    raise ValueError(f"no tile for {n}")


def _matmul(x_bf, w, col_off, ncols, out_dtype, name="matmul"):
    m, k = x_bf.shape
    tn = _pick_tile(math.gcd(col_off, ncols) if col_off else ncols, (512, 256, 128))
    tm = _pick_tile(m, (1024, 512, 256, 128, 64, 32, 16))
    off_blocks = col_off // tn
    k_chunk = _pick_tile(k, (256, 128))
    return pl.pallas_call(
        functools.partial(_mm_kernel, k_chunk=k_chunk), name=name,
        grid=(ncols // tn, m // tm),
        in_specs=[pl.BlockSpec((tm, k), lambda n, i: (i, 0)),
                  pl.BlockSpec((k, tn), lambda n, i: (0, n + off_blocks))],
        out_specs=pl.BlockSpec((tm, tn), lambda n, i: (i, n)),
        out_shape=jax.ShapeDtypeStruct((m, ncols), out_dtype),
        scratch_shapes=[pltpu.VMEM((k, tn), bf16)],
        compiler_params=_params(2),
    )(x_bf, w)


def _ln_kernel(y_ref, x_ref, g_ref, b_ref, o_ref):
    z = ALPHA * x_ref[...] + y_ref[...]
    mu = jnp.mean(z, axis=-1, keepdims=True)
    zc = z - mu
    var = jnp.mean(zc * zc, axis=-1, keepdims=True)
    o_ref[...] = zc * lax.rsqrt(var + LN_EPS) * g_ref[...] + b_ref[...]


def _layernorm(y, x, g, b):
    m, d = x.shape
    tm = _pick_tile(m, (128, 64, 32, 16, 8))
    row = pl.BlockSpec((tm, d), lambda i: (i, 0))
    vec = pl.BlockSpec((1, d), lambda i: (0, 0))
    return pl.pallas_call(
        _ln_kernel, grid=(m // tm,), name="layernorm",
        in_specs=[row, row, vec, vec], out_specs=row,
        out_shape=jax.ShapeDtypeStruct((m, d), f32),
        compiler_params=_params(1),
    )(y, x, g.reshape(1, d), b.reshape(1, d))


def _rel_bucket_np(n):
    max_exact = N_BUCKETS // 2
    nf = np.maximum(n, 1).astype(np.float64)
    large = max_exact + (np.log(nf / max_exact) / math.log(REL_MAX_DIST / max_exact)
                         * (N_BUCKETS - max_exact)).astype(np.int64)
    large = np.minimum(large, N_BUCKETS - 1)
    return np.where(n < max_exact, n, large).astype(np.int32)


def _bucket_codes(dist, valid=None):
    code = np.where(dist >= 0, _rel_bucket_np(np.maximum(dist, 0)), BUCKET_ZERO)
    if valid is not None:
        code = np.where(valid, code, BUCKET_MASKED)
    return code.astype(np.int32)


def _bias_kernel(tbl_ref, bm_ref, o_ref, *, cw):
    h = pl.program_id(0)

    def body(c, carry):
        cols = pl.ds(pl.multiple_of(c * cw, LANE), cw)
        bm = bm_ref[:, cols]
        acc = jnp.where(bm == BUCKET_MASKED, NEG_INF, 0.0).astype(f32)
        for bk in range(N_BUCKETS):
            acc = jnp.where(bm == bk, tbl_ref[bk, h], acc)
        o_ref[0, :, cols] = acc
        return carry

    lax.fori_loop(0, bm_ref.shape[1] // cw, body, 0)


def _bias_table(rel_bias, codes):
    r, c = codes.shape
    n_heads = rel_bias.shape[1]
    units = c // LANE
    per = max(1, 16 // -(-r // 8))
    cw = LANE * max(u for u in range(1, units + 1) if units % u == 0 and u <= per)
    return pl.pallas_call(
        functools.partial(_bias_kernel, cw=cw), name="bias_table",
        grid=(n_heads,),
        in_specs=[pl.BlockSpec(memory_space=pltpu.SMEM),
                  pl.BlockSpec((r, c), lambda h: (0, 0))],
        out_specs=pl.BlockSpec((1, r, c), lambda h: (h, 0, 0)),
        out_shape=jax.ShapeDtypeStruct((n_heads, r, c), f32),
        compiler_params=_params(1),
    )(rel_bias, jnp.asarray(codes))


def _sortable_key(x):
    bits = pltpu.bitcast(x, i32)
    return bits ^ ((bits >> 31) & 0x7FFFFFFF)


def _kth_largest_key(count_ge, shape, k_sel):
    def body(it, t):
        cand = t + lax.shift_left(jnp.int32(1), 31 - it)
        return jnp.where(count_ge(cand) >= k_sel, cand, t)
    return lax.fori_loop(0, 32, body, jnp.full(shape, INT_MIN, i32))


def _select_prompt_kernel(qi_ref, ki_ref, wt_ref, m_ref, kibf_ref, key_ref, *, n_idx_heads, k_sel, ck, cscale):
    i = pl.program_id(1)
    tq = qi_ref.shape[0]

    @pl.when(i == 0)
    def _():
        kibf_ref[...] = ki_ref[...].astype(bf16)

    nck = ((i + 1) * tq + ck - 1) // ck
    t_row = i * tq + lax.broadcasted_iota(i32, (1, tq), 1)

    def chunk_body(c, carry):
        koff = pl.multiple_of(c * ck, ck)
        kc = kibf_ref[pl.ds(koff, ck), :]

        def head_body(h, acc):
            qh = qi_ref[:, pl.ds(pl.multiple_of(h * IDX_DIM, IDX_DIM), IDX_DIM)]
            s = _nt_dot(kc, qh)
            w = wt_ref[pl.ds(h, 1), :] * cscale
            return acc + jnp.maximum(s, 0.0) * w

        acc = lax.fori_loop(0, n_idx_heads, head_body, jnp.zeros((ck, tq), f32))
        l_col = koff + lax.broadcasted_iota(i32, (ck, 1), 0)
        key_ref[pl.ds(koff, ck), :] = jnp.where(l_col <= t_row, _sortable_key(acc), INT_MIN)
        return carry

    lax.fori_loop(0, nck, chunk_body, 0)

    def count_ge(cand):
        def body(c, cnt):
            kc = key_ref[pl.ds(pl.multiple_of(c * ck, ck), ck), :]
            hit = jnp.where(kc >= cand, 1, 0).astype(i32)
            return cnt + hit.reshape(ck // 8, 8, tq).sum(axis=0)
        cnt = lax.fori_loop(0, nck, body, jnp.zeros((8, tq), i32))
        return cnt.sum(axis=0, keepdims=True)

    thr = _kth_largest_key(count_ge, (1, tq), k_sel)
    thr = jnp.maximum(thr, INT_MIN + 1)

    m_ref[...] = jnp.full(m_ref.shape, NEG_INF, m_ref.dtype)

    def out_body(c, carry):
        koff = pl.multiple_of(c * ck, ck)
        sel = jnp.where(key_ref[pl.ds(koff, ck), :] >= thr, 0.0, NEG_INF).astype(f32)
        m_ref[:, pl.ds(koff, ck)] = sel.T.astype(m_ref.dtype)
        return carry

    lax.fori_loop(0, nck, out_body, 0)


def _select_prompt(qi, ki, wi_t, batch, seq, k_sel):
    n_idx_heads = wi_t.shape[0]
    tq, ck = BLK, 256
    nblk = seq // tq
    cscale = (IDX_DIM ** -0.5) * (n_idx_heads ** -0.5)
    return pl.pallas_call(
        functools.partial(_select_prompt_kernel, n_idx_heads=n_idx_heads, k_sel=k_sel, ck=ck, cscale=cscale),
        name="select_prompt", grid=(batch, nblk),
        in_specs=[pl.BlockSpec((tq, n_idx_heads * IDX_DIM), lambda b, i: (b * nblk + i, 0)),
                  pl.BlockSpec((seq, IDX_DIM), lambda b, i: (b, 0)),
                  pl.BlockSpec((n_idx_heads, tq), lambda b, i: (0, b * nblk + i))],
        out_specs=pl.BlockSpec((tq, seq), lambda b, i: (b * nblk + i, 0)),
        out_shape=jax.ShapeDtypeStruct((batch * seq, seq), bf16),
        scratch_shapes=[pltpu.VMEM((seq, IDX_DIM), bf16), pltpu.VMEM((seq, tq), i32)],
        compiler_params=_params(2),
    )(qi, ki, wi_t)


def _attend_prompt_kernel(q_ref, k_ref, v_ref, m_ref, g_ref, gate_ref, o_ref, kbf_ref, vbf_ref, *, group, ck, scale):
    i = pl.program_id(2)
    nblk = pl.num_programs(2)
    tq = q_ref.shape[0]

    @pl.when(i == 0)
    def _():
        kbf_ref[...] = k_ref[...].astype(bf16)
        vbf_ref[...] = v_ref[...].astype(bf16)

    nck = ((i + 1) * tq + ck - 1) // ck
    goff0 = (nblk - 1 - i) * tq

    for g in range(group):
        qg = q_ref[:, g * HEAD_DIM:(g + 1) * HEAD_DIM]

        def body(c, carry, g=g, qg=qg):
            m_old, l_old, acc = carry
            koff = pl.multiple_of(c * ck, ck)
            s = _nt_dot(qg, kbf_ref[pl.ds(koff, ck), :]) * scale
            s = s + g_ref[g, :, pl.ds(pl.multiple_of(goff0 + koff, LANE), ck)]
            s = s + m_ref[:, pl.ds(koff, ck)].astype(f32)
            m_new = jnp.maximum(m_old, s.max(axis=1, keepdims=True))
            alpha = jnp.exp(m_old - m_new)
            p = jnp.exp(s - m_new)
            l_new = alpha * l_old + p.sum(axis=1, keepdims=True)
            acc = alpha * acc + jnp.dot(p.astype(bf16), vbf_ref[pl.ds(koff, ck), :], preferred_element_type=f32)
            return m_new, l_new, acc

        init = (jnp.full((tq, 1), M_INIT, f32), jnp.zeros((tq, 1), f32), jnp.zeros((tq, HEAD_DIM), f32))
        _, l_fin, acc = lax.fori_loop(0, nck, body, init)
        gt = gate_ref[:, g * HEAD_DIM:(g + 1) * HEAD_DIM]
        o_ref[:, g * HEAD_DIM:(g + 1) * HEAD_DIM] = (acc / l_fin * _silu(gt)).astype(o_ref.dtype)


def _attend_prompt(q, k, v, maskadd, bias_tab, gate, batch, seq):
    kv = k.shape[1] // HEAD_DIM
    group = q.shape[1] // (kv * HEAD_DIM)
    tq, ck = BLK, 256
    nblk = seq // tq
    gw = group * HEAD_DIM
    qspec = pl.BlockSpec((tq, gw), lambda kh, b, i: (b * nblk + i, kh))
    kvspec = pl.BlockSpec((seq, HEAD_DIM), lambda kh, b, i: (b, kh))
    return pl.pallas_call(
        functools.partial(_attend_prompt_kernel, group=group, ck=ck, scale=HEAD_DIM ** -0.5),
        name="attend_prompt_a", grid=(kv, batch, nblk),
        in_specs=[qspec, kvspec, kvspec,
                  pl.BlockSpec((tq, seq), lambda kh, b, i: (b * nblk + i, 0)),
                  pl.BlockSpec((group, tq, bias_tab.shape[2]), lambda kh, b, i: (kh, 0, 0)),
                  qspec],
        out_specs=qspec,
        out_shape=jax.ShapeDtypeStruct(q.shape, bf16),
        scratch_shapes=[pltpu.VMEM((seq, HEAD_DIM), bf16), pltpu.VMEM((seq, HEAD_DIM), bf16)],
        compiler_params=_params(3),
    )(q, k, v, maskadd, bias_tab, gate)


def _score_page(qs, wb, kpage, t_new):
    s = _nt_dot(qs, kpage.astype(bf16))
    x = jnp.maximum(s, 0.0) * wb
    return x.reshape(t_new, x.shape[0] // t_new, x.shape[1]).sum(axis=1)


def _sample_scores_kernel(pt_ref, qs_ref, wb_ref, *refs, pg, t_new):
    o_ref = refs[pg]
    for r in range(pg):
        o_ref[0, :, r * PAGE_SIZE:(r + 1) * PAGE_SIZE] = _score_page(qs_ref[0], wb_ref[0], refs[r][0], t_new)


def _sample_scores_tail_kernel(qs_ref, wb_ref, kt_ref, o_ref, *, t_new):
    o_ref[0] = _score_page(qs_ref[0], wb_ref[0], kt_ref[0], t_new)


def _sample_scores(page_table, qs, wb, cache_ki, ki_tail, t_new):
    dbatch, rows, _ = qs.shape
    n_pages = page_table.shape[1]
    pg = _pick_tile(n_pages, (8, 4, 2, 1))
    qspec = pl.BlockSpec((1, rows, IDX_DIM), lambda b, s, pt: (b, 0, 0))
    page_specs = [pl.BlockSpec((1, PAGE_SIZE, IDX_DIM), lambda b, s, pt, r=r: (pt[b, s * pg + r], 0, 0))
                  for r in range(pg)]
    main = pl.pallas_call(
        functools.partial(_sample_scores_kernel, pg=pg, t_new=t_new), name="sample_scores",
        grid_spec=pltpu.PrefetchScalarGridSpec(
            num_scalar_prefetch=1, grid=(dbatch, n_pages // pg),
            in_specs=[qspec, qspec] + page_specs,
            out_specs=pl.BlockSpec((1, t_new, pg * PAGE_SIZE), lambda b, s, pt: (b, 0, s))),
        out_shape=jax.ShapeDtypeStruct((dbatch, t_new, n_pages * PAGE_SIZE), f32),
        compiler_params=_params(2),
    )(page_table, qs, wb, *([cache_ki] * pg))
    spec3 = lambda d1, d2: pl.BlockSpec((1, d1, d2), lambda b: (b, 0, 0))
    tail = pl.pallas_call(
        functools.partial(_sample_scores_tail_kernel, t_new=t_new), name="sample_scores_tail",
        grid=(dbatch,),
        in_specs=[spec3(rows, IDX_DIM), spec3(rows, IDX_DIM), spec3(PAGE_SIZE, IDX_DIM)],
        out_specs=spec3(t_new, PAGE_SIZE),
        out_shape=jax.ShapeDtypeStruct((dbatch, t_new, PAGE_SIZE), f32),
        compiler_params=_params(1),
    )(qs, wb, ki_tail)
    return jnp.concatenate([main, tail], axis=2)


def _select_sample_kernel(s_ref, o_ref, key_ref, *, past, t_new, k_sel):
    rows, n = s_ref.shape
    t = past + lax.broadcasted_iota(i32, (rows, 1), 0) % t_new
    l = lax.broadcasted_iota(i32, (1, n), 1)
    key_ref[...] = jnp.where(l <= t, _sortable_key(s_ref[...]), INT_MIN)

    def count_ge(cand):
        return jnp.where(key_ref[...] >= cand, 1, 0).astype(i32).sum(axis=1, keepdims=True)

    thr = jnp.maximum(_kth_largest_key(count_ge, (rows, 1), k_sel), INT_MIN + 1)
    o_ref[...] = jnp.where(key_ref[...] >= thr, 0.0, NEG_INF).astype(f32)


def _select_sample(scores, past, t_new, k_sel):
    return pl.pallas_call(
        functools.partial(_select_sample_kernel, past=past, t_new=t_new, k_sel=k_sel), name="select_sample",
        out_shape=jax.ShapeDtypeStruct(scores.shape, f32),
        scratch_shapes=[pltpu.VMEM(scores.shape, i32)],
        compiler_params=pltpu.CompilerParams(vmem_limit_bytes=VMEM_LIMIT),
    )(scores)


def _attend_sample_a_kernel(pt_ref, q_ref, bias_ref, mask_ref, biast_ref, maskt_ref, gate_ref, *refs, pg, kv, scale):
    kpages, vpages = refs[:pg], refs[pg:2 * pg]
    kt_ref, vt_ref, o_ref, m_ref, l_ref, acc_ref = refs[2 * pg:]
    s_id = pl.program_id(1)

    @pl.when(s_id == 0)
    def _():
        m_ref[...] = jnp.full(m_ref.shape, M_INIT, f32)
        l_ref[...] = jnp.zeros(l_ref.shape, f32)
        acc_ref[...] = jnp.zeros(acc_ref.shape, f32)

    def page(kp, vp, bias, mask):
        for kh in range(kv):
            cols = slice(kh * HEAD_DIM, (kh + 1) * HEAD_DIM)
            s = _nt_dot(q_ref[0, kh], kp[:, cols].astype(bf16)) * scale + bias[kh] + mask
            m_old = m_ref[kh]
            m_new = jnp.maximum(m_old, s.max(axis=1, keepdims=True))
            alpha = jnp.exp(m_old - m_new)
            p = jnp.exp(s - m_new)
            l_ref[kh] = alpha * l_ref[kh] + p.sum(axis=1, keepdims=True)
            acc_ref[kh] = alpha * acc_ref[kh] + jnp.dot(p.astype(bf16), vp[:, cols].astype(bf16),
                                                        preferred_element_type=f32)
            m_ref[kh] = m_new

    for r in range(pg):
        lanes = slice(r * PAGE_SIZE, (r + 1) * PAGE_SIZE)
        page(kpages[r][0], vpages[r][0], bias_ref[:, :, lanes], mask_ref[0, :, lanes])

    @pl.when(s_id == pl.num_programs(1) - 1)
    def _():
        page(kt_ref[0], vt_ref[0], biast_ref[...], maskt_ref[0])
        o_ref[0] = acc_ref[...] / l_ref[...] * _silu(gate_ref[0])


def _attend_sample_a(page_table, q16, bias_tab, mask16, cache_k, cache_v, k_tail, v_tail, gate16):
    dbatch, kv, rows, _ = q16.shape
    n_pages = page_table.shape[1]
    past = n_pages * PAGE_SIZE
    pg = _pick_tile(n_pages, (4, 2, 1))
    width = kv * HEAD_DIM
    qspec = pl.BlockSpec((1, kv, rows, HEAD_DIM), lambda b, s, pt: (b, 0, 0, 0))
    kspecs = [pl.BlockSpec((1, PAGE_SIZE, width), lambda b, s, pt, r=r: (pt[b, s * pg + r], 0, 0))
              for r in range(pg)]
    tailspec = pl.BlockSpec((1, PAGE_SIZE, width), lambda b, s, pt: (b, 0, 0))
    return pl.pallas_call(
        functools.partial(_attend_sample_a_kernel, pg=pg, kv=kv, scale=HEAD_DIM ** -0.5), name="attend_sample_a",
        grid_spec=pltpu.PrefetchScalarGridSpec(
            num_scalar_prefetch=1, grid=(dbatch, n_pages // pg),
            in_specs=[qspec,
                      pl.BlockSpec((kv, rows, pg * PAGE_SIZE), lambda b, s, pt: (0, 0, s)),
                      pl.BlockSpec((1, rows, pg * PAGE_SIZE), lambda b, s, pt: (b, 0, s)),
                      pl.BlockSpec((kv, rows, PAGE_SIZE), lambda b, s, pt: (0, 0, past // PAGE_SIZE)),
                      pl.BlockSpec((1, rows, PAGE_SIZE), lambda b, s, pt: (b, 0, past // PAGE_SIZE)),
                      qspec] + kspecs + kspecs + [tailspec, tailspec],
            out_specs=qspec,
            scratch_shapes=[pltpu.VMEM((kv, rows, 1), f32), pltpu.VMEM((kv, rows, 1), f32),
                            pltpu.VMEM((kv, rows, HEAD_DIM), f32)]),
        out_shape=jax.ShapeDtypeStruct(q16.shape, f32),
        compiler_params=_params(2),
    )(page_table, q16, bias_tab, mask16, bias_tab, mask16, gate16,
      *([cache_k] * pg), *([cache_v] * pg), k_tail, v_tail)


def _attend_prompt_b_kernel(q0_ref, q1_ref, q2_ref, k_ref, v_ref, gate_ref, gb_ref, o_ref,
                            qf_ref, og_ref, lse_ref, *, rates, scale):
    seq = k_ref.shape[0]
    col = lax.broadcasted_iota(i32, (BLK, 2 * BLK), 1)

    for g, (q_ref, d) in enumerate(zip((q0_ref, q1_ref, q2_ref), rates)):
        qf_ref[...] = q_ref[...].astype(f32)
        nb = seq // d // BLK
        bias = gb_ref[0, :, g * 2 * BLK:(g + 1) * 2 * BLK]

        def blk_body(it, carry, g=g, d=d, nb=nb, bias=bias):
            r = it // nb
            blk = it % nb
            start = r + blk * (BLK * d)
            start_prev = r + jnp.maximum(blk - 1, 0) * (BLK * d)
            cur = pl.ds(start, BLK, stride=d)
            prev = pl.ds(start_prev, BLK, stride=d)
            qb = qf_ref[cur, :].astype(bf16)
            kcat = jnp.concatenate([k_ref[prev, :], k_ref[cur, :]], axis=0).astype(bf16)
            vcat = jnp.concatenate([v_ref[prev, :], v_ref[cur, :]], axis=0).astype(bf16)
            s = _nt_dot(qb, kcat) * scale + bias
            s = jnp.where((col >= BLK) | (blk > 0), s, NEG_INF)
            m = s.max(axis=1, keepdims=True)
            e = jnp.exp(s - m)
            ssum = e.sum(axis=1, keepdims=True)
            og_ref[g, cur, :] = jnp.dot(e.astype(bf16), vcat, preferred_element_type=f32) / ssum
            lse_ref[g, cur, :] = jnp.broadcast_to(m + jnp.log(ssum), (BLK, HEAD_DIM))
            return carry

        lax.fori_loop(0, d * nb, blk_body, 0)

    def merge_body(cb, carry):
        rows = pl.ds(pl.multiple_of(cb * BLK, BLK), BLK)
        lses = [lse_ref[g, rows, :] for g in range(len(rates))]
        top = functools.reduce(jnp.maximum, lses)
        wts = [jnp.exp(x - top) for x in lses]
        num = sum(w * og_ref[g, rows, :] for g, w in enumerate(wts))
        gt = gate_ref[rows, :]
        o_ref[rows, :] = (num / sum(wts) * _silu(gt)).astype(o_ref.dtype)
        return carry

    lax.fori_loop(0, seq // BLK, merge_body, 0)


def _attend_prompt_b(q, k, v, gate, bias_tab, batch, seq):
    n_heads = k.shape[1] // HEAD_DIM
    n_dil = len(DIL_RATES)
    blk = (seq, HEAD_DIM)
    qspecs = [pl.BlockSpec(blk, lambda b, h, g=g: (b, g * n_heads + h)) for g in range(n_dil)]
    hspec = pl.BlockSpec(blk, lambda b, h: (b, h))
    return pl.pallas_call(
        functools.partial(_attend_prompt_b_kernel, rates=DIL_RATES, scale=HEAD_DIM ** -0.5),
        name="attend_prompt_b", grid=(batch, n_heads),
        in_specs=qspecs + [hspec, hspec, hspec,
                           pl.BlockSpec((1, BLK, bias_tab.shape[2]), lambda b, h: (h, 0, 0))],
        out_specs=hspec,
        out_shape=jax.ShapeDtypeStruct(k.shape, bf16),
        scratch_shapes=[pltpu.VMEM(blk, f32), pltpu.VMEM((n_dil,) + blk, f32), pltpu.VMEM((n_dil,) + blk, f32)],
        compiler_params=_params(2),
    )(q, q, q, k, v, gate, bias_tab)


def _attend_sample_b_kernel(q_ref, k_ref, v_ref, kt_ref, vt_ref, bias_ref, gate_ref, o_ref, *, n_dil, t_new, scale):
    wb = k_ref.shape[1]
    q = q_ref[0, 0]
    s_main = _nt_dot(q, k_ref[0].astype(bf16)) * scale + bias_ref[0, :, :wb]
    s_tail = _nt_dot(q, kt_ref[0].astype(bf16)) * scale + bias_ref[0, :, wb:]
    m = jnp.maximum(s_main.max(axis=1, keepdims=True), s_tail.max(axis=1, keepdims=True))
    e_main = jnp.exp(s_main - m)
    e_tail = jnp.exp(s_tail - m)
    ssum = e_main.sum(axis=1, keepdims=True) + e_tail.sum(axis=1, keepdims=True)
    o = (jnp.dot(e_main.astype(bf16), v_ref[0].astype(bf16), preferred_element_type=f32)
         + jnp.dot(e_tail.astype(bf16), vt_ref[0].astype(bf16), preferred_element_type=f32)) / ssum
    lse = m + jnp.log(ssum)
    lses = [lse[g * t_new:(g + 1) * t_new] for g in range(n_dil)]
    top = functools.reduce(jnp.maximum, lses)
    wts = [jnp.exp(x - top) for x in lses]
    num = sum(w * o[g * t_new:(g + 1) * t_new] for g, w in enumerate(wts))
    o_ref[0, 0] = num / sum(wts) * _silu(gate_ref[0, 0])


def _attend_sample_b(q16, buf_k, buf_v, k_tail, v_tail, bias_tab, gate4, t_new):
    dbatch, n_heads, rows, _ = q16.shape
    wb = buf_k.shape[1]
    bufspec = pl.BlockSpec((1, wb, HEAD_DIM), lambda b, h: (b, 0, h))
    tailspec = pl.BlockSpec((1, LANE, HEAD_DIM), lambda b, h: (b, 0, h))
    return pl.pallas_call(
        functools.partial(_attend_sample_b_kernel, n_dil=len(DIL_RATES), t_new=t_new, scale=HEAD_DIM ** -0.5),
        name="attend_sample_b", grid=(dbatch, n_heads),
        in_specs=[pl.BlockSpec((1, 1, rows, HEAD_DIM), lambda b, h: (b, h, 0, 0)),
                  bufspec, bufspec, tailspec, tailspec,
                  pl.BlockSpec((1, rows, wb + LANE), lambda b, h: (h, 0, 0)),
                  pl.BlockSpec((1, 1, t_new, HEAD_DIM), lambda b, h: (b, h, 0, 0))],
        out_specs=pl.BlockSpec((1, 1, t_new, HEAD_DIM), lambda b, h: (b, h, 0, 0)),
        out_shape=jax.ShapeDtypeStruct((dbatch, n_heads, t_new, HEAD_DIM), f32),
        compiler_params=_params(2),
    )(q16, buf_k, buf_v, k_tail, v_tail, bias_tab, gate4)


def _shift_kernel(bk_ref, bv_ref, nk_ref, nv_ref, ok_ref, ov_ref, sems, *, dbatch, rows, shift):
    copies = []
    for b in range(dbatch):
        for a, (src, new, dst) in enumerate(((bk_ref, nk_ref, ok_ref), (bv_ref, nv_ref, ov_ref))):
            copies.append(pltpu.make_async_copy(src.at[b, pl.ds(shift, rows - shift)],
                                                dst.at[b, pl.ds(0, rows - shift)], sems.at[a, 0, b]))
            copies.append(pltpu.make_async_copy(new.at[b], dst.at[b, pl.ds(rows - shift, shift)], sems.at[a, 1, b]))
    for c in copies:
        c.start()
    for c in copies:
        c.wait()


def _shift_buffers(buf_k, buf_v, new_k, new_v):
    dbatch, rows, _ = buf_k.shape
    shift = new_k.shape[1]
    anyspec = pl.BlockSpec(memory_space=pl.ANY)
    out = jax.ShapeDtypeStruct(buf_k.shape, buf_k.dtype)
    return pl.pallas_call(
        functools.partial(_shift_kernel, dbatch=dbatch, rows=rows, shift=shift), name="shift_buffers",
        in_specs=[anyspec] * 4, out_specs=[anyspec] * 2, out_shape=[out, out],
        scratch_shapes=[pltpu.SemaphoreType.DMA((2, 2, dbatch))],
    )(buf_k, buf_v, new_k, new_v)


def _offsets(sizes):
    return [int(x) for x in np.cumsum((0,) + tuple(sizes))[:-1]]


def _pad_rows(a, rows):
    return jnp.pad(a, ((0, 0), (0, rows - a.shape[1]), (0, 0)))


def _layer_a(xp, xs, dims, cache_k, cache_v, cache_ki, page_table, w_in, w_out, ln_g, ln_b, rel_bias):
    batch, seq, dbatch, t_new = dims
    d_model = xp.shape[1]
    n_heads = d_model // HEAD_DIM
    kv, hi = KV_HEADS_A, IDX_HEADS
    group = n_heads // kv
    branch = n_heads * HEAD_DIM
    sizes = (branch, kv * HEAD_DIM, kv * HEAD_DIM, branch, hi * IDX_DIM, IDX_DIM, hi)
    offs = _offsets(sizes)
    tail_w = 2 * LANE
    w_tail = jnp.pad(w_in[:, offs[5]:], ((0, 0), (0, tail_w - IDX_DIM - hi)))

    def project(x):
        xb = x.astype(bf16)
        q = _matmul(xb, w_in, offs[0], sizes[0], bf16)
        k = _matmul(xb, w_in, offs[1], sizes[1], f32)
        v = _matmul(xb, w_in, offs[2], sizes[2], f32)
        gate = _matmul(xb, w_in, offs[3], sizes[3], f32)
        qi = _matmul(xb, w_in, offs[4], sizes[4], bf16)
        kw = _matmul(xb, w_tail, 0, tail_w, f32)
        return q, k, v, gate, qi, kw[:, :IDX_DIM], kw[:, IDX_DIM:IDX_DIM + hi]

    q, k_p, v_p, gate, qi, ki_p, wi = project(xp)
    nblk = seq // BLK
    dist = (np.arange(BLK)[:, None] + (nblk - 1) * BLK) - np.arange(seq + BLK)[None, :]
    bias_p = _bias_table(rel_bias, _bucket_codes(dist))
    maskadd = _select_prompt(qi, ki_p, wi.T, batch, seq, min(TOPK_MAX, seq // 4))
    og = _attend_prompt(q, k_p, v_p, maskadd, bias_p, gate, batch, seq)
    xp_new = _layernorm(_matmul(og, w_out, 0, d_model, f32), xp, ln_g, ln_b)

    q, k_s, v_s, gate, qi, ki_s, wi = project(xs)
    n_pages = page_table.shape[1]
    past = n_pages * PAGE_SIZE
    n_keys = past + t_new
    n_lanes = past + PAGE_SIZE
    cscale = (IDX_DIM ** -0.5) * (hi ** -0.5)
    qs = qi.reshape(dbatch, t_new * hi, IDX_DIM)
    wbc = jnp.broadcast_to((wi * cscale).reshape(dbatch, t_new * hi, 1), (dbatch, t_new * hi, LANE))
    ki_tail = _pad_rows(ki_s.reshape(dbatch, t_new, IDX_DIM), PAGE_SIZE)
    scores = _sample_scores(page_table, qs, wbc, cache_ki, ki_tail, t_new)
    mask_s = _select_sample(scores.reshape(dbatch * t_new, n_lanes), past, t_new, min(TOPK_MAX, n_keys // 4))

    def to_rows(a):
        a = a.reshape(dbatch, t_new, kv, group, HEAD_DIM)
        return a.transpose(0, 2, 3, 1, 4).reshape(dbatch, kv, group * t_new, HEAD_DIM)

    dist = (past + np.arange(t_new)[:, None]) - np.arange(n_lanes)[None, :]
    bias_s = _bias_table(rel_bias, _bucket_codes(dist)).reshape(kv, group * t_new, n_lanes)
    mask16 = jnp.broadcast_to(mask_s.reshape(dbatch, 1, t_new, n_lanes), (dbatch, group, t_new, n_lanes))
    mask16 = mask16.reshape(dbatch, group * t_new, n_lanes)
    width = kv * HEAD_DIM
    k_tail = _pad_rows(k_s.reshape(dbatch, t_new, width), PAGE_SIZE)
    v_tail = _pad_rows(v_s.reshape(dbatch, t_new, width), PAGE_SIZE)
    og16 = _attend_sample_a(page_table, to_rows(q), bias_s, mask16,
                            cache_k.reshape(-1, PAGE_SIZE, width), cache_v.reshape(-1, PAGE_SIZE, width),
                            k_tail, v_tail, to_rows(gate))
    og_s = og16.reshape(dbatch, kv, group, t_new, HEAD_DIM).transpose(0, 3, 1, 2, 4)
    og_s = og_s.reshape(dbatch * t_new, branch).astype(bf16)
    xs_new = _layernorm(_matmul(og_s, w_out, 0, d_model, f32), xs, ln_g, ln_b)

    shp = lambda a, b_, t_, *rest: a.reshape(b_, t_, *rest)
    outs = (shp(k_p, batch, seq, kv, HEAD_DIM), shp(v_p, batch, seq, kv, HEAD_DIM), shp(ki_p, batch, seq, IDX_DIM),
            shp(k_s, dbatch, t_new, kv, HEAD_DIM), shp(v_s, dbatch, t_new, kv, HEAD_DIM),
            shp(ki_s, dbatch, t_new, IDX_DIM))
    return xp_new, xs_new, outs


def _layer_b(xp, xs, dims, buf_k, buf_v, w_in, w_out, ln_g, ln_b, rel_bias):
    batch, seq, dbatch, t_new = dims
    d_model = xp.shape[1]
    n_heads = d_model // HEAD_DIM
    branch = n_heads * HEAD_DIM
    n_dil = len(DIL_RATES)
    sizes = (n_dil * branch, branch, branch, branch)
    offs = _offsets(sizes)

    def project(x, q_dtype):
        xb = x.astype(bf16)
        return (_matmul(xb, w_in, offs[0], sizes[0], q_dtype), _matmul(xb, w_in, offs[1], sizes[1], f32),
                _matmul(xb, w_in, offs[2], sizes[2], f32), _matmul(xb, w_in, offs[3], sizes[3], f32))

    q, k_p, v_p, gate = project(xp, bf16)
    m = (np.arange(BLK)[:, None] + BLK) - np.arange(2 * BLK)[None, :]
    codes = [_bucket_codes(d * np.clip(m, 0, w // d), (m >= 0) & (m <= w // d))
             for d, w in zip(DIL_RATES, DIL_WINDOWS)]
    bias_p = _bias_table(rel_bias, np.concatenate(codes, axis=1))
    og = _attend_prompt_b(q, k_p, v_p, gate, bias_p, batch, seq)
    xp_new = _layernorm(_matmul(og, w_out, 0, d_model, f32), xp, ln_g, ln_b)

    q, k_s, v_s, gate = project(xs, bf16)
    wb = buf_k.shape[1]
    rows = -(-(n_dil * t_new) // 16) * 16
    pos = np.arange(wb + LANE)[None, :]
    codes = []
    for d, w in zip(DIL_RATES, DIL_WINDOWS):
        dist = wb + np.arange(t_new)[:, None] - pos
        codes.append(_bucket_codes(dist, (dist >= 0) & (dist % d == 0) & (dist // d <= w // d) & (pos < wb + t_new)))
    codes.append(np.full((rows - n_dil * t_new, wb + LANE), BUCKET_ZERO, np.int32))
    bias_s = _bias_table(rel_bias, np.concatenate(codes, axis=0))
    q16 = q.reshape(dbatch, t_new, n_dil, n_heads, HEAD_DIM).transpose(0, 3, 2, 1, 4)
    q16 = q16.reshape(dbatch, n_heads, n_dil * t_new, HEAD_DIM)
    q16 = jnp.pad(q16, ((0, 0), (0, 0), (0, rows - n_dil * t_new), (0, 0)))
    gate4 = gate.reshape(dbatch, t_new, n_heads, HEAD_DIM).transpose(0, 2, 1, 3)
    k_tail = _pad_rows(k_s.reshape(dbatch, t_new, branch), LANE)
    v_tail = _pad_rows(v_s.reshape(dbatch, t_new, branch), LANE)
    og4 = _attend_sample_b(q16, buf_k.reshape(dbatch, wb, branch), buf_v.reshape(dbatch, wb, branch),
                           k_tail, v_tail, bias_s, gate4, t_new)
    og_s = og4.transpose(0, 2, 1, 3).reshape(dbatch * t_new, branch).astype(bf16)
    xs_new = _layernorm(_matmul(og_s, w_out, 0, d_model, f32), xs, ln_g, ln_b)

    new_k, new_v = _shift_buffers(buf_k.reshape(dbatch, wb * n_heads, HEAD_DIM),
                                  buf_v.reshape(dbatch, wb * n_heads, HEAD_DIM),
                                  k_s.reshape(dbatch, t_new * n_heads, HEAD_DIM),
                                  v_s.reshape(dbatch, t_new * n_heads, HEAD_DIM))
    wp = min(W_MAX, seq)
    k_win = k_p.reshape(batch, seq, n_heads, HEAD_DIM)[:, seq - wp:]
    v_win = v_p.reshape(batch, seq, n_heads, HEAD_DIM)[:, seq - wp:]
    outs = (k_win, v_win, new_k.reshape(dbatch, wb, n_heads, HEAD_DIM), new_v.reshape(dbatch, wb, n_heads, HEAD_DIM))
    return xp_new, xs_new, outs


def kernel(x_prompt, x_sample, cache_k_a, cache_v_a, cache_kidx_a, cache_k_b, cache_v_b, page_table, rel_bias,
           w_in_a, w_out_a, ln_g_a, ln_b_a, w_in_b, w_out_b, ln_g_b, ln_b_b):
    batch, seq, d_model = x_prompt.shape
    dbatch, t_new, _ = x_sample.shape
    dims = (batch, seq, dbatch, t_new)
    xp = x_prompt.reshape(batch * seq, d_model)
    xs = x_sample.reshape(dbatch * t_new, d_model)
    outs_a, outs_b = [], []
    for layer in range(DEPTH):
        j = layer // 2
        if layer % 2 == 0:
            xp, xs, o = _layer_a(xp, xs, dims, cache_k_a[j], cache_v_a[j], cache_kidx_a[j], page_table,
                                 w_in_a[j], w_out_a[j], ln_g_a[j], ln_b_a[j], rel_bias)
            outs_a.append(o)
        else:
            xp, xs, o = _layer_b(xp, xs, dims, cache_k_b[j], cache_v_b[j],
                                 w_in_b[j], w_out_b[j], ln_g_b[j], ln_b_b[j], rel_bias)
            outs_b.append(o)
    stack = lambda group, idx: jnp.stack([o[idx] for o in group])
    return (xp.reshape(batch, seq, d_model), xs.reshape(dbatch, t_new, d_model),
            *(stack(outs_a, n) for n in range(6)), *(stack(outs_b, n) for n in range(4)))
```

```python
import functools
import math

import jax
import jax.numpy as jnp
import numpy as np
from jax import lax
from jax.experimental import pallas as pl
from jax.experimental.pallas import tpu as pltpu

HEAD_DIM = 128
KV_HEADS_A = 8
IDX_HEADS = 64
IDX_DIM = 128
TOPK_MAX = 256
DIL_WINDOWS = (128, 512, 2048)
DIL_RATES = (1, 4, 16)
W_MAX = max(DIL_WINDOWS)
BLK = 128
N_BUCKETS = 32
REL_MAX_DIST = 2048
DEPTH = 2
ALPHA = (2 * DEPTH) ** 0.25
LN_EPS = 1e-5
PAGE_SIZE = 128

LANE = 128
NEG_INF = float("-inf")
INT_MIN = -(2 ** 31)
M_INIT = -1e30
BUCKET_ZERO = -1
BUCKET_MASKED = -2
VMEM_LIMIT = 56 * 1024 * 1024

f32 = jnp.float32
bf16 = jnp.bfloat16
i32 = jnp.int32


def _params(n_axes, vmem=VMEM_LIMIT):
    return pltpu.CompilerParams(dimension_semantics=("arbitrary",) * n_axes, vmem_limit_bytes=vmem)


def _nt_dot(a, b):
    return lax.dot_general(a, b, (((1,), (1,)), ((), ())), preferred_element_type=f32)


def _silu(x):
    return x * (1.0 / (1.0 + jnp.exp(-x)))


def _softmax_step(s, m_old, l_old):
    m_new = jnp.maximum(m_old, s.max(axis=1, keepdims=True))
    alpha = jnp.exp(m_old - m_new)
    p = jnp.exp(s - m_new)
    return m_new, alpha * l_old + p.sum(axis=1, keepdims=True), alpha, p


def _mm_kernel(x_ref, w_ref, o_ref, wbf_ref, *, k_chunk):
    @pl.when(pl.program_id(1) == 0)
    def _():
        def body(r, c):
            rows = pl.ds(pl.multiple_of(r * k_chunk, k_chunk), k_chunk)
            wbf_ref[rows, :] = w_ref[rows, :].astype(bf16)
            return c
        lax.fori_loop(0, w_ref.shape[0] // k_chunk, body, 0)

    o_ref[...] = jnp.dot(x_ref[...], wbf_ref[...], preferred_element_type=f32).astype(o_ref.dtype)


def _pick_tile(n, candidates):
    for c in candidates:
        if n % c == 0:
            return c
    raise ValueError(f"no tile for {n}")


def _matmul(x_bf, w, layer, col_off, ncols, out_dtype, name):
    m, k = x_bf.shape
    tn = _pick_tile(math.gcd(col_off, ncols) if col_off else ncols, (512, 256, 128))
    tm = _pick_tile(m, (1024, 512, 256, 128, 64, 32, 16))
    off_blocks = col_off // tn
    k_chunk = _pick_tile(k, (256, 128))
    return pl.pallas_call(
        functools.partial(_mm_kernel, k_chunk=k_chunk), name=name,
        grid=(ncols // tn, m // tm),
        in_specs=[pl.BlockSpec((tm, k), lambda n, i: (i, 0)),
                  pl.BlockSpec((None, k, tn), lambda n, i: (layer, 0, n + off_blocks))],
        out_specs=pl.BlockSpec((tm, tn), lambda n, i: (i, n)),
        out_shape=jax.ShapeDtypeStruct((m, ncols), out_dtype),
        scratch_shapes=[pltpu.VMEM((k, tn), bf16)],
        compiler_params=_params(2),
    )(x_bf, w)


def _ln_kernel(y_ref, x_ref, g_ref, b_ref, o_ref):
    z = ALPHA * x_ref[...] + y_ref[...]
    mu = jnp.mean(z, axis=-1, keepdims=True)
    zc = z - mu
    var = jnp.mean(zc * zc, axis=-1, keepdims=True)
    o_ref[...] = zc * lax.rsqrt(var + LN_EPS) * g_ref[...] + b_ref[...]


def _layernorm(y, x, g, b):
    m, d = x.shape
    tm = _pick_tile(m, (128, 64, 32, 16, 8))
    row = pl.BlockSpec((tm, d), lambda i: (i, 0))
    vec = pl.BlockSpec((1, d), lambda i: (0, 0))
    return pl.pallas_call(
        _ln_kernel, grid=(m // tm,), name="layernorm",
        in_specs=[row, row, vec, vec], out_specs=row,
        out_shape=jax.ShapeDtypeStruct((m, d), f32),
        compiler_params=_params(1),
    )(y, x, g.reshape(1, d), b.reshape(1, d))


def _rel_bucket_np(n):
    max_exact = N_BUCKETS // 2
    nf = np.maximum(n, 1).astype(np.float64)
    large = max_exact + (np.log(nf / max_exact) / math.log(REL_MAX_DIST / max_exact)
                         * (N_BUCKETS - max_exact)).astype(np.int64)
    large = np.minimum(large, N_BUCKETS - 1)
    return np.where(n < max_exact, n, large).astype(np.int32)


def _bucket_codes(dist, valid=None):
    code = np.where(dist >= 0, _rel_bucket_np(np.maximum(dist, 0)), BUCKET_ZERO)
    if valid is not None:
        code = np.where(valid, code, BUCKET_MASKED)
    return code.astype(np.int32)


def _bias_kernel(tbl_ref, bm_ref, o_ref, *, cw):
    h = pl.program_id(0)

    def body(c, carry):
        cols = pl.ds(pl.multiple_of(c * cw, LANE), cw)
        bm = bm_ref[:, cols]
        acc = jnp.where(bm == BUCKET_MASKED, NEG_INF, 0.0).astype(f32)
        for bk in range(N_BUCKETS):
            acc = jnp.where(bm == bk, tbl_ref[bk, h], acc)
        o_ref[0, :, cols] = acc
        return carry

    lax.fori_loop(0, bm_ref.shape[1] // cw, body, 0)


def _bias_table(rel_bias, codes):
    r, c = codes.shape
    n_heads = rel_bias.shape[1]
    units = c // LANE
    per = max(1, 16 // -(-r // 8))
    cw = LANE * max(u for u in range(1, units + 1) if units % u == 0 and u <= per)
    return pl.pallas_call(
        functools.partial(_bias_kernel, cw=cw), name="bias_table",
        grid=(n_heads,),
        in_specs=[pl.BlockSpec(memory_space=pltpu.SMEM),
                  pl.BlockSpec((r, c), lambda h: (0, 0))],
        out_specs=pl.BlockSpec((1, r, c), lambda h: (h, 0, 0)),
        out_shape=jax.ShapeDtypeStruct((n_heads, r, c), f32),
        compiler_params=_params(1),
    )(rel_bias, jnp.asarray(codes))


def _sortable_key(x):
    bits = pltpu.bitcast(x, i32)
    return bits ^ ((bits >> 31) & 0x7FFFFFFF)


def _kth_largest_key(count_ge, shape, k_sel):
    def body(it, t):
        cand = t + lax.shift_left(jnp.int32(1), 31 - it)
        return jnp.where(count_ge(cand) >= k_sel, cand, t)
    return lax.fori_loop(0, 32, body, jnp.full(shape, INT_MIN, i32))


def _select_prompt_kernel(qi_ref, ki_ref, wt_ref, m_ref, kibf_ref, key_ref, *, n_idx_heads, k_sel, ck, hb, cscale):
    i = pl.program_id(1)
    tq = qi_ref.shape[0]

    @pl.when(i == 0)
    def _():
        kibf_ref[...] = ki_ref[...].astype(bf16)

    nck = ((i + 1) * tq + ck - 1) // ck
    t_row = i * tq + lax.broadcasted_iota(i32, (1, tq), 1)

    def chunk_body(c, carry):
        koff = pl.multiple_of(c * ck, ck)
        kc = kibf_ref[pl.ds(koff, ck), :]

        def heads_body(hg, acc):
            hoff = hg * (hb * IDX_DIM)
            qstack = jnp.concatenate(
                [qi_ref[:, pl.ds(pl.multiple_of(hoff + j * IDX_DIM, IDX_DIM), IDX_DIM)] for j in range(hb)], axis=0)
            s = _nt_dot(kc, qstack)
            w = wt_ref[pl.ds(pl.multiple_of(hg * hb, hb), hb), :] * cscale
            for j in range(hb):
                acc = acc + jnp.maximum(s[:, j * tq:(j + 1) * tq], 0.0) * w[j:j + 1, :]
            return acc

        acc = lax.fori_loop(0, n_idx_heads // hb, heads_body, jnp.zeros((ck, tq), f32), unroll=2)
        l_col = koff + lax.broadcasted_iota(i32, (ck, 1), 0)
        key_ref[pl.ds(koff, ck), :] = jnp.where(l_col <= t_row, _sortable_key(acc), INT_MIN)
        return carry

    lax.fori_loop(0, nck, chunk_body, 0)

    def count_ge(cand):
        def body(c, cnt):
            kc = key_ref[pl.ds(pl.multiple_of(c * ck, ck), ck), :]
            hit = jnp.where(kc >= cand, 1, 0).astype(i32)
            return cnt + hit.reshape(ck // 8, 8, tq).sum(axis=0)
        cnt = lax.fori_loop(0, nck, body, jnp.zeros((8, tq), i32))
        return cnt.sum(axis=0, keepdims=True)

    thr = _kth_largest_key(count_ge, (1, tq), k_sel)
    thr = jnp.maximum(thr, INT_MIN + 1)

    m_ref[...] = jnp.full(m_ref.shape, NEG_INF, m_ref.dtype)

    def out_body(c, carry):
        koff = pl.multiple_of(c * ck, ck)
        sel = jnp.where(key_ref[pl.ds(koff, ck), :] >= thr, 0.0, NEG_INF).astype(f32)
        m_ref[:, pl.ds(koff, ck)] = sel.T.astype(m_ref.dtype)
        return carry

    lax.fori_loop(0, nck, out_body, 0)


def _select_prompt(qi, ki, wi_t, batch, seq, k_sel):
    n_idx_heads = wi_t.shape[0]
    tq, ck = BLK, 256
    hb = _pick_tile(n_idx_heads, (8, 4, 2, 1))
    nblk = seq // tq
    cscale = (IDX_DIM ** -0.5) * (n_idx_heads ** -0.5)
    return pl.pallas_call(
        functools.partial(_select_prompt_kernel, n_idx_heads=n_idx_heads, k_sel=k_sel, ck=ck, hb=hb, cscale=cscale),
        name="select_prompt", grid=(batch, nblk),
        in_specs=[pl.BlockSpec((tq, n_idx_heads * IDX_DIM), lambda b, i: (b * nblk + i, 0)),
                  pl.BlockSpec((seq, IDX_DIM), lambda b, i: (b, 0)),
                  pl.BlockSpec((n_idx_heads, tq), lambda b, i: (0, b * nblk + i))],
        out_specs=pl.BlockSpec((tq, seq), lambda b, i: (b * nblk + i, 0)),
        out_shape=jax.ShapeDtypeStruct((batch * seq, seq), bf16),
        scratch_shapes=[pltpu.VMEM((seq, IDX_DIM), bf16), pltpu.VMEM((seq, tq), i32)],
        compiler_params=_params(2),
    )(qi, ki, wi_t)


def _attend_prompt_kernel(q_ref, k_ref, v_ref, mk_ref, g_ref, gate_ref, o_ref,
                          kbf_ref, vbf_ref, s_ref, l_ref, acc_ref, *, group, ck, sub, scale):
    i = pl.program_id(2)
    nblk = pl.num_programs(2)
    tq = q_ref.shape[0]
    rows = group * tq

    @pl.when(i == 0)
    def _():
        kbf_ref[...] = k_ref[...].astype(bf16)
        vbf_ref[...] = v_ref[...].astype(bf16)

    nck = ((i + 1) * tq + ck - 1) // ck
    goff0 = (nblk - 1 - i) * tq
    qs = jnp.concatenate([q_ref[:, g * HEAD_DIM:(g + 1) * HEAD_DIM] for g in range(group)], axis=0)

    def logits_body(c, mpart):
        for u in range(ck // sub):
            koff = pl.multiple_of(c * ck + u * sub, sub)
            goff = pl.multiple_of(goff0 + koff, LANE)
            s = _nt_dot(qs, kbf_ref[pl.ds(koff, sub), :]) * scale
            bias = jnp.concatenate([g_ref[g, :, pl.ds(goff, sub)] for g in range(group)], axis=0)
            sel = mk_ref[:, pl.ds(koff, sub)].astype(f32)
            s = s + bias + jnp.concatenate([sel] * group, axis=0)
            s_ref[:, pl.ds(koff, sub)] = s
            for t in range(sub // LANE):
                mpart = jnp.maximum(mpart, s[:, t * LANE:(t + 1) * LANE])
        return mpart

    mpart = lax.fori_loop(0, nck, logits_body, jnp.full((rows, LANE), M_INIT, f32))
    m_all = jnp.broadcast_to(mpart.max(axis=1, keepdims=True), (rows, LANE))

    l_ref[...] = jnp.zeros(l_ref.shape, f32)
    acc_ref[...] = jnp.zeros(acc_ref.shape, f32)

    def pv_body(c, carry):
        for u in range(ck // sub):
            koff = pl.multiple_of(c * ck + u * sub, sub)
            ps = [jnp.exp(s_ref[:, pl.ds(pl.multiple_of(koff + t * LANE, LANE), LANE)] - m_all)
                  for t in range(sub // LANE)]
            l_ref[...] += functools.reduce(jnp.add, ps)
            p = jnp.concatenate(ps, axis=1).astype(bf16)
            acc_ref[...] += jnp.dot(p, vbf_ref[pl.ds(koff, sub), :], preferred_element_type=f32)
        return carry

    lax.fori_loop(0, nck, pv_body, 0)
    o = acc_ref[...] / l_ref[...].sum(axis=1, keepdims=True)
    for g in range(group):
        cols = slice(g * HEAD_DIM, (g + 1) * HEAD_DIM)
        o_ref[:, cols] = (o[g * tq:(g + 1) * tq] * _silu(gate_ref[:, cols])).astype(o_ref.dtype)


ATTN_A_CHUNK = 512


def _attend_prompt(q, k, v, maskadd, bias_tab, gate, batch, seq):
    kv = k.shape[1] // HEAD_DIM
    group = q.shape[1] // (kv * HEAD_DIM)
    tq, ck = BLK, ATTN_A_CHUNK
    nblk = seq // tq
    gw = group * HEAD_DIM
    qspec = pl.BlockSpec((tq, gw), lambda kh, b, i: (b * nblk + i, kh))
    kvspec = pl.BlockSpec((seq, HEAD_DIM), lambda kh, b, i: (b, kh))
    return pl.pallas_call(
        functools.partial(_attend_prompt_kernel, group=group, ck=ck, sub=256, scale=HEAD_DIM ** -0.5),
        name="attend_prompt_a", grid=(kv, batch, nblk),
        in_specs=[qspec, kvspec, kvspec,
                  pl.BlockSpec((tq, seq), lambda kh, b, i: (b * nblk + i, 0)),
                  pl.BlockSpec((group, tq, bias_tab.shape[2]), lambda kh, b, i: (kh, 0, 0)),
                  qspec],
        out_specs=qspec,
        out_shape=jax.ShapeDtypeStruct(q.shape, bf16),
        scratch_shapes=[pltpu.VMEM((seq, HEAD_DIM), bf16), pltpu.VMEM((seq, HEAD_DIM), bf16),
                        pltpu.VMEM((group * tq, seq), f32), pltpu.VMEM((group * tq, LANE), f32),
                        pltpu.VMEM((group * tq, HEAD_DIM), f32)],
        compiler_params=_params(3),
    )(q, k, v, maskadd, bias_tab, gate)


def _score_page(qs, wb, kpage, t_new):
    s = _nt_dot(qs, kpage.astype(bf16))
    x = jnp.maximum(s, 0.0) * wb
    return x.reshape(t_new, x.shape[0] // t_new, x.shape[1]).sum(axis=1)


def _sample_scores_kernel(pt_ref, qs_ref, wb_ref, *refs, pg, t_new):
    o_ref = refs[pg]
    for r in range(pg):
        o_ref[0, :, r * PAGE_SIZE:(r + 1) * PAGE_SIZE] = _score_page(qs_ref[0], wb_ref[0], refs[r][...], t_new)


def _sample_scores_tail_kernel(qs_ref, wb_ref, kt_ref, o_ref, *, t_new):
    o_ref[0] = _score_page(qs_ref[0], wb_ref[0], kt_ref[0], t_new)


def _sample_scores(page_table, qs, wb, cache_ki, layer, ki_tail, t_new):
    dbatch, rows, _ = qs.shape
    n_pages = page_table.shape[1]
    pg = _pick_tile(n_pages, (8, 4, 2, 1))
    qspec = pl.BlockSpec((1, rows, IDX_DIM), lambda b, s, pt: (b, 0, 0))
    page_specs = [pl.BlockSpec((None, None, PAGE_SIZE, IDX_DIM),
                               lambda b, s, pt, r=r: (layer, pt[b, s * pg + r], 0, 0)) for r in range(pg)]
    main = pl.pallas_call(
        functools.partial(_sample_scores_kernel, pg=pg, t_new=t_new), name="sample_scores",
        grid_spec=pltpu.PrefetchScalarGridSpec(
            num_scalar_prefetch=1, grid=(dbatch, n_pages // pg),
            in_specs=[qspec, qspec] + page_specs,
            out_specs=pl.BlockSpec((1, t_new, pg * PAGE_SIZE), lambda b, s, pt: (b, 0, s))),
        out_shape=jax.ShapeDtypeStruct((dbatch, t_new, n_pages * PAGE_SIZE), f32),
        compiler_params=_params(2),
    )(page_table, qs, wb, *([cache_ki] * pg))
    spec3 = lambda d1, d2: pl.BlockSpec((1, d1, d2), lambda b: (b, 0, 0))
    tail = pl.pallas_call(
        functools.partial(_sample_scores_tail_kernel, t_new=t_new), name="sample_scores_tail",
        grid=(dbatch,),
        in_specs=[spec3(rows, IDX_DIM), spec3(rows, IDX_DIM), spec3(PAGE_SIZE, IDX_DIM)],
        out_specs=spec3(t_new, PAGE_SIZE),
        out_shape=jax.ShapeDtypeStruct((dbatch, t_new, PAGE_SIZE), f32),
        compiler_params=_params(1),
    )(qs, wb, ki_tail)
    return jnp.concatenate([main, tail], axis=2)


def _select_sample_kernel(s_ref, o_ref, key_ref, *, past, t_new, k_sel):
    rows, n = s_ref.shape
    t = past + lax.broadcasted_iota(i32, (rows, 1), 0) % t_new
    l = lax.broadcasted_iota(i32, (1, n), 1)
    key_ref[...] = jnp.where(l <= t, _sortable_key(s_ref[...]), INT_MIN)

    def count_ge(cand):
        return jnp.where(key_ref[...] >= cand, 1, 0).astype(i32).sum(axis=1, keepdims=True)

    thr = jnp.maximum(_kth_largest_key(count_ge, (rows, 1), k_sel), INT_MIN + 1)
    o_ref[...] = jnp.where(key_ref[...] >= thr, 0.0, NEG_INF).astype(f32)


def _select_sample(scores, past, t_new, k_sel):
    return pl.pallas_call(
        functools.partial(_select_sample_kernel, past=past, t_new=t_new, k_sel=k_sel), name="select_sample",
        out_shape=jax.ShapeDtypeStruct(scores.shape, f32),
        scratch_shapes=[pltpu.VMEM(scores.shape, i32)],
        compiler_params=pltpu.CompilerParams(vmem_limit_bytes=VMEM_LIMIT),
    )(scores)


def _attend_sample_a_kernel(pt_ref, q_ref, bias_ref, mask_ref, biast_ref, maskt_ref, gate_ref, *refs, pg, kv, scale):
    kpages, vpages = refs[:pg], refs[pg:2 * pg]
    kt_ref, vt_ref, o_ref, m_ref, l_ref, acc_ref = refs[2 * pg:]
    s_id = pl.program_id(1)
    rows = q_ref.shape[1] // kv

    @pl.when(s_id == 0)
    def _():
        m_ref[...] = jnp.full(m_ref.shape, M_INIT, f32)
        l_ref[...] = jnp.zeros(l_ref.shape, f32)
        acc_ref[...] = jnp.zeros(acc_ref.shape, f32)

    def head_rows(page_ref, kh):
        return page_ref[pl.ds(kh, PAGE_SIZE, stride=kv), :].astype(bf16)

    def update(k_refs, v_refs, bias, mask):
        s = jnp.concatenate(
            [jnp.concatenate([_nt_dot(q_ref[0, kh * rows:(kh + 1) * rows, :], head_rows(kr, kh)) for kr in k_refs],
                             axis=1) for kh in range(kv)], axis=0)
        s = s * scale + bias + jnp.concatenate([mask] * kv, axis=0)
        m_new, l_new, alpha, p = _softmax_step(s, m_ref[...], l_ref[...])
        p = p.astype(bf16)
        pv = []
        for kh in range(kv):
            pk = p[kh * rows:(kh + 1) * rows]
            pv.append(sum(jnp.dot(pk[:, r * PAGE_SIZE:(r + 1) * PAGE_SIZE], head_rows(vr, kh),
                                  preferred_element_type=f32) for r, vr in enumerate(v_refs)))
        acc_ref[...] = alpha * acc_ref[...] + jnp.concatenate(pv, axis=0)
        m_ref[...] = m_new
        l_ref[...] = l_new

    update(kpages, vpages, bias_ref[...], mask_ref[0])

    @pl.when(s_id == pl.num_programs(1) - 1)
    def _():
        update([kt_ref.at[0]], [vt_ref.at[0]], biast_ref[...], maskt_ref[0])
        o_ref[0] = acc_ref[...] / l_ref[...] * _silu(gate_ref[0])


def _attend_sample_a(page_table, q2, bias_tab, mask, cache_k, cache_v, layer, k_tail, v_tail, gate2):
    dbatch, qrows, _ = q2.shape
    kv = cache_k.shape[2] // PAGE_SIZE
    rows = qrows // kv
    n_pages = page_table.shape[1]
    past = n_pages * PAGE_SIZE
    pg = _pick_tile(n_pages, (4, 2, 1))
    qspec = pl.BlockSpec((1, qrows, HEAD_DIM), lambda b, s, pt: (b, 0, 0))
    kspecs = [pl.BlockSpec((None, None, PAGE_SIZE * kv, HEAD_DIM),
                           lambda b, s, pt, r=r: (layer, pt[b, s * pg + r], 0, 0)) for r in range(pg)]
    tailspec = pl.BlockSpec((1, PAGE_SIZE * kv, HEAD_DIM), lambda b, s, pt: (b, 0, 0))
    return pl.pallas_call(
        functools.partial(_attend_sample_a_kernel, pg=pg, kv=kv, scale=HEAD_DIM ** -0.5), name="attend_sample_a",
        grid_spec=pltpu.PrefetchScalarGridSpec(
            num_scalar_prefetch=1, grid=(dbatch, n_pages // pg),
            in_specs=[qspec,
                      pl.BlockSpec((qrows, pg * PAGE_SIZE), lambda b, s, pt: (0, s)),
                      pl.BlockSpec((1, rows, pg * PAGE_SIZE), lambda b, s, pt: (b, 0, s)),
                      pl.BlockSpec((qrows, PAGE_SIZE), lambda b, s, pt: (0, past // PAGE_SIZE)),
                      pl.BlockSpec((1, rows, PAGE_SIZE), lambda b, s, pt: (b, 0, past // PAGE_SIZE)),
                      qspec] + kspecs + kspecs + [tailspec, tailspec],
            out_specs=qspec,
            scratch_shapes=[pltpu.VMEM((qrows, 1), f32), pltpu.VMEM((qrows, 1), f32),
                            pltpu.VMEM((qrows, HEAD_DIM), f32)]),
        out_shape=jax.ShapeDtypeStruct(q2.shape, f32),
        compiler_params=_params(2),
    )(page_table, q2, bias_tab, mask, bias_tab, mask, gate2,
      *([cache_k] * pg), *([cache_v] * pg), k_tail, v_tail)


def _attend_prompt_b_kernel(q0_ref, q1_ref, q2_ref, k_ref, v_ref, gate_ref, gb_ref, o_ref,
                            qf_ref, og_ref, lse_ref, *, rates, scale, unroll):
    seq = k_ref.shape[0]
    col = lax.broadcasted_iota(i32, (BLK, 2 * BLK), 1)

    for g, (q_ref, d) in enumerate(zip((q0_ref, q1_ref, q2_ref), rates)):
        qf_ref[...] = q_ref[...].astype(f32)
        nb = seq // d // BLK
        bias = gb_ref[0, :, g * 2 * BLK:(g + 1) * 2 * BLK]

        def blk_body(it, carry, g=g, d=d, nb=nb, bias=bias):
            r = it // nb
            blk = it % nb
            start = r + blk * (BLK * d)
            start_prev = r + jnp.maximum(blk - 1, 0) * (BLK * d)
            cur = pl.ds(start, BLK, stride=d)
            prev = pl.ds(start_prev, BLK, stride=d)
            qb = qf_ref[cur, :].astype(bf16)
            kcat = jnp.concatenate([k_ref[prev, :], k_ref[cur, :]], axis=0).astype(bf16)
            vcat = jnp.concatenate([v_ref[prev, :], v_ref[cur, :]], axis=0).astype(bf16)
            s = _nt_dot(qb, kcat) * scale + bias
            s = jnp.where((col >= BLK) | (blk > 0), s, NEG_INF)
            m = s.max(axis=1, keepdims=True)
            e = jnp.exp(s - m)
            ssum = e.sum(axis=1, keepdims=True)
            og_ref[g, cur, :] = jnp.dot(e.astype(bf16), vcat, preferred_element_type=f32) / ssum
            lse_ref[g, cur, :] = jnp.broadcast_to(m + jnp.log(ssum), (BLK, HEAD_DIM))
            return carry

        lax.fori_loop(0, d * nb, blk_body, 0, unroll=unroll)

    def merge_body(cb, carry):
        rows = pl.ds(pl.multiple_of(cb * BLK, BLK), BLK)
        lses = [lse_ref[g, rows, :] for g in range(len(rates))]
        top = functools.reduce(jnp.maximum, lses)
        wts = [jnp.exp(x - top) for x in lses]
        num = sum(w * og_ref[g, rows, :] for g, w in enumerate(wts))
        gt = gate_ref[rows, :]
        o_ref[rows, :] = (num / sum(wts) * _silu(gt)).astype(o_ref.dtype)
        return carry

    lax.fori_loop(0, seq // BLK, merge_body, 0)


def _attend_prompt_b(q, k, v, gate, bias_tab, batch, seq):
    n_heads = k.shape[1] // HEAD_DIM
    n_dil = len(DIL_RATES)
    blk = (seq, HEAD_DIM)
    qspecs = [pl.BlockSpec(blk, lambda b, h, g=g: (b, g * n_heads + h)) for g in range(n_dil)]
    hspec = pl.BlockSpec(blk, lambda b, h: (b, h))
    return pl.pallas_call(
        functools.partial(_attend_prompt_b_kernel, rates=DIL_RATES, scale=HEAD_DIM ** -0.5, unroll=4),
        name="attend_prompt_b", grid=(batch, n_heads),
        in_specs=qspecs + [hspec, hspec, hspec,
                           pl.BlockSpec((1, BLK, bias_tab.shape[2]), lambda b, h: (h, 0, 0))],
        out_specs=hspec,
        out_shape=jax.ShapeDtypeStruct(k.shape, bf16),
        scratch_shapes=[pltpu.VMEM(blk, f32), pltpu.VMEM((n_dil,) + blk, f32), pltpu.VMEM((n_dil,) + blk, f32)],
        compiler_params=_params(2),
    )(q, q, q, k, v, gate, bias_tab)


def _attend_sample_b_kernel(q_ref, k_ref, v_ref, kt_ref, vt_ref, bias_ref, gate_ref, o_ref, *, n_dil, t_new, scale):
    wb, hgs, _ = k_ref.shape
    k2 = k_ref.reshape(wb * hgs, HEAD_DIM)
    v2 = v_ref.reshape(wb * hgs, HEAD_DIM)
    for hl in range(hgs):
        cols = slice(hl * HEAD_DIM, (hl + 1) * HEAD_DIM)
        head = pl.ds(hl, wb, stride=hgs)
        q = q_ref[0, hl]
        s_main = _nt_dot(q, k2[head, :].astype(bf16)) * scale + bias_ref[hl, :, :wb]
        s_tail = _nt_dot(q, kt_ref[0, :, cols].astype(bf16)) * scale + bias_ref[hl, :, wb:]
        m = jnp.maximum(s_main.max(axis=1, keepdims=True), s_tail.max(axis=1, keepdims=True))
        e_main = jnp.exp(s_main - m)
        e_tail = jnp.exp(s_tail - m)
        ssum = e_main.sum(axis=1, keepdims=True) + e_tail.sum(axis=1, keepdims=True)
        o = (jnp.dot(e_main.astype(bf16), v2[head, :].astype(bf16), preferred_element_type=f32)
             + jnp.dot(e_tail.astype(bf16), vt_ref[0, :, cols].astype(bf16), preferred_element_type=f32)) / ssum
        lse = m + jnp.log(ssum)
        lses = [lse[g * t_new:(g + 1) * t_new] for g in range(n_dil)]
        top = functools.reduce(jnp.maximum, lses)
        wts = [jnp.exp(x - top) for x in lses]
        num = sum(w * o[g * t_new:(g + 1) * t_new] for g, w in enumerate(wts))
        o_ref[0, hl] = num / sum(wts) * _silu(gate_ref[0, hl])


def _attend_sample_b(q16, buf_k, buf_v, layer, k_tail, v_tail, bias_tab, gate4, t_new):
    dbatch, n_heads, rows, _ = q16.shape
    wb = buf_k.shape[2]
    hgs = 8 if n_heads % 8 == 0 else n_heads
    grouped = lambda a: a.reshape(a.shape[0], dbatch, wb, n_heads // hgs, hgs, HEAD_DIM)
    bufspec = pl.BlockSpec((None, None, wb, None, hgs, HEAD_DIM), lambda b, hg: (layer, b, 0, hg, 0, 0))
    tailspec = pl.BlockSpec((1, LANE, hgs * HEAD_DIM), lambda b, hg: (b, 0, hg))
    hspec = lambda r: pl.BlockSpec((1, hgs, r, HEAD_DIM), lambda b, hg: (b, hg, 0, 0))
    return pl.pallas_call(
        functools.partial(_attend_sample_b_kernel, n_dil=len(DIL_RATES), t_new=t_new, scale=HEAD_DIM ** -0.5),
        name="attend_sample_b", grid=(dbatch, n_heads // hgs),
        in_specs=[hspec(rows), bufspec, bufspec, tailspec, tailspec,
                  pl.BlockSpec((hgs, rows, wb + LANE), lambda b, hg: (hg, 0, 0)),
                  hspec(t_new)],
        out_specs=hspec(t_new),
        out_shape=jax.ShapeDtypeStruct((dbatch, n_heads, t_new, HEAD_DIM), f32),
        compiler_params=_params(2),
    )(q16, grouped(buf_k), grouped(buf_v), k_tail, v_tail, bias_tab, gate4)


def _offsets(sizes):
    return [int(x) for x in np.cumsum((0,) + tuple(sizes))[:-1]]


def _pad_rows(a, rows):
    return jnp.pad(a, ((0, 0), (0, rows - a.shape[1])) + ((0, 0),) * (a.ndim - 2))


def _layer_a(xp, xs, dims, j, cache_k, cache_v, cache_ki, page_table, w_in, w_out, ln_g, ln_b, rel_bias):
    batch, seq, dbatch, t_new = dims
    d_model = xp.shape[1]
    n_heads = d_model // HEAD_DIM
    kv, hi = KV_HEADS_A, IDX_HEADS
    group = n_heads // kv
    branch = n_heads * HEAD_DIM
    sizes = (branch, kv * HEAD_DIM, kv * HEAD_DIM, branch, hi * IDX_DIM, IDX_DIM, hi)
    offs = _offsets(sizes)
    tail_w = 2 * LANE
    w_tail = jnp.pad(w_in[j:j + 1, :, offs[5]:], ((0, 0), (0, 0), (0, tail_w - IDX_DIM - hi)))

    def project(x, tag):
        xb = x.astype(bf16)
        mm = lambda slot, dt, nm: _matmul(xb, w_in, j, offs[slot], sizes[slot], dt, f"proj_a_{nm}_{tag}")
        kw = _matmul(xb, w_tail, 0, 0, tail_w, f32, f"proj_a_kiwi_{tag}")
        return (mm(0, bf16, "q"), mm(1, f32, "k"), mm(2, f32, "v"), mm(3, f32, "gate"), mm(4, bf16, "qi"),
                kw[:, :IDX_DIM], kw[:, IDX_DIM:IDX_DIM + hi])

    q, k_p, v_p, gate, qi, ki_p, wi = project(xp, "p")
    nblk = seq // BLK
    dist = (np.arange(BLK)[:, None] + (nblk - 1) * BLK) - np.arange(seq + ATTN_A_CHUNK - BLK)[None, :]
    bias_p = _bias_table(rel_bias, _bucket_codes(dist))
    maskadd = _select_prompt(qi, ki_p, wi.T, batch, seq, min(TOPK_MAX, seq // 4))
    og = _attend_prompt(q, k_p, v_p, maskadd, bias_p, gate, batch, seq)
    xp_new = _layernorm(_matmul(og, w_out, j, 0, d_model, f32, "out_a_p"), xp, ln_g[j], ln_b[j])

    q, k_s, v_s, gate, qi, ki_s, wi = project(xs, "s")
    n_pages = page_table.shape[1]
    past = n_pages * PAGE_SIZE
    n_keys = past + t_new
    n_lanes = past + PAGE_SIZE
    cscale = (IDX_DIM ** -0.5) * (hi ** -0.5)
    qs = qi.reshape(dbatch, t_new * hi, IDX_DIM)
    wbc = jnp.broadcast_to((wi * cscale).reshape(dbatch, t_new * hi, 1), (dbatch, t_new * hi, LANE))
    ki_tail = _pad_rows(ki_s.reshape(dbatch, t_new, IDX_DIM), PAGE_SIZE)
    scores = _sample_scores(page_table, qs, wbc, cache_ki, j, ki_tail, t_new)
    mask_s = _select_sample(scores.reshape(dbatch * t_new, n_lanes), past, t_new, min(TOPK_MAX, n_keys // 4))

    def to_rows(a):
        a = a.reshape(dbatch, t_new, kv, group, HEAD_DIM)
        return a.transpose(0, 2, 3, 1, 4).reshape(dbatch, kv * group * t_new, HEAD_DIM)

    def to_page(a):
        return _pad_rows(a.reshape(dbatch, t_new, kv, HEAD_DIM), PAGE_SIZE).reshape(dbatch, PAGE_SIZE * kv, HEAD_DIM)

    rows = group * t_new
    dist = (past + np.arange(t_new)[:, None]) - np.arange(n_lanes)[None, :]
    bias_s = _bias_table(rel_bias, _bucket_codes(dist)).reshape(kv * rows, n_lanes)
    mask_g = jnp.broadcast_to(mask_s.reshape(dbatch, 1, t_new, n_lanes), (dbatch, group, t_new, n_lanes))
    mask_g = mask_g.reshape(dbatch, rows, n_lanes)
    pool = cache_k.shape[1]
    paged = lambda c: c.reshape(c.shape[0], pool, PAGE_SIZE * kv, HEAD_DIM)
    og2 = _attend_sample_a(page_table, to_rows(q), bias_s, mask_g, paged(cache_k), paged(cache_v), j,
                           to_page(k_s), to_page(v_s), to_rows(gate))
    og_s = og2.reshape(dbatch, kv, group, t_new, HEAD_DIM).transpose(0, 3, 1, 2, 4)
    og_s = og_s.reshape(dbatch * t_new, branch).astype(bf16)
    xs_new = _layernorm(_matmul(og_s, w_out, j, 0, d_model, f32, "out_a_s"), xs, ln_g[j], ln_b[j])

    shp = lambda a, b_, t_, *rest: a.reshape(b_, t_, *rest)
    outs = (shp(k_p, batch, seq, kv, HEAD_DIM), shp(v_p, batch, seq, kv, HEAD_DIM), shp(ki_p, batch, seq, IDX_DIM),
            shp(k_s, dbatch, t_new, kv, HEAD_DIM), shp(v_s, dbatch, t_new, kv, HEAD_DIM),
            shp(ki_s, dbatch, t_new, IDX_DIM))
    return xp_new, xs_new, outs


def _layer_b(xp, xs, dims, j, buf_k, buf_v, w_in, w_out, ln_g, ln_b, rel_bias):
    batch, seq, dbatch, t_new = dims
    d_model = xp.shape[1]
    n_heads = d_model // HEAD_DIM
    branch = n_heads * HEAD_DIM
    n_dil = len(DIL_RATES)
    sizes = (n_dil * branch, branch, branch, branch)
    offs = _offsets(sizes)

    def project(x, tag):
        xb = x.astype(bf16)
        mm = lambda slot, dt, nm: _matmul(xb, w_in, j, offs[slot], sizes[slot], dt, f"proj_b_{nm}_{tag}")
        return mm(0, bf16, "q"), mm(1, f32, "k"), mm(2, f32, "v"), mm(3, f32, "gate")

    q, k_p, v_p, gate = project(xp, "p")
    m = (np.arange(BLK)[:, None] + BLK) - np.arange(2 * BLK)[None, :]
    codes = [_bucket_codes(d * np.clip(m, 0, w // d), (m >= 0) & (m <= w // d))
             for d, w in zip(DIL_RATES, DIL_WINDOWS)]
    bias_p = _bias_table(rel_bias, np.concatenate(codes, axis=1))
    og = _attend_prompt_b(q, k_p, v_p, gate, bias_p, batch, seq)
    xp_new = _layernorm(_matmul(og, w_out, j, 0, d_model, f32, "out_b_p"), xp, ln_g[j], ln_b[j])

    q, k_s, v_s, gate = project(xs, "s")
    wb = buf_k.shape[2]
    rows = -(-(n_dil * t_new) // 16) * 16
    pos = np.arange(wb + LANE)[None, :]
    codes = []
    for d, w in zip(DIL_RATES, DIL_WINDOWS):
        dist = wb + np.arange(t_new)[:, None] - pos
        codes.append(_bucket_codes(dist, (dist >= 0) & (dist % d == 0) & (dist // d <= w // d) & (pos < wb + t_new)))
    codes.append(np.full((rows - n_dil * t_new, wb + LANE), BUCKET_ZERO, np.int32))
    bias_s = _bias_table(rel_bias, np.concatenate(codes, axis=0))
    q16 = q.reshape(dbatch, t_new, n_dil, n_heads, HEAD_DIM).transpose(0, 3, 2, 1, 4)
    q16 = q16.reshape(dbatch, n_heads, n_dil * t_new, HEAD_DIM)
    q16 = jnp.pad(q16, ((0, 0), (0, 0), (0, rows - n_dil * t_new), (0, 0)))
    gate4 = gate.reshape(dbatch, t_new, n_heads, HEAD_DIM).transpose(0, 2, 1, 3)
    k_tail = _pad_rows(k_s.reshape(dbatch, t_new, branch), LANE)
    v_tail = _pad_rows(v_s.reshape(dbatch, t_new, branch), LANE)
    og4 = _attend_sample_b(q16, buf_k, buf_v, j, k_tail, v_tail, bias_s, gate4, t_new)
    og_s = og4.transpose(0, 2, 1, 3).reshape(dbatch * t_new, branch).astype(bf16)
    xs_new = _layernorm(_matmul(og_s, w_out, j, 0, d_model, f32, "out_b_s"), xs, ln_g[j], ln_b[j])

    shift = lambda buf, new: jnp.concatenate([buf[j, :, t_new:], new.reshape(dbatch, t_new, n_heads, HEAD_DIM)], axis=1)
    wp = min(W_MAX, seq)
    k_win = k_p.reshape(batch, seq, n_heads, HEAD_DIM)[:, seq - wp:]
    v_win = v_p.reshape(batch, seq, n_heads, HEAD_DIM)[:, seq - wp:]
    return xp_new, xs_new, (k_win, v_win, shift(buf_k, k_s), shift(buf_v, v_s))


def kernel(x_prompt, x_sample, cache_k_a, cache_v_a, cache_kidx_a, cache_k_b, cache_v_b, page_table, rel_bias,
           w_in_a, w_out_a, ln_g_a, ln_b_a, w_in_b, w_out_b, ln_g_b, ln_b_b):
    batch, seq, d_model = x_prompt.shape
    dbatch, t_new, _ = x_sample.shape
    dims = (batch, seq, dbatch, t_new)
    xp = x_prompt.reshape(batch * seq, d_model)
    xs = x_sample.reshape(dbatch * t_new, d_model)
    outs_a, outs_b = [], []
    for layer in range(DEPTH):
        j = layer // 2
        if layer % 2 == 0:
            xp, xs, o = _layer_a(xp, xs, dims, j, cache_k_a, cache_v_a, cache_kidx_a, page_table,
                                 w_in_a, w_out_a, ln_g_a, ln_b_a, rel_bias)
            outs_a.append(o)
        else:
            xp, xs, o = _layer_b(xp, xs, dims, j, cache_k_b, cache_v_b, w_in_b, w_out_b, ln_g_b, ln_b_b, rel_bias)
            outs_b.append(o)
    stack = lambda group, idx: group[0][idx][None] if len(group) == 1 else jnp.stack([o[idx] for o in group])
    return (xp.reshape(batch, seq, d_model), xs.reshape(dbatch, t_new, d_model),
            *(stack(outs_a, n) for n in range(6)), *(stack(outs_b, n) for n in range(4)))
```

```python
import functools
import math

import jax
import jax.numpy as jnp
import numpy as np
from jax import lax
from jax.experimental import pallas as pl
from jax.experimental.pallas import tpu as pltpu

HEAD_DIM = 128
KV_HEADS_A = 8
IDX_HEADS = 64
IDX_DIM = 128
TOPK_MAX = 256
DIL_WINDOWS = (128, 512, 2048)
DIL_RATES = (1, 4, 16)
W_MAX = max(DIL_WINDOWS)
BLK = 128
N_BUCKETS = 32
REL_MAX_DIST = 2048
DEPTH = 2
ALPHA = (2 * DEPTH) ** 0.25
LN_EPS = 1e-5
PAGE_SIZE = 128

LANE = 128
NEG_INF = float("-inf")
INT_MIN = -(2 ** 31)
M_INIT = -1e30
BUCKET_ZERO = -1
BUCKET_MASKED = -2
VMEM_LIMIT = 56 * 1024 * 1024

f32 = jnp.float32
bf16 = jnp.bfloat16
i32 = jnp.int32


def _params(n_axes, vmem=VMEM_LIMIT):
    return pltpu.CompilerParams(dimension_semantics=("arbitrary",) * n_axes, vmem_limit_bytes=vmem)


def _nt_dot(a, b):
    return lax.dot_general(a, b, (((1,), (1,)), ((), ())), preferred_element_type=f32)


def _silu(x):
    return x * (1.0 / (1.0 + jnp.exp(-x)))


def _softmax_step(s, m_old, l_old):
    m_new = jnp.maximum(m_old, s.max(axis=1, keepdims=True))
    alpha = jnp.exp(m_old - m_new)
    p = jnp.exp(s - m_new)
    return m_new, alpha * l_old + p.sum(axis=1, keepdims=True), alpha, p


def _mm_kernel(x_ref, xs_ref, w_ref, o_ref, os_ref, wbf_ref, *, k_chunk):
    @pl.when(pl.program_id(1) == 0)
    def _():
        def body(r, c):
            rows = pl.ds(pl.multiple_of(r * k_chunk, k_chunk), k_chunk)
            wbf_ref[rows, :] = w_ref[rows, :].astype(bf16)
            return c
        lax.fori_loop(0, w_ref.shape[0] // k_chunk, body, 0)
        os_ref[...] = jnp.dot(xs_ref[...], wbf_ref[...], preferred_element_type=f32).astype(os_ref.dtype)

    o_ref[...] = jnp.dot(x_ref[...], wbf_ref[...], preferred_element_type=f32).astype(o_ref.dtype)


def _pick_tile(n, candidates):
    for c in candidates:
        if n % c == 0:
            return c
    raise ValueError(f"no tile for {n}")


def _matmul(x_bf, xs_bf, w, layer, col_off, ncols, out_dtype, name):
    m, k = x_bf.shape
    ms = xs_bf.shape[0]
    tn = _pick_tile(math.gcd(col_off, ncols) if col_off else ncols, (512, 256, 128))
    tm = _pick_tile(m, (1024, 512, 256, 128))
    off_blocks = col_off // tn
    k_chunk = _pick_tile(k, (256, 128))
    return pl.pallas_call(
        functools.partial(_mm_kernel, k_chunk=k_chunk), name=name,
        grid=(ncols // tn, m // tm),
        in_specs=[pl.BlockSpec((tm, k), lambda n, i: (i, 0)),
                  pl.BlockSpec((ms, k), lambda n, i: (0, 0)),
                  pl.BlockSpec((None, k, tn), lambda n, i: (layer, 0, n + off_blocks))],
        out_specs=[pl.BlockSpec((tm, tn), lambda n, i: (i, n)),
                   pl.BlockSpec((ms, tn), lambda n, i: (0, n))],
        out_shape=[jax.ShapeDtypeStruct((m, ncols), out_dtype), jax.ShapeDtypeStruct((ms, ncols), out_dtype)],
        scratch_shapes=[pltpu.VMEM((k, tn), bf16)],
        compiler_params=_params(2),
    )(x_bf, xs_bf, w)


def _ln_kernel(y_ref, x_ref, g_ref, b_ref, o_ref):
    z = ALPHA * x_ref[...] + y_ref[...]
    mu = jnp.mean(z, axis=-1, keepdims=True)
    zc = z - mu
    var = jnp.mean(zc * zc, axis=-1, keepdims=True)
    o_ref[...] = zc * lax.rsqrt(var + LN_EPS) * g_ref[...] + b_ref[...]


def _layernorm(y, x, g, b):
    m, d = x.shape
    tm = _pick_tile(m, (128, 64, 32, 16, 8))
    row = pl.BlockSpec((tm, d), lambda i: (i, 0))
    vec = pl.BlockSpec((1, d), lambda i: (0, 0))
    return pl.pallas_call(
        _ln_kernel, grid=(m // tm,), name="layernorm",
        in_specs=[row, row, vec, vec], out_specs=row,
        out_shape=jax.ShapeDtypeStruct((m, d), f32),
        compiler_params=_params(1),
    )(y, x, g.reshape(1, d), b.reshape(1, d))


def _rel_bucket_np(n):
    max_exact = N_BUCKETS // 2
    nf = np.maximum(n, 1).astype(np.float64)
    large = max_exact + (np.log(nf / max_exact) / math.log(REL_MAX_DIST / max_exact)
                         * (N_BUCKETS - max_exact)).astype(np.int64)
    large = np.minimum(large, N_BUCKETS - 1)
    return np.where(n < max_exact, n, large).astype(np.int32)


def _bucket_codes(dist, valid=None):
    code = np.where(dist >= 0, _rel_bucket_np(np.maximum(dist, 0)), BUCKET_ZERO)
    if valid is not None:
        code = np.where(valid, code, BUCKET_MASKED)
    return code.astype(np.int32)


def _bias_kernel(tbl_ref, bm_ref, o_ref, *, cw):
    h = pl.program_id(0)
    row = tbl_ref[pl.ds(h, 1), :]

    def body(c, carry):
        for t in range(cw // LANE):
            cols = pl.ds(pl.multiple_of(c * cw + t * LANE, LANE), LANE)
            code = bm_ref[:, cols]
            val = jnp.take_along_axis(jnp.broadcast_to(row, code.shape), jnp.maximum(code, 0), axis=1)
            o_ref[0, :, cols] = jnp.where(code >= 0, val, jnp.where(code == BUCKET_MASKED, NEG_INF, 0.0))
        return carry

    lax.fori_loop(0, bm_ref.shape[1] // cw, body, 0, unroll=8)


def _bias_table(rel_bias, codes):
    r, c = codes.shape
    n_heads = rel_bias.shape[1]
    units = c // LANE
    per = max(1, 16 // -(-r // 8))
    cw = LANE * max(u for u in range(1, units + 1) if units % u == 0 and u <= per)
    return pl.pallas_call(
        functools.partial(_bias_kernel, cw=cw), name="bias_table",
        grid=(n_heads,),
        in_specs=[pl.BlockSpec((n_heads, LANE), lambda h: (0, 0)),
                  pl.BlockSpec((r, c), lambda h: (0, 0))],
        out_specs=pl.BlockSpec((1, r, c), lambda h: (h, 0, 0)),
        out_shape=jax.ShapeDtypeStruct((n_heads, r, c), f32),
        compiler_params=_params(1),
    )(jnp.pad(rel_bias.T, ((0, 0), (0, LANE - N_BUCKETS))), jnp.asarray(codes))


def _sortable_key(x):
    bits = pltpu.bitcast(x, i32)
    return bits ^ ((bits >> 31) & 0x7FFFFFFF)


def _kth_largest_key(count_ge, shape, k_sel):
    def body(it, t):
        cand = t + lax.shift_left(jnp.int32(1), 31 - it)
        return jnp.where(count_ge(cand) >= k_sel, cand, t)
    return lax.fori_loop(0, 32, body, jnp.full(shape, INT_MIN, i32))


def _select_prompt_kernel(qi_ref, ki_ref, wt_ref, m_ref, kibf_ref, key_ref, *, n_idx_heads, k_sel, ck, hb, cscale):
    i = pl.program_id(1)
    tq = qi_ref.shape[0]

    @pl.when(i == 0)
    def _():
        kibf_ref[...] = ki_ref[...].astype(bf16)

    nck = ((i + 1) * tq + ck - 1) // ck
    t_row = i * tq + lax.broadcasted_iota(i32, (1, tq), 1)

    def chunk_body(c, carry):
        koff = pl.multiple_of(c * ck, ck)
        kc = kibf_ref[pl.ds(koff, ck), :]

        def heads_body(hg, acc):
            hoff = hg * (hb * IDX_DIM)
            qstack = jnp.concatenate(
                [qi_ref[:, pl.ds(pl.multiple_of(hoff + j * IDX_DIM, IDX_DIM), IDX_DIM)] for j in range(hb)], axis=0)
            s = _nt_dot(kc, qstack)
            w = wt_ref[pl.ds(pl.multiple_of(hg * hb, hb), hb), :] * cscale
            for j in range(hb):
                acc = acc + jnp.maximum(s[:, j * tq:(j + 1) * tq], 0.0) * w[j:j + 1, :]
            return acc

        acc = lax.fori_loop(0, n_idx_heads // hb, heads_body, jnp.zeros((ck, tq), f32), unroll=2)
        l_col = koff + lax.broadcasted_iota(i32, (ck, 1), 0)
        key_ref[pl.ds(koff, ck), :] = jnp.where(l_col <= t_row, _sortable_key(acc), INT_MIN)
        return carry

    lax.fori_loop(0, nck, chunk_body, 0)

    def count_ge(cand):
        def body(c, cnt):
            kc = key_ref[pl.ds(pl.multiple_of(c * ck, ck), ck), :]
            hit = jnp.where(kc >= cand, 1, 0).astype(i32)
            return cnt + hit.reshape(ck // 8, 8, tq).sum(axis=0)
        cnt = lax.fori_loop(0, nck, body, jnp.zeros((8, tq), i32))
        return cnt.sum(axis=0, keepdims=True)

    thr = _kth_largest_key(count_ge, (1, tq), k_sel)
    thr = jnp.maximum(thr, INT_MIN + 1)

    m_ref[...] = jnp.full(m_ref.shape, NEG_INF, m_ref.dtype)

    def out_body(c, carry):
        koff = pl.multiple_of(c * ck, ck)
        sel = jnp.where(key_ref[pl.ds(koff, ck), :] >= thr, 0.0, NEG_INF).astype(f32)
        m_ref[:, pl.ds(koff, ck)] = sel.T.astype(m_ref.dtype)
        return carry

    lax.fori_loop(0, nck, out_body, 0)


def _select_prompt(qi, ki, wi_t, batch, seq, k_sel):
    n_idx_heads = wi_t.shape[0]
    tq, ck = BLK, 512
    hb = _pick_tile(n_idx_heads, (8, 4, 2, 1))
    nblk = seq // tq
    cscale = (IDX_DIM ** -0.5) * (n_idx_heads ** -0.5)
    return pl.pallas_call(
        functools.partial(_select_prompt_kernel, n_idx_heads=n_idx_heads, k_sel=k_sel, ck=ck, hb=hb, cscale=cscale),
        name="select_prompt", grid=(batch, nblk),
        in_specs=[pl.BlockSpec((tq, n_idx_heads * IDX_DIM), lambda b, i: (b * nblk + i, 0)),
                  pl.BlockSpec((seq, IDX_DIM), lambda b, i: (b, 0)),
                  pl.BlockSpec((n_idx_heads, tq), lambda b, i: (0, b * nblk + i))],
        out_specs=pl.BlockSpec((tq, seq), lambda b, i: (b * nblk + i, 0)),
        out_shape=jax.ShapeDtypeStruct((batch * seq, seq), bf16),
        scratch_shapes=[pltpu.VMEM((seq, IDX_DIM), bf16), pltpu.VMEM((seq, tq), i32)],
        compiler_params=_params(2),
    )(qi, ki, wi_t)


def _attend_prompt_kernel(q_ref, k_ref, v_ref, mk_ref, g_ref, gate_ref, o_ref,
                          kbf_ref, vbf_ref, s_ref, l_ref, acc_ref, *, group, ck, sub, scale):
    i = pl.program_id(2)
    nblk = pl.num_programs(2)
    tq = q_ref.shape[0]
    rows = group * tq

    @pl.when(i == 0)
    def _():
        kbf_ref[...] = k_ref[...].astype(bf16)
        vbf_ref[...] = v_ref[...].astype(bf16)

    nck = ((i + 1) * tq + ck - 1) // ck
    goff0 = (nblk - 1 - i) * tq
    qs = jnp.concatenate([q_ref[:, g * HEAD_DIM:(g + 1) * HEAD_DIM] for g in range(group)], axis=0)

    def logits_body(c, mpart):
        for u in range(ck // sub):
            koff = pl.multiple_of(c * ck + u * sub, sub)
            goff = pl.multiple_of(goff0 + koff, LANE)
            s = _nt_dot(qs, kbf_ref[pl.ds(koff, sub), :]) * scale
            bias = jnp.concatenate([g_ref[g, :, pl.ds(goff, sub)] for g in range(group)], axis=0)
            sel = mk_ref[:, pl.ds(koff, sub)].astype(f32)
            s = s + bias + jnp.concatenate([sel] * group, axis=0)
            s_ref[:, pl.ds(koff, sub)] = s
            for t in range(sub // LANE):
                mpart = jnp.maximum(mpart, s[:, t * LANE:(t + 1) * LANE])
        return mpart

    mpart = lax.fori_loop(0, nck, logits_body, jnp.full((rows, LANE), M_INIT, f32))
    m_all = jnp.broadcast_to(mpart.max(axis=1, keepdims=True), (rows, LANE))

    l_ref[...] = jnp.zeros(l_ref.shape, f32)
    acc_ref[...] = jnp.zeros(acc_ref.shape, f32)

    def pv_body(c, carry):
        for u in range(ck // sub):
            koff = pl.multiple_of(c * ck + u * sub, sub)
            ps = [jnp.exp(s_ref[:, pl.ds(pl.multiple_of(koff + t * LANE, LANE), LANE)] - m_all)
                  for t in range(sub // LANE)]
            l_ref[...] += functools.reduce(jnp.add, ps)
            p = jnp.concatenate(ps, axis=1).astype(bf16)
            acc_ref[...] += jnp.dot(p, vbf_ref[pl.ds(koff, sub), :], preferred_element_type=f32)
        return carry

    lax.fori_loop(0, nck, pv_body, 0)
    o = acc_ref[...] / l_ref[...].sum(axis=1, keepdims=True)
    for g in range(group):
        cols = slice(g * HEAD_DIM, (g + 1) * HEAD_DIM)
        o_ref[:, cols] = (o[g * tq:(g + 1) * tq] * _silu(gate_ref[:, cols])).astype(o_ref.dtype)


ATTN_A_CHUNK = 512


def _attend_prompt(q, k, v, maskadd, bias_tab, gate, batch, seq):
    kv = k.shape[1] // HEAD_DIM
    group = q.shape[1] // (kv * HEAD_DIM)
    tq, ck = BLK, ATTN_A_CHUNK
    nblk = seq // tq
    gw = group * HEAD_DIM
    qspec = pl.BlockSpec((tq, gw), lambda kh, b, i: (b * nblk + i, kh))
    kvspec = pl.BlockSpec((seq, HEAD_DIM), lambda kh, b, i: (b, kh))
    return pl.pallas_call(
        functools.partial(_attend_prompt_kernel, group=group, ck=ck, sub=256, scale=HEAD_DIM ** -0.5),
        name="attend_prompt_a", grid=(kv, batch, nblk),
        in_specs=[qspec, kvspec, kvspec,
                  pl.BlockSpec((tq, seq), lambda kh, b, i: (b * nblk + i, 0)),
                  pl.BlockSpec((group, tq, bias_tab.shape[2]), lambda kh, b, i: (kh, 0, 0)),
                  qspec],
        out_specs=qspec,
        out_shape=jax.ShapeDtypeStruct(q.shape, bf16),
        scratch_shapes=[pltpu.VMEM((seq, HEAD_DIM), bf16), pltpu.VMEM((seq, HEAD_DIM), bf16),
                        pltpu.VMEM((group * tq, seq), f32), pltpu.VMEM((group * tq, LANE), f32),
                        pltpu.VMEM((group * tq, HEAD_DIM), f32)],
        compiler_params=_params(3),
    )(q, k, v, maskadd, bias_tab, gate)


def _score_page(qs, wb, kpage, t_new):
    s = _nt_dot(qs, kpage.astype(bf16))
    x = jnp.maximum(s, 0.0) * wb
    return x.reshape(t_new, x.shape[0] // t_new, x.shape[1]).sum(axis=1)


def _sample_scores_kernel(pt_ref, qs_ref, wb_ref, *refs, pg, t_new):
    o_ref = refs[pg]
    for r in range(pg):
        o_ref[0, :, r * PAGE_SIZE:(r + 1) * PAGE_SIZE] = _score_page(qs_ref[0], wb_ref[0], refs[r][...], t_new)


def _sample_scores_tail_kernel(qs_ref, wb_ref, kt_ref, o_ref, *, t_new):
    o_ref[0] = _score_page(qs_ref[0], wb_ref[0], kt_ref[0], t_new)


def _sample_scores(page_table, qs, wb, cache_ki, layer, ki_tail, t_new):
    dbatch, rows, _ = qs.shape
    n_pages = page_table.shape[1]
    pg = _pick_tile(n_pages, (8, 4, 2, 1))
    qspec = pl.BlockSpec((1, rows, IDX_DIM), lambda b, s, pt: (b, 0, 0))
    page_specs = [pl.BlockSpec((None, None, PAGE_SIZE, IDX_DIM),
                               lambda b, s, pt, r=r: (layer, pt[b, s * pg + r], 0, 0)) for r in range(pg)]
    main = pl.pallas_call(
        functools.partial(_sample_scores_kernel, pg=pg, t_new=t_new), name="sample_scores",
        grid_spec=pltpu.PrefetchScalarGridSpec(
            num_scalar_prefetch=1, grid=(dbatch, n_pages // pg),
            in_specs=[qspec, qspec] + page_specs,
            out_specs=pl.BlockSpec((1, t_new, pg * PAGE_SIZE), lambda b, s, pt: (b, 0, s))),
        out_shape=jax.ShapeDtypeStruct((dbatch, t_new, n_pages * PAGE_SIZE), f32),
        compiler_params=_params(2),
    )(page_table, qs, wb, *([cache_ki] * pg))
    spec3 = lambda d1, d2: pl.BlockSpec((1, d1, d2), lambda b: (b, 0, 0))
    tail = pl.pallas_call(
        functools.partial(_sample_scores_tail_kernel, t_new=t_new), name="sample_scores_tail",
        grid=(dbatch,),
        in_specs=[spec3(rows, IDX_DIM), spec3(rows, IDX_DIM), spec3(PAGE_SIZE, IDX_DIM)],
        out_specs=spec3(t_new, PAGE_SIZE),
        out_shape=jax.ShapeDtypeStruct((dbatch, t_new, PAGE_SIZE), f32),
        compiler_params=_params(1),
    )(qs, wb, ki_tail)
    return jnp.concatenate([main, tail], axis=2)


def _select_sample_kernel(s_ref, o_ref, key_ref, *, past, t_new, k_sel):
    rows, n = s_ref.shape
    t = past + lax.broadcasted_iota(i32, (rows, 1), 0) % t_new
    l = lax.broadcasted_iota(i32, (1, n), 1)
    key_ref[...] = jnp.where(l <= t, _sortable_key(s_ref[...]), INT_MIN)

    def count_ge(cand):
        return jnp.where(key_ref[...] >= cand, 1, 0).astype(i32).sum(axis=1, keepdims=True)

    thr = jnp.maximum(_kth_largest_key(count_ge, (rows, 1), k_sel), INT_MIN + 1)
    o_ref[...] = jnp.where(key_ref[...] >= thr, 0.0, NEG_INF).astype(f32)


def _select_sample(scores, past, t_new, k_sel):
    return pl.pallas_call(
        functools.partial(_select_sample_kernel, past=past, t_new=t_new, k_sel=k_sel), name="select_sample",
        out_shape=jax.ShapeDtypeStruct(scores.shape, f32),
        scratch_shapes=[pltpu.VMEM(scores.shape, i32)],
        compiler_params=pltpu.CompilerParams(vmem_limit_bytes=VMEM_LIMIT),
    )(scores)


def _attend_sample_a_kernel(pt_ref, q_ref, bias_ref, mask_ref, biast_ref, maskt_ref, gate_ref, *refs, pg, kv, scale):
    kpages, vpages = refs[:pg], refs[pg:2 * pg]
    kt_ref, vt_ref, o_ref, m_ref, l_ref, acc_ref = refs[2 * pg:]
    s_id = pl.program_id(1)
    rows = q_ref.shape[1] // kv

    @pl.when(s_id == 0)
    def _():
        m_ref[...] = jnp.full(m_ref.shape, M_INIT, f32)
        l_ref[...] = jnp.zeros(l_ref.shape, f32)
        acc_ref[...] = jnp.zeros(acc_ref.shape, f32)

    def head_rows(page_ref, kh):
        return page_ref[pl.ds(kh, PAGE_SIZE, stride=kv), :].astype(bf16)

    def update(k_refs, v_refs, bias, mask):
        s = jnp.concatenate(
            [jnp.concatenate([_nt_dot(q_ref[0, kh * rows:(kh + 1) * rows, :], head_rows(kr, kh)) for kr in k_refs],
                             axis=1) for kh in range(kv)], axis=0)
        s = s * scale + bias + jnp.concatenate([mask] * kv, axis=0)
        m_new, l_new, alpha, p = _softmax_step(s, m_ref[...], l_ref[...])
        p = p.astype(bf16)
        pv = []
        for kh in range(kv):
            pk = p[kh * rows:(kh + 1) * rows]
            pv.append(sum(jnp.dot(pk[:, r * PAGE_SIZE:(r + 1) * PAGE_SIZE], head_rows(vr, kh),
                                  preferred_element_type=f32) for r, vr in enumerate(v_refs)))
        acc_ref[...] = alpha * acc_ref[...] + jnp.concatenate(pv, axis=0)
        m_ref[...] = m_new
        l_ref[...] = l_new

    update(kpages, vpages, bias_ref[...], mask_ref[0])

    @pl.when(s_id == pl.num_programs(1) - 1)
    def _():
        update([kt_ref.at[0]], [vt_ref.at[0]], biast_ref[...], maskt_ref[0])
        o_ref[0] = acc_ref[...] / l_ref[...] * _silu(gate_ref[0])


def _attend_sample_a(page_table, q2, bias_tab, mask, cache_k, cache_v, layer, k_tail, v_tail, gate2):
    dbatch, qrows, _ = q2.shape
    kv = cache_k.shape[2] // PAGE_SIZE
    rows = qrows // kv
    n_pages = page_table.shape[1]
    past = n_pages * PAGE_SIZE
    pg = _pick_tile(n_pages, (4, 2, 1))
    qspec = pl.BlockSpec((1, qrows, HEAD_DIM), lambda b, s, pt: (b, 0, 0))
    kspecs = [pl.BlockSpec((None, None, PAGE_SIZE * kv, HEAD_DIM),
                           lambda b, s, pt, r=r: (layer, pt[b, s * pg + r], 0, 0)) for r in range(pg)]
    tailspec = pl.BlockSpec((1, PAGE_SIZE * kv, HEAD_DIM), lambda b, s, pt: (b, 0, 0))
    return pl.pallas_call(
        functools.partial(_attend_sample_a_kernel, pg=pg, kv=kv, scale=HEAD_DIM ** -0.5), name="attend_sample_a",
        grid_spec=pltpu.PrefetchScalarGridSpec(
            num_scalar_prefetch=1, grid=(dbatch, n_pages // pg),
            in_specs=[qspec,
                      pl.BlockSpec((qrows, pg * PAGE_SIZE), lambda b, s, pt: (0, s)),
                      pl.BlockSpec((1, rows, pg * PAGE_SIZE), lambda b, s, pt: (b, 0, s)),
                      pl.BlockSpec((qrows, PAGE_SIZE), lambda b, s, pt: (0, past // PAGE_SIZE)),
                      pl.BlockSpec((1, rows, PAGE_SIZE), lambda b, s, pt: (b, 0, past // PAGE_SIZE)),
                      qspec] + kspecs + kspecs + [tailspec, tailspec],
            out_specs=qspec,
            scratch_shapes=[pltpu.VMEM((qrows, 1), f32), pltpu.VMEM((qrows, 1), f32),
                            pltpu.VMEM((qrows, HEAD_DIM), f32)]),
        out_shape=jax.ShapeDtypeStruct(q2.shape, f32),
        compiler_params=_params(2),
    )(page_table, q2, bias_tab, mask, bias_tab, mask, gate2,
      *([cache_k] * pg), *([cache_v] * pg), k_tail, v_tail)


def _attend_prompt_b_kernel(q0_ref, q1_ref, q2_ref, k_ref, v_ref, gate_ref, gb_ref, o_ref,
                            qf_ref, og_ref, lse_ref, *, rates, scale, unroll):
    seq = k_ref.shape[0]
    col = lax.broadcasted_iota(i32, (BLK, 2 * BLK), 1)

    for g, (q_ref, d) in enumerate(zip((q0_ref, q1_ref, q2_ref), rates)):
        qf_ref[...] = q_ref[...].astype(f32)
        nb = seq // d // BLK
        bias = gb_ref[0, :, g * 2 * BLK:(g + 1) * 2 * BLK]

        def blk_body(it, carry, g=g, d=d, nb=nb, bias=bias):
            r = it // nb
            blk = it % nb
            start = r + blk * (BLK * d)
            start_prev = r + jnp.maximum(blk - 1, 0) * (BLK * d)
            cur = pl.ds(start, BLK, stride=d)
            prev = pl.ds(start_prev, BLK, stride=d)
            qb = qf_ref[cur, :].astype(bf16)
            kcat = jnp.concatenate([k_ref[prev, :], k_ref[cur, :]], axis=0).astype(bf16)
            vcat = jnp.concatenate([v_ref[prev, :], v_ref[cur, :]], axis=0).astype(bf16)
            s = _nt_dot(qb, kcat) * scale + bias
            s = jnp.where((col >= BLK) | (blk > 0), s, NEG_INF)
            m = s.max(axis=1, keepdims=True)
            e = jnp.exp(s - m)
            ssum = e.sum(axis=1, keepdims=True)
            og_ref[g, cur, :] = jnp.dot(e.astype(bf16), vcat, preferred_element_type=f32) / ssum
            lse_ref[g, cur, :] = jnp.broadcast_to(m + jnp.log(ssum), (BLK, HEAD_DIM))
            return carry

        lax.fori_loop(0, d * nb, blk_body, 0, unroll=unroll)

    def merge_body(cb, carry):
        rows = pl.ds(pl.multiple_of(cb * BLK, BLK), BLK)
        lses = [lse_ref[g, rows, :] for g in range(len(rates))]
        top = functools.reduce(jnp.maximum, lses)
        wts = [jnp.exp(x - top) for x in lses]
        num = sum(w * og_ref[g, rows, :] for g, w in enumerate(wts))
        gt = gate_ref[rows, :]
        o_ref[rows, :] = (num / sum(wts) * _silu(gt)).astype(o_ref.dtype)
        return carry

    lax.fori_loop(0, seq // BLK, merge_body, 0)


def _attend_prompt_b(q, k, v, gate, bias_tab, batch, seq):
    n_heads = k.shape[1] // HEAD_DIM
    n_dil = len(DIL_RATES)
    blk = (seq, HEAD_DIM)
    qspecs = [pl.BlockSpec(blk, lambda b, h, g=g: (b, g * n_heads + h)) for g in range(n_dil)]
    hspec = pl.BlockSpec(blk, lambda b, h: (b, h))
    return pl.pallas_call(
        functools.partial(_attend_prompt_b_kernel, rates=DIL_RATES, scale=HEAD_DIM ** -0.5, unroll=16),
        name="attend_prompt_b", grid=(batch, n_heads),
        in_specs=qspecs + [hspec, hspec, hspec,
                           pl.BlockSpec((1, BLK, bias_tab.shape[2]), lambda b, h: (h, 0, 0))],
        out_specs=hspec,
        out_shape=jax.ShapeDtypeStruct(k.shape, bf16),
        scratch_shapes=[pltpu.VMEM(blk, f32), pltpu.VMEM((n_dil,) + blk, f32), pltpu.VMEM((n_dil,) + blk, f32)],
        compiler_params=_params(2),
    )(q, q, q, k, v, gate, bias_tab)


def _attend_sample_b_kernel(q_ref, k_ref, v_ref, kt_ref, vt_ref, bias_ref, gate_ref, o_ref, *, n_dil, t_new, scale):
    wb, hgs, _ = k_ref.shape
    k2 = k_ref.reshape(wb * hgs, HEAD_DIM)
    v2 = v_ref.reshape(wb * hgs, HEAD_DIM)
    for hl in range(hgs):
        cols = slice(hl * HEAD_DIM, (hl + 1) * HEAD_DIM)
        head = pl.ds(hl, wb, stride=hgs)
        q = q_ref[0, hl]
        s_main = _nt_dot(q, k2[head, :].astype(bf16)) * scale + bias_ref[hl, :, :wb]
        s_tail = _nt_dot(q, kt_ref[0, :, cols].astype(bf16)) * scale + bias_ref[hl, :, wb:]
        m = jnp.maximum(s_main.max(axis=1, keepdims=True), s_tail.max(axis=1, keepdims=True))
        e_main = jnp.exp(s_main - m)
        e_tail = jnp.exp(s_tail - m)
        ssum = e_main.sum(axis=1, keepdims=True) + e_tail.sum(axis=1, keepdims=True)
        o = (jnp.dot(e_main.astype(bf16), v2[head, :].astype(bf16), preferred_element_type=f32)
             + jnp.dot(e_tail.astype(bf16), vt_ref[0, :, cols].astype(bf16), preferred_element_type=f32)) / ssum
        lse = m + jnp.log(ssum)
        lses = [lse[g * t_new:(g + 1) * t_new] for g in range(n_dil)]
        top = functools.reduce(jnp.maximum, lses)
        wts = [jnp.exp(x - top) for x in lses]
        num = sum(w * o[g * t_new:(g + 1) * t_new] for g, w in enumerate(wts))
        o_ref[0, hl] = num / sum(wts) * _silu(gate_ref[0, hl])


def _attend_sample_b(q16, buf_k, buf_v, layer, k_tail, v_tail, bias_tab, gate4, t_new):
    dbatch, n_heads, rows, _ = q16.shape
    wb = buf_k.shape[2]
    hgs = 8 if n_heads % 8 == 0 else n_heads
    grouped = lambda a: a.reshape(a.shape[0], dbatch, wb, n_heads // hgs, hgs, HEAD_DIM)
    bufspec = pl.BlockSpec((None, None, wb, None, hgs, HEAD_DIM), lambda b, hg: (layer, b, 0, hg, 0, 0))
    tailspec = pl.BlockSpec((1, LANE, hgs * HEAD_DIM), lambda b, hg: (b, 0, hg))
    hspec = lambda r: pl.BlockSpec((1, hgs, r, HEAD_DIM), lambda b, hg: (b, hg, 0, 0))
    return pl.pallas_call(
        functools.partial(_attend_sample_b_kernel, n_dil=len(DIL_RATES), t_new=t_new, scale=HEAD_DIM ** -0.5),
        name="attend_sample_b", grid=(dbatch, n_heads // hgs),
        in_specs=[hspec(rows), bufspec, bufspec, tailspec, tailspec,
                  pl.BlockSpec((hgs, rows, wb + LANE), lambda b, hg: (hg, 0, 0)),
                  hspec(t_new)],
        out_specs=hspec(t_new),
        out_shape=jax.ShapeDtypeStruct((dbatch, n_heads, t_new, HEAD_DIM), f32),
        compiler_params=_params(2),
    )(q16, grouped(buf_k), grouped(buf_v), k_tail, v_tail, bias_tab, gate4)


SHIFT_RING = 4
SHIFT_LAG = 2
SHIFT_CHUNK_BYTES = 5 << 20


def _shift_kernel(bk_ref, bv_ref, nk_ref, nv_ref, ok_ref, ov_ref, ring, in_sem, out_sem, new_sem,
                  *, layer, dbatch, rows, shift, chunk):
    reads, writes, tails = [], [], []
    n_chunks = (rows - shift) // chunk
    for a, (src, new, dst) in enumerate(((bk_ref, nk_ref, ok_ref), (bv_ref, nv_ref, ov_ref))):
        for b in range(dbatch):
            for c in range(n_chunks):
                slot = len(reads) % SHIFT_RING
                reads.append(pltpu.make_async_copy(src.at[layer, b, pl.ds(shift + c * chunk, chunk)],
                                                   ring.at[slot], in_sem.at[slot]))
                writes.append(pltpu.make_async_copy(ring.at[slot], dst.at[b, pl.ds(c * chunk, chunk)],
                                                    out_sem.at[slot]))
            tails.append(pltpu.make_async_copy(new.at[b], dst.at[b, pl.ds(rows - shift, shift)], new_sem.at[a, b]))
    for t in tails:
        t.start()
    n = len(reads)
    for i in range(n + SHIFT_LAG):
        if i < n:
            if i >= SHIFT_RING:
                writes[i - SHIFT_RING].wait()
            reads[i].start()
        if 0 <= i - SHIFT_LAG < n:
            reads[i - SHIFT_LAG].wait()
            writes[i - SHIFT_LAG].start()
    for i in range(max(0, n - SHIFT_RING), n):
        writes[i].wait()
    for t in tails:
        t.wait()


def _shift_buffers(buf_k, buf_v, layer, new_k, new_v):
    _, dbatch, rows, width = buf_k.shape
    shift = new_k.shape[1]
    body = (rows - shift) // shift
    per = max(u for u in range(1, body + 1) if body % u == 0 and u * shift * width * 4 <= SHIFT_CHUNK_BYTES)
    chunk = per * shift
    anyspec = pl.BlockSpec(memory_space=pl.ANY)
    out = jax.ShapeDtypeStruct((dbatch, rows, width), buf_k.dtype)
    return pl.pallas_call(
        functools.partial(_shift_kernel, layer=layer, dbatch=dbatch, rows=rows, shift=shift, chunk=chunk),
        name="shift_buffers",
        in_specs=[anyspec] * 4, out_specs=[anyspec] * 2, out_shape=[out, out],
        scratch_shapes=[pltpu.VMEM((SHIFT_RING, chunk, width), buf_k.dtype),
                        pltpu.SemaphoreType.DMA((SHIFT_RING,)), pltpu.SemaphoreType.DMA((SHIFT_RING,)),
                        pltpu.SemaphoreType.DMA((2, dbatch))],
        compiler_params=pltpu.CompilerParams(vmem_limit_bytes=VMEM_LIMIT),
    )(buf_k, buf_v, new_k, new_v)


def _offsets(sizes):
    return [int(x) for x in np.cumsum((0,) + tuple(sizes))[:-1]]


def _pad_rows(a, rows):
    return jnp.pad(a, ((0, 0), (0, rows - a.shape[1])) + ((0, 0),) * (a.ndim - 2))


def _layer_a(xp, xs, dims, j, cache_k, cache_v, cache_ki, page_table, w_in, w_out, ln_g, ln_b, rel_bias):
    batch, seq, dbatch, t_new = dims
    d_model = xp.shape[1]
    n_heads = d_model // HEAD_DIM
    kv, hi = KV_HEADS_A, IDX_HEADS
    group = n_heads // kv
    branch = n_heads * HEAD_DIM
    sizes = (branch, kv * HEAD_DIM, kv * HEAD_DIM, branch, hi * IDX_DIM, IDX_DIM, hi)
    offs = _offsets(sizes)
    tail_w = 2 * LANE
    w_tail = jnp.pad(w_in[j:j + 1, :, offs[5]:], ((0, 0), (0, 0), (0, tail_w - IDX_DIM - hi)))

    xpb, xsb = xp.astype(bf16), xs.astype(bf16)
    mm = lambda slot, dt, nm: _matmul(xpb, xsb, w_in, j, offs[slot], sizes[slot], dt, f"proj_a_{nm}")
    (q, q_s), (k_p, k_s), (v_p, v_s) = mm(0, bf16, "q"), mm(1, f32, "k"), mm(2, f32, "v")
    (gate, gate_s), (qi, qi_s) = mm(3, f32, "gate"), mm(4, bf16, "qi")
    kw, kw_s = _matmul(xpb, xsb, w_tail, 0, 0, tail_w, f32, "proj_a_kiwi")
    ki_p, wi = kw[:, :IDX_DIM], kw[:, IDX_DIM:IDX_DIM + hi]
    ki_s, wi_s = kw_s[:, :IDX_DIM], kw_s[:, IDX_DIM:IDX_DIM + hi]

    nblk = seq // BLK
    dist = (np.arange(BLK)[:, None] + (nblk - 1) * BLK) - np.arange(seq + ATTN_A_CHUNK - BLK)[None, :]
    bias_p = _bias_table(rel_bias, _bucket_codes(dist))
    maskadd = _select_prompt(qi, ki_p, wi.T, batch, seq, min(TOPK_MAX, seq // 4))
    og = _attend_prompt(q, k_p, v_p, maskadd, bias_p, gate, batch, seq)

    q, gate, qi, wi = q_s, gate_s, qi_s, wi_s
    n_pages = page_table.shape[1]
    past = n_pages * PAGE_SIZE
    n_keys = past + t_new
    n_lanes = past + PAGE_SIZE
    cscale = (IDX_DIM ** -0.5) * (hi ** -0.5)
    qs = qi.reshape(dbatch, t_new * hi, IDX_DIM)
    wbc = jnp.broadcast_to((wi * cscale).reshape(dbatch, t_new * hi, 1), (dbatch, t_new * hi, LANE))
    ki_tail = _pad_rows(ki_s.reshape(dbatch, t_new, IDX_DIM), PAGE_SIZE)
    scores = _sample_scores(page_table, qs, wbc, cache_ki, j, ki_tail, t_new)
    mask_s = _select_sample(scores.reshape(dbatch * t_new, n_lanes), past, t_new, min(TOPK_MAX, n_keys // 4))

    def to_rows(a):
        a = a.reshape(dbatch, t_new, kv, group, HEAD_DIM)
        return a.transpose(0, 2, 3, 1, 4).reshape(dbatch, kv * group * t_new, HEAD_DIM)

    def to_page(a):
        return _pad_rows(a.reshape(dbatch, t_new, kv, HEAD_DIM), PAGE_SIZE).reshape(dbatch, PAGE_SIZE * kv, HEAD_DIM)

    rows = group * t_new
    dist = (past + np.arange(t_new)[:, None]) - np.arange(n_lanes)[None, :]
    bias_s = _bias_table(rel_bias, _bucket_codes(dist)).reshape(kv * rows, n_lanes)
    mask_g = jnp.broadcast_to(mask_s.reshape(dbatch, 1, t_new, n_lanes), (dbatch, group, t_new, n_lanes))
    mask_g = mask_g.reshape(dbatch, rows, n_lanes)
    pool = cache_k.shape[1]
    paged = lambda c: c.reshape(c.shape[0], pool, PAGE_SIZE * kv, HEAD_DIM)
    og2 = _attend_sample_a(page_table, to_rows(q), bias_s, mask_g, paged(cache_k), paged(cache_v), j,
                           to_page(k_s), to_page(v_s), to_rows(gate))
    og_s = og2.reshape(dbatch, kv, group, t_new, HEAD_DIM).transpose(0, 3, 1, 2, 4)
    og_s = og_s.reshape(dbatch * t_new, branch).astype(bf16)
    y, y_s = _matmul(og, og_s, w_out, j, 0, d_model, f32, "out_a")
    xp_new = _layernorm(y, xp, ln_g[j], ln_b[j])
    xs_new = _layernorm(y_s, xs, ln_g[j], ln_b[j])

    shp = lambda a, b_, t_, *rest: a.reshape(b_, t_, *rest)
    outs = (shp(k_p, batch, seq, kv, HEAD_DIM), shp(v_p, batch, seq, kv, HEAD_DIM), shp(ki_p, batch, seq, IDX_DIM),
            shp(k_s, dbatch, t_new, kv, HEAD_DIM), shp(v_s, dbatch, t_new, kv, HEAD_DIM),
            shp(ki_s, dbatch, t_new, IDX_DIM))
    return xp_new, xs_new, outs


def _layer_b(xp, xs, dims, j, buf_k, buf_v, w_in, w_out, ln_g, ln_b, rel_bias):
    batch, seq, dbatch, t_new = dims
    d_model = xp.shape[1]
    n_heads = d_model // HEAD_DIM
    branch = n_heads * HEAD_DIM
    n_dil = len(DIL_RATES)
    sizes = (n_dil * branch, branch, branch, branch)
    offs = _offsets(sizes)

    xpb, xsb = xp.astype(bf16), xs.astype(bf16)
    mm = lambda slot, dt, nm: _matmul(xpb, xsb, w_in, j, offs[slot], sizes[slot], dt, f"proj_b_{nm}")
    (q, q_s), (k_p, k_s), (v_p, v_s), (gate, gate_s) = mm(0, bf16, "q"), mm(1, f32, "k"), mm(2, f32, "v"), mm(3, f32, "gate")

    m =(np.arange(BLK)[:, None] + BLK) - np.arange(2 * BLK)[None, :]
    codes = [_bucket_codes(d * np.clip(m, 0, w // d), (m >= 0) & (m <= w // d))
             for d, w in zip(DIL_RATES, DIL_WINDOWS)]
    bias_p = _bias_table(rel_bias, np.concatenate(codes, axis=1))
    og = _attend_prompt_b(q, k_p, v_p, gate, bias_p, batch, seq)

    q, gate = q_s, gate_s
    wb = buf_k.shape[2]
    rows = -(-(n_dil * t_new) // 16) * 16
    pos = np.arange(wb + LANE)[None, :]
    codes = []
    for d, w in zip(DIL_RATES, DIL_WINDOWS):
        dist = wb + np.arange(t_new)[:, None] - pos
        codes.append(_bucket_codes(dist, (dist >= 0) & (dist % d == 0) & (dist // d <= w // d) & (pos < wb + t_new)))
    codes.append(np.full((rows - n_dil * t_new, wb + LANE), BUCKET_ZERO, np.int32))
    bias_s = _bias_table(rel_bias, np.concatenate(codes, axis=0))
    q16 = q.reshape(dbatch, t_new, n_dil, n_heads, HEAD_DIM).transpose(0, 3, 2, 1, 4)
    q16 = q16.reshape(dbatch, n_heads, n_dil * t_new, HEAD_DIM)
    q16 = jnp.pad(q16, ((0, 0), (0, 0), (0, rows - n_dil * t_new), (0, 0)))
    gate4 = gate.reshape(dbatch, t_new, n_heads, HEAD_DIM).transpose(0, 2, 1, 3)
    k_tail = _pad_rows(k_s.reshape(dbatch, t_new, branch), LANE)
    v_tail = _pad_rows(v_s.reshape(dbatch, t_new, branch), LANE)
    og4 = _attend_sample_b(q16, buf_k, buf_v, j, k_tail, v_tail, bias_s, gate4, t_new)
    og_s = og4.transpose(0, 2, 1, 3).reshape(dbatch * t_new, branch).astype(bf16)
    y, y_s = _matmul(og, og_s, w_out, j, 0, d_model, f32, "out_b")
    xp_new = _layernorm(y, xp, ln_g[j], ln_b[j])
    xs_new = _layernorm(y_s, xs, ln_g[j], ln_b[j])

    flat = lambda buf: buf.reshape(buf.shape[0], dbatch, wb * n_heads, HEAD_DIM)
    new_k, new_v = _shift_buffers(flat(buf_k), flat(buf_v), j, k_s.reshape(dbatch, t_new * n_heads, HEAD_DIM),
                                  v_s.reshape(dbatch, t_new * n_heads, HEAD_DIM))
    wp = min(W_MAX, seq)
    k_win = k_p.reshape(batch, seq, n_heads, HEAD_DIM)[:, seq - wp:]
    v_win = v_p.reshape(batch, seq, n_heads, HEAD_DIM)[:, seq - wp:]
    return xp_new, xs_new, (k_win, v_win, new_k.reshape(dbatch, wb, n_heads, HEAD_DIM),
                            new_v.reshape(dbatch, wb, n_heads, HEAD_DIM))


def kernel(x_prompt, x_sample, cache_k_a, cache_v_a, cache_kidx_a, cache_k_b, cache_v_b, page_table, rel_bias,
           w_in_a, w_out_a, ln_g_a, ln_b_a, w_in_b, w_out_b, ln_g_b, ln_b_b):
    batch, seq, d_model = x_prompt.shape
    dbatch, t_new, _ = x_sample.shape
    dims = (batch, seq, dbatch, t_new)
    xp = x_prompt.reshape(batch * seq, d_model)
    xs = x_sample.reshape(dbatch * t_new, d_model)
    outs_a, outs_b = [], []
    for layer in range(DEPTH):
        j = layer // 2
        if layer % 2 == 0:
            xp, xs, o = _layer_a(xp, xs, dims, j, cache_k_a, cache_v_a, cache_kidx_a, page_table,
                                 w_in_a, w_out_a, ln_g_a, ln_b_a, rel_bias)
            outs_a.append(o)
        else:
            xp, xs, o = _layer_b(xp, xs, dims, j, cache_k_b, cache_v_b, w_in_b, w_out_b, ln_g_b, ln_b_b, rel_bias)
            outs_b.append(o)
    stack = lambda group, idx: group[0][idx][None] if len(group) == 1 else jnp.stack([o[idx] for o in group])
    return (xp.reshape(batch, seq, d_model), xs.reshape(dbatch, t_new, d_model),
            *(stack(outs_a, n) for n in range(6)), *(stack(outs_b, n) for n in range(4)))
```

```python
import functools
import math

import jax
import jax.numpy as jnp
import numpy as np
from jax import lax
from jax.experimental import pallas as pl
from jax.experimental.pallas import tpu as pltpu

HEAD_DIM = 128
KV_HEADS_A = 8
IDX_HEADS = 64
IDX_DIM = 128
TOPK_MAX = 256
DIL_WINDOWS = (128, 512, 2048)
DIL_RATES = (1, 4, 16)
W_MAX = max(DIL_WINDOWS)
BLK = 128
N_BUCKETS = 32
REL_MAX_DIST = 2048
DEPTH = 2
ALPHA = (2 * DEPTH) ** 0.25
LN_EPS = 1e-5
PAGE_SIZE = 128

LANE = 128
NEG_INF = float("-inf")
INT_MIN = -(2 ** 31)
M_INIT = -1e30
BUCKET_ZERO = -1
BUCKET_MASKED = -2
VMEM_LIMIT = 56 * 1024 * 1024

f32 = jnp.float32
bf16 = jnp.bfloat16
i32 = jnp.int32


def _params(n_axes, vmem=VMEM_LIMIT):
    return pltpu.CompilerParams(dimension_semantics=("arbitrary",) * n_axes, vmem_limit_bytes=vmem)


def _nt_dot(a, b):
    return lax.dot_general(a, b, (((1,), (1,)), ((), ())), preferred_element_type=f32)


def _silu(x):
    return x * (1.0 / (1.0 + jnp.exp(-x)))


def _softmax_step(s, m_old, l_old):
    m_new = jnp.maximum(m_old, s.max(axis=1, keepdims=True))
    alpha = jnp.exp(m_old - m_new)
    p = jnp.exp(s - m_new)
    return m_new, alpha * l_old + p.sum(axis=1, keepdims=True), alpha, p


def _mm_kernel(x_ref, xs_ref, w_ref, o_ref, os_ref, wbf_ref, *, k_chunk):
    @pl.when(pl.program_id(1) == 0)
    def _():
        def body(r, c):
            rows = pl.ds(pl.multiple_of(r * k_chunk, k_chunk), k_chunk)
            wbf_ref[rows, :] = w_ref[rows, :].astype(bf16)
            return c
        lax.fori_loop(0, w_ref.shape[0] // k_chunk, body, 0)
        os_ref[...] = jnp.dot(xs_ref[...], wbf_ref[...], preferred_element_type=f32).astype(os_ref.dtype)

    o_ref[...] = jnp.dot(x_ref[...], wbf_ref[...], preferred_element_type=f32).astype(o_ref.dtype)


def _pick_tile(n, candidates):
    for c in candidates:
        if n % c == 0:
            return c
    raise ValueError(f"no tile for {n}")


def _matmul(x_bf, xs_bf, w, layer, col_off, ncols, out_dtype, name):
    m, k = x_bf.shape
    ms = xs_bf.shape[0]
    tn = _pick_tile(math.gcd(col_off, ncols) if col_off else ncols, (512, 256, 128))
    tm = _pick_tile(m, (1024, 512, 256, 128))
    off_blocks = col_off // tn
    k_chunk = _pick_tile(k, (256, 128))
    return pl.pallas_call(
        functools.partial(_mm_kernel, k_chunk=k_chunk), name=name,
        grid=(ncols // tn, m // tm),
        in_specs=[pl.BlockSpec((tm, k), lambda n, i: (i, 0)),
                  pl.BlockSpec((ms, k), lambda n, i: (0, 0)),
                  pl.BlockSpec((None, k, tn), lambda n, i: (layer, 0, n + off_blocks))],
        out_specs=[pl.BlockSpec((tm, tn), lambda n, i: (i, n)),
                   pl.BlockSpec((ms, tn), lambda n, i: (0, n))],
        out_shape=[jax.ShapeDtypeStruct((m, ncols), out_dtype), jax.ShapeDtypeStruct((ms, ncols), out_dtype)],
        scratch_shapes=[pltpu.VMEM((k, tn), bf16)],
        compiler_params=_params(2),
    )(x_bf, xs_bf, w)


def _mm_heads_kernel(x_ref, xs_ref, w_ref, o_ref, os_ref, oh_ref, wbf_ref, *, k_chunk):
    @pl.when(pl.program_id(1) == 0)
    def _():
        def body(r, c):
            rows = pl.ds(pl.multiple_of(r * k_chunk, k_chunk), k_chunk)
            wbf_ref[rows, :] = w_ref[rows, :].astype(bf16)
            return c
        lax.fori_loop(0, w_ref.shape[0] // k_chunk, body, 0)
        os_ref[...] = jnp.dot(xs_ref[...], wbf_ref[...], preferred_element_type=f32)

    res = jnp.dot(x_ref[...], wbf_ref[...], preferred_element_type=f32)
    o_ref[...] = res
    tm, nh, dh = oh_ref.shape
    flat = oh_ref.reshape(tm * nh, dh)
    for hh in range(nh):
        flat[pl.ds(hh, tm, stride=nh), :] = res[:, hh * dh:(hh + 1) * dh]


def _matmul_heads(x_bf, xs_bf, w, layer, col_off, n_heads, name):
    m, k = x_bf.shape
    ms = xs_bf.shape[0]
    nh = 8 if n_heads % 8 == 0 else n_heads
    tn = nh * HEAD_DIM
    ncols = n_heads * HEAD_DIM
    tm = _pick_tile(m, (512, 256, 128))
    off_blocks = col_off // tn
    k_chunk = _pick_tile(k, (256, 128))
    return pl.pallas_call(
        functools.partial(_mm_heads_kernel, k_chunk=k_chunk), name=name,
        grid=(ncols // tn, m // tm),
        in_specs=[pl.BlockSpec((tm, k), lambda n, i: (i, 0)),
                  pl.BlockSpec((ms, k), lambda n, i: (0, 0)),
                  pl.BlockSpec((None, k, tn), lambda n, i: (layer, 0, n + off_blocks),
                               pipeline_mode=pl.Buffered(1))],
        out_specs=[pl.BlockSpec((tm, tn), lambda n, i: (i, n)),
                   pl.BlockSpec((ms, tn), lambda n, i: (0, n)),
                   pl.BlockSpec((tm, nh, HEAD_DIM), lambda n, i: (i, n, 0))],
        out_shape=[jax.ShapeDtypeStruct((m, ncols), f32), jax.ShapeDtypeStruct((ms, ncols), f32),
                   jax.ShapeDtypeStruct((m, n_heads, HEAD_DIM), f32)],
        scratch_shapes=[pltpu.VMEM((k, tn), bf16)],
        compiler_params=_params(2),
    )(x_bf, xs_bf, w)


def _ln_kernel(y_ref, x_ref, g_ref, b_ref, o_ref, obf_ref):
    z = ALPHA * x_ref[...] + y_ref[...]
    mu = jnp.mean(z, axis=-1, keepdims=True)
    zc = z - mu
    var = jnp.mean(zc * zc, axis=-1, keepdims=True)
    out = zc * lax.rsqrt(var + LN_EPS) * g_ref[...] + b_ref[...]
    o_ref[...] = out
    obf_ref[...] = out.astype(bf16)


def _layernorm(y, x, g, b):
    m, d = x.shape
    tm = _pick_tile(m, (128, 64, 32, 16, 8))
    row = pl.BlockSpec((tm, d), lambda i: (i, 0))
    vec = pl.BlockSpec((1, d), lambda i: (0, 0))
    return pl.pallas_call(
        _ln_kernel, grid=(m // tm,), name="layernorm",
        in_specs=[row, row, vec, vec], out_specs=[row, row],
        out_shape=[jax.ShapeDtypeStruct((m, d), f32), jax.ShapeDtypeStruct((m, d), bf16)],
        compiler_params=_params(1),
    )(y, x, g.reshape(1, d), b.reshape(1, d))


def _rel_bucket_np(n):
    max_exact = N_BUCKETS // 2
    nf = np.maximum(n, 1).astype(np.float64)
    large = max_exact + (np.log(nf / max_exact) / math.log(REL_MAX_DIST / max_exact)
                         * (N_BUCKETS - max_exact)).astype(np.int64)
    large = np.minimum(large, N_BUCKETS - 1)
    return np.where(n < max_exact, n, large).astype(np.int32)


def _bucket_codes(dist, valid=None):
    code = np.where(dist >= 0, _rel_bucket_np(np.maximum(dist, 0)), BUCKET_ZERO)
    if valid is not None:
        code = np.where(valid, code, BUCKET_MASKED)
    return code.astype(np.int32)


def _bias_kernel(tbl_ref, bm_ref, o_ref, *, cw):
    h = pl.program_id(0)
    row = tbl_ref[pl.ds(h, 1), :]

    def body(c, carry):
        for t in range(cw // LANE):
            cols = pl.ds(pl.multiple_of(c * cw + t * LANE, LANE), LANE)
            code = bm_ref[:, cols]
            val = jnp.take_along_axis(jnp.broadcast_to(row, code.shape), jnp.maximum(code, 0), axis=1)
            o_ref[0, :, cols] = jnp.where(code >= 0, val, jnp.where(code == BUCKET_MASKED, NEG_INF, 0.0))
        return carry

    lax.fori_loop(0, bm_ref.shape[1] // cw, body, 0, unroll=8)


def _bias_table(rel_bias, codes):
    r, c = codes.shape
    n_heads = rel_bias.shape[1]
    units = c // LANE
    per = max(1, 16 // -(-r // 8))
    cw = LANE * max(u for u in range(1, units + 1) if units % u == 0 and u <= per)
    return pl.pallas_call(
        functools.partial(_bias_kernel, cw=cw), name="bias_table",
        grid=(n_heads,),
        in_specs=[pl.BlockSpec((n_heads, LANE), lambda h: (0, 0)),
                  pl.BlockSpec((r, c), lambda h: (0, 0))],
        out_specs=pl.BlockSpec((1, r, c), lambda h: (h, 0, 0)),
        out_shape=jax.ShapeDtypeStruct((n_heads, r, c), f32),
        compiler_params=_params(1),
    )(jnp.pad(rel_bias.T, ((0, 0), (0, LANE - N_BUCKETS))), jnp.asarray(codes))


def _sortable_key(x):
    bits = pltpu.bitcast(x, i32)
    return bits ^ ((bits >> 31) & 0x7FFFFFFF)


def _kth_largest_key(count_ge, shape, k_sel):
    def body(it, t):
        cand = t + lax.shift_left(jnp.int32(1), 31 - it)
        return jnp.where(count_ge(cand) >= k_sel, cand, t)
    return lax.fori_loop(0, 32, body, jnp.full(shape, INT_MIN, i32))


def _select_prompt_kernel(qi_ref, ki_ref, wt_ref, m_ref, kibf_ref, key_ref, *, n_idx_heads, k_sel, ck, hb, cscale):
    i = pl.program_id(1)
    tq = qi_ref.shape[0]

    @pl.when(i == 0)
    def _():
        kibf_ref[...] = ki_ref[...].astype(bf16)

    nck = ((i + 1) * tq + ck - 1) // ck
    t_row = i * tq + lax.broadcasted_iota(i32, (1, tq), 1)

    def chunk_body(c, carry):
        koff = pl.multiple_of(c * ck, ck)
        kc = kibf_ref[pl.ds(koff, ck), :]

        def heads_body(hg, acc):
            hoff = hg * (hb * IDX_DIM)
            qstack = jnp.concatenate(
                [qi_ref[:, pl.ds(pl.multiple_of(hoff + j * IDX_DIM, IDX_DIM), IDX_DIM)] for j in range(hb)], axis=0)
            s = _nt_dot(kc, qstack)
            w = wt_ref[pl.ds(pl.multiple_of(hg * hb, hb), hb), :] * cscale
            for j in range(hb):
                acc = acc + jnp.maximum(s[:, j * tq:(j + 1) * tq], 0.0) * w[j:j + 1, :]
            return acc

        acc = lax.fori_loop(0, n_idx_heads // hb, heads_body, jnp.zeros((ck, tq), f32), unroll=2)
        l_col = koff + lax.broadcasted_iota(i32, (ck, 1), 0)
        key_ref[pl.ds(koff, ck), :] = jnp.where(l_col <= t_row, _sortable_key(acc), INT_MIN)
        return carry

    lax.fori_loop(0, nck, chunk_body, 0)

    def count_ge(cand):
        def body(c, cnt):
            kc = key_ref[pl.ds(pl.multiple_of(c * ck, ck), ck), :]
            hit = jnp.where(kc >= cand, 1, 0).astype(i32)
            return cnt + hit.reshape(ck // 8, 8, tq).sum(axis=0)
        cnt = lax.fori_loop(0, nck, body, jnp.zeros((8, tq), i32))
        return cnt.sum(axis=0, keepdims=True)

    thr = _kth_largest_key(count_ge, (1, tq), k_sel)
    thr = jnp.maximum(thr, INT_MIN + 1)

    m_ref[...] = jnp.full(m_ref.shape, NEG_INF, m_ref.dtype)

    def out_body(c, carry):
        koff = pl.multiple_of(c * ck, ck)
        sel = jnp.where(key_ref[pl.ds(koff, ck), :] >= thr, 0.0, NEG_INF).astype(f32)
        m_ref[:, pl.ds(koff, ck)] = sel.T.astype(m_ref.dtype)
        return carry

    lax.fori_loop(0, nck, out_body, 0)


def _select_prompt(qi, ki, wi_t, batch, seq, k_sel):
    n_idx_heads = wi_t.shape[0]
    tq, ck = BLK, 256
    hb = _pick_tile(n_idx_heads, (16, 8, 4, 2, 1))
    nblk = seq // tq
    cscale = (IDX_DIM ** -0.5) * (n_idx_heads ** -0.5)
    return pl.pallas_call(
        functools.partial(_select_prompt_kernel, n_idx_heads=n_idx_heads, k_sel=k_sel, ck=ck, hb=hb, cscale=cscale),
        name="select_prompt", grid=(batch, nblk),
        in_specs=[pl.BlockSpec((tq, n_idx_heads * IDX_DIM), lambda b, i: (b * nblk + i, 0)),
                  pl.BlockSpec((seq, IDX_DIM), lambda b, i: (b, 0)),
                  pl.BlockSpec((n_idx_heads, tq), lambda b, i: (0, b * nblk + i))],
        out_specs=pl.BlockSpec((tq, seq), lambda b, i: (b * nblk + i, 0)),
        out_shape=jax.ShapeDtypeStruct((batch * seq, seq), bf16),
        scratch_shapes=[pltpu.VMEM((seq, IDX_DIM), bf16), pltpu.VMEM((seq, tq), i32)],
        compiler_params=_params(2),
    )(qi, ki, wi_t)


def _attend_prompt_kernel(q_ref, k_ref, v_ref, mk_ref, g_ref, gate_ref, o_ref,
                          kbf_ref, vbf_ref, s_ref, mp_ref, l_ref, acc_ref, *, group, ck, sub, scale):
    i = pl.program_id(2)
    nblk = pl.num_programs(2)
    tq = q_ref.shape[0]
    rows = group * tq

    @pl.when(i == 0)
    def _():
        kbf_ref[...] = k_ref[...].astype(bf16)
        vbf_ref[...] = v_ref[...].astype(bf16)

    nck = ((i + 1) * tq + ck - 1) // ck
    goff0 = (nblk - 1 - i) * tq
    qs = jnp.concatenate([q_ref[:, g * HEAD_DIM:(g + 1) * HEAD_DIM] for g in range(group)], axis=0)

    mp_ref[...] = jnp.full(mp_ref.shape, M_INIT, f32)

    def logits_body(c, carry):
        for u in range(ck // sub):
            koff = pl.multiple_of(c * ck + u * sub, sub)
            goff = pl.multiple_of(goff0 + koff, LANE)
            s = _nt_dot(qs, kbf_ref[pl.ds(koff, sub), :]) * scale
            bias = jnp.concatenate([g_ref[g, :, pl.ds(goff, sub)] for g in range(group)], axis=0)
            sel = mk_ref[:, pl.ds(koff, sub)].astype(f32)
            s = s + bias + jnp.concatenate([sel] * group, axis=0)
            s_ref[:, pl.ds(koff, sub)] = s
            mp_ref[...] = functools.reduce(jnp.maximum, [mp_ref[...]] + [s[:, t * LANE:(t + 1) * LANE]
                                                                         for t in range(sub // LANE)])
        return carry

    lax.fori_loop(0, nck, logits_body, 0)
    m_all = jnp.broadcast_to(mp_ref[...].max(axis=1, keepdims=True), (rows, LANE))

    l_ref[...] = jnp.zeros(l_ref.shape, f32)
    acc_ref[...] = jnp.zeros(acc_ref.shape, f32)

    def pv_body(c, carry):
        for u in range(ck // sub):
            koff = pl.multiple_of(c * ck + u * sub, sub)
            ps = [jnp.exp(s_ref[:, pl.ds(pl.multiple_of(koff + t * LANE, LANE), LANE)] - m_all)
                  for t in range(sub // LANE)]
            l_ref[...] += functools.reduce(jnp.add, ps)
            p = jnp.concatenate(ps, axis=1).astype(bf16)
            acc_ref[...] += jnp.dot(p, vbf_ref[pl.ds(koff, sub), :], preferred_element_type=f32)
        return carry

    lax.fori_loop(0, nck, pv_body, 0)
    o = acc_ref[...] / l_ref[...].sum(axis=1, keepdims=True)
    for g in range(group):
        cols = slice(g * HEAD_DIM, (g + 1) * HEAD_DIM)
        o_ref[:, cols] = (o[g * tq:(g + 1) * tq] * _silu(gate_ref[:, cols])).astype(o_ref.dtype)


ATTN_A_CHUNK = 512
ATTN_A_TQ = 256


def _attend_prompt(q, k, v, maskadd, bias_tab, gate, batch, seq):
    kv = k.shape[1] // HEAD_DIM
    group = q.shape[1] // (kv * HEAD_DIM)
    tq, ck = ATTN_A_TQ, ATTN_A_CHUNK
    nblk = seq // tq
    gw = group * HEAD_DIM
    qspec = pl.BlockSpec((tq, gw), lambda kh, b, i: (b * nblk + i, kh))
    kvspec = pl.BlockSpec((seq, HEAD_DIM), lambda kh, b, i: (b, kh))
    return pl.pallas_call(
        functools.partial(_attend_prompt_kernel, group=group, ck=ck, sub=256, scale=HEAD_DIM ** -0.5),
        name="attend_prompt_a", grid=(kv, batch, nblk),
        in_specs=[qspec, kvspec, kvspec,
                  pl.BlockSpec((tq, seq), lambda kh, b, i: (b * nblk + i, 0)),
                  pl.BlockSpec((group, tq, bias_tab.shape[2]), lambda kh, b, i: (kh, 0, 0)),
                  qspec],
        out_specs=qspec,
        out_shape=jax.ShapeDtypeStruct(q.shape, bf16),
        scratch_shapes=[pltpu.VMEM((seq, HEAD_DIM), bf16), pltpu.VMEM((seq, HEAD_DIM), bf16),
                        pltpu.VMEM((group * tq, seq), f32), pltpu.VMEM((group * tq, LANE), f32),
                        pltpu.VMEM((group * tq, LANE), f32), pltpu.VMEM((group * tq, HEAD_DIM), f32)],
        compiler_params=_params(3),
    )(q, k, v, maskadd, bias_tab, gate)


def _score_page(qs, wb, kpage, t_new):
    s = _nt_dot(qs, kpage.astype(bf16))
    x = jnp.maximum(s, 0.0) * wb
    return x.reshape(t_new, x.shape[0] // t_new, x.shape[1]).sum(axis=1)


def _sample_scores_kernel(pt_ref, qs_ref, wb_ref, *refs, pg, t_new):
    o_ref = refs[pg]
    for r in range(pg):
        o_ref[0, :, r * PAGE_SIZE:(r + 1) * PAGE_SIZE] = _score_page(qs_ref[0], wb_ref[0], refs[r][...], t_new)


def _sample_scores_tail_kernel(qs_ref, wb_ref, kt_ref, o_ref, *, t_new):
    o_ref[0] = _score_page(qs_ref[0], wb_ref[0], kt_ref[0], t_new)


def _sample_scores(page_table, qs, wb, cache_ki, layer, ki_tail, t_new):
    dbatch, rows, _ = qs.shape
    n_pages = page_table.shape[1]
    pg = _pick_tile(n_pages, (8, 4, 2, 1))
    qspec = pl.BlockSpec((1, rows, IDX_DIM), lambda b, s, pt: (b, 0, 0))
    page_specs = [pl.BlockSpec((None, None, PAGE_SIZE, IDX_DIM),
                               lambda b, s, pt, r=r: (layer, pt[b, s * pg + r], 0, 0)) for r in range(pg)]
    main = pl.pallas_call(
        functools.partial(_sample_scores_kernel, pg=pg, t_new=t_new), name="sample_scores",
        grid_spec=pltpu.PrefetchScalarGridSpec(
            num_scalar_prefetch=1, grid=(dbatch, n_pages // pg),
            in_specs=[qspec, qspec] + page_specs,
            out_specs=pl.BlockSpec((1, t_new, pg * PAGE_SIZE), lambda b, s, pt: (b, 0, s))),
        out_shape=jax.ShapeDtypeStruct((dbatch, t_new, n_pages * PAGE_SIZE), f32),
        compiler_params=_params(2),
    )(page_table, qs, wb, *([cache_ki] * pg))
    spec3 = lambda d1, d2: pl.BlockSpec((1, d1, d2), lambda b: (b, 0, 0))
    tail = pl.pallas_call(
        functools.partial(_sample_scores_tail_kernel, t_new=t_new), name="sample_scores_tail",
        grid=(dbatch,),
        in_specs=[spec3(rows, IDX_DIM), spec3(rows, IDX_DIM), spec3(PAGE_SIZE, IDX_DIM)],
        out_specs=spec3(t_new, PAGE_SIZE),
        out_shape=jax.ShapeDtypeStruct((dbatch, t_new, PAGE_SIZE), f32),
        compiler_params=_params(1),
    )(qs, wb, ki_tail)
    return jnp.concatenate([main, tail], axis=2)


def _select_sample_kernel(s_ref, o_ref, key_ref, *, past, t_new, k_sel):
    rows, n = s_ref.shape
    t = past + lax.broadcasted_iota(i32, (rows, 1), 0) % t_new
    l = lax.broadcasted_iota(i32, (1, n), 1)
    key_ref[...] = jnp.where(l <= t, _sortable_key(s_ref[...]), INT_MIN)

    def count_ge(cand):
        return jnp.where(key_ref[...] >= cand, 1, 0).astype(i32).sum(axis=1, keepdims=True)

    thr = jnp.maximum(_kth_largest_key(count_ge, (rows, 1), k_sel), INT_MIN + 1)
    o_ref[...] = jnp.where(key_ref[...] >= thr, 0.0, NEG_INF).astype(f32)


def _select_sample(scores, past, t_new, k_sel):
    return pl.pallas_call(
        functools.partial(_select_sample_kernel, past=past, t_new=t_new, k_sel=k_sel), name="select_sample",
        out_shape=jax.ShapeDtypeStruct(scores.shape, f32),
        scratch_shapes=[pltpu.VMEM(scores.shape, i32)],
        compiler_params=pltpu.CompilerParams(vmem_limit_bytes=VMEM_LIMIT),
    )(scores)


def _attend_sample_a_kernel(pt_ref, q_ref, bias_ref, mask_ref, biast_ref, maskt_ref, gate_ref, *refs, pg, kv, scale):
    kpages, vpages = refs[:pg], refs[pg:2 * pg]
    kt_ref, vt_ref, o_ref, m_ref, l_ref, acc_ref = refs[2 * pg:]
    s_id = pl.program_id(1)
    rows = q_ref.shape[1] // kv

    @pl.when(s_id == 0)
    def _():
        m_ref[...] = jnp.full(m_ref.shape, M_INIT, f32)
        l_ref[...] = jnp.zeros(l_ref.shape, f32)
        acc_ref[...] = jnp.zeros(acc_ref.shape, f32)

    def head_rows(page_ref, kh):
        return page_ref[pl.ds(kh, PAGE_SIZE, stride=kv), :].astype(bf16)

    def update(k_refs, v_refs, bias, mask):
        s = jnp.concatenate(
            [jnp.concatenate([_nt_dot(q_ref[0, kh * rows:(kh + 1) * rows, :], head_rows(kr, kh)) for kr in k_refs],
                             axis=1) for kh in range(kv)], axis=0)
        s = s * scale + bias + jnp.concatenate([mask] * kv, axis=0)
        m_new, l_new, alpha, p = _softmax_step(s, m_ref[...], l_ref[...])
        p = p.astype(bf16)
        pv = []
        for kh in range(kv):
            pk = p[kh * rows:(kh + 1) * rows]
            pv.append(sum(jnp.dot(pk[:, r * PAGE_SIZE:(r + 1) * PAGE_SIZE], head_rows(vr, kh),
                                  preferred_element_type=f32) for r, vr in enumerate(v_refs)))
        acc_ref[...] = alpha * acc_ref[...] + jnp.concatenate(pv, axis=0)
        m_ref[...] = m_new
        l_ref[...] = l_new

    update(kpages, vpages, bias_ref[...], mask_ref[0])

    @pl.when(s_id == pl.num_programs(1) - 1)
    def _():
        update([kt_ref.at[0]], [vt_ref.at[0]], biast_ref[...], maskt_ref[0])
        o_ref[0] = acc_ref[...] / l_ref[...] * _silu(gate_ref[0])


def _attend_sample_a(page_table, q2, bias_tab, mask, cache_k, cache_v, layer, k_tail, v_tail, gate2):
    dbatch, qrows, _ = q2.shape
    kv = cache_k.shape[2] // PAGE_SIZE
    rows = qrows // kv
    n_pages = page_table.shape[1]
    past = n_pages * PAGE_SIZE
    pg = _pick_tile(n_pages, (4, 2, 1))
    qspec = pl.BlockSpec((1, qrows, HEAD_DIM), lambda b, s, pt: (b, 0, 0))
    kspecs = [pl.BlockSpec((None, None, PAGE_SIZE * kv, HEAD_DIM),
                           lambda b, s, pt, r=r: (layer, pt[b, s * pg + r], 0, 0)) for r in range(pg)]
    tailspec = pl.BlockSpec((1, PAGE_SIZE * kv, HEAD_DIM), lambda b, s, pt: (b, 0, 0))
    return pl.pallas_call(
        functools.partial(_attend_sample_a_kernel, pg=pg, kv=kv, scale=HEAD_DIM ** -0.5), name="attend_sample_a",
        grid_spec=pltpu.PrefetchScalarGridSpec(
            num_scalar_prefetch=1, grid=(dbatch, n_pages // pg),
            in_specs=[qspec,
                      pl.BlockSpec((qrows, pg * PAGE_SIZE), lambda b, s, pt: (0, s)),
                      pl.BlockSpec((1, rows, pg * PAGE_SIZE), lambda b, s, pt: (b, 0, s)),
                      pl.BlockSpec((qrows, PAGE_SIZE), lambda b, s, pt: (0, past // PAGE_SIZE)),
                      pl.BlockSpec((1, rows, PAGE_SIZE), lambda b, s, pt: (b, 0, past // PAGE_SIZE)),
                      qspec] + kspecs + kspecs + [tailspec, tailspec],
            out_specs=qspec,
            scratch_shapes=[pltpu.VMEM((qrows, 1), f32), pltpu.VMEM((qrows, 1), f32),
                            pltpu.VMEM((qrows, HEAD_DIM), f32)]),
        out_shape=jax.ShapeDtypeStruct(q2.shape, f32),
        compiler_params=_params(2),
    )(page_table, q2, bias_tab, mask, bias_tab, mask, gate2,
      *([cache_k] * pg), *([cache_v] * pg), k_tail, v_tail)


def _attend_prompt_b_kernel(q0_ref, q1_ref, q2_ref, k_ref, v_ref, gate_ref, gb_ref, o_ref,
                            qf_ref, og_ref, lse_ref, *, rates, scale, unroll):
    seq = k_ref.shape[0]
    col = lax.broadcasted_iota(i32, (BLK, 2 * BLK), 1)

    for g, (q_ref, d) in enumerate(zip((q0_ref, q1_ref, q2_ref), rates)):
        qf_ref[...] = q_ref[...].astype(f32)
        nb = seq // d // BLK
        bias = gb_ref[0, :, g * 2 * BLK:(g + 1) * 2 * BLK]

        def blk_body(it, carry, g=g, d=d, nb=nb, bias=bias):
            r = it // nb
            blk = it % nb
            start = r + blk * (BLK * d)
            start_prev = r + jnp.maximum(blk - 1, 0) * (BLK * d)
            cur = pl.ds(start, BLK, stride=d)
            prev = pl.ds(start_prev, BLK, stride=d)
            qb = qf_ref[cur, :].astype(bf16)
            kcat = jnp.concatenate([k_ref[prev, :], k_ref[cur, :]], axis=0).astype(bf16)
            vcat = jnp.concatenate([v_ref[prev, :], v_ref[cur, :]], axis=0).astype(bf16)
            s = _nt_dot(qb, kcat) * scale + bias
            s = jnp.where((col >= BLK) | (blk > 0), s, NEG_INF)
            m = s.max(axis=1, keepdims=True)
            e = jnp.exp(s - m)
            ssum = e.sum(axis=1, keepdims=True)
            og_ref[g, cur, :] = jnp.dot(e.astype(bf16), vcat, preferred_element_type=f32) / ssum
            lse_ref[g, cur, :] = jnp.broadcast_to(m + jnp.log(ssum), (BLK, HEAD_DIM))
            return carry

        lax.fori_loop(0, d * nb, blk_body, 0, unroll=unroll)

    def merge_body(cb, carry):
        rows = pl.ds(pl.multiple_of(cb * BLK, BLK), BLK)
        lses = [lse_ref[g, rows, :] for g in range(len(rates))]
        top = functools.reduce(jnp.maximum, lses)
        wts = [jnp.exp(x - top) for x in lses]
        num = sum(w * og_ref[g, rows, :] for g, w in enumerate(wts))
        gt = gate_ref[rows, :]
        o_ref[rows, :] = (num / sum(wts) * _silu(gt)).astype(o_ref.dtype)
        return carry

    lax.fori_loop(0, seq // BLK, merge_body, 0)


def _attend_prompt_b(q, k, v, gate, bias_tab, batch, seq):
    n_heads = k.shape[1] // HEAD_DIM
    n_dil = len(DIL_RATES)
    blk = (seq, HEAD_DIM)
    qspecs = [pl.BlockSpec(blk, lambda b, h, g=g: (b, g * n_heads + h)) for g in range(n_dil)]
    hspec = pl.BlockSpec(blk, lambda b, h: (b, h))
    return pl.pallas_call(
        functools.partial(_attend_prompt_b_kernel, rates=DIL_RATES, scale=HEAD_DIM ** -0.5, unroll=16),
        name="attend_prompt_b", grid=(batch, n_heads),
        in_specs=qspecs + [hspec, hspec, hspec,
                           pl.BlockSpec((1, BLK, bias_tab.shape[2]), lambda b, h: (h, 0, 0))],
        out_specs=hspec,
        out_shape=jax.ShapeDtypeStruct(k.shape, bf16),
        scratch_shapes=[pltpu.VMEM(blk, f32), pltpu.VMEM((n_dil,) + blk, f32), pltpu.VMEM((n_dil,) + blk, f32)],
        compiler_params=_params(2),
    )(q, q, q, k, v, gate, bias_tab)


def _attend_sample_b_kernel(q_ref, k_ref, v_ref, kt_ref, vt_ref, bias_ref, gate_ref, o_ref, *, n_dil, t_new, scale):
    wb, hgs, _ = k_ref.shape
    k2 = k_ref.reshape(wb * hgs, HEAD_DIM)
    v2 = v_ref.reshape(wb * hgs, HEAD_DIM)
    for hl in range(hgs):
        cols = slice(hl * HEAD_DIM, (hl + 1) * HEAD_DIM)
        head = pl.ds(hl, wb, stride=hgs)
        q = q_ref[0, hl]
        s_main = _nt_dot(q, k2[head, :].astype(bf16)) * scale + bias_ref[hl, :, :wb]
        s_tail = _nt_dot(q, kt_ref[0, :, cols].astype(bf16)) * scale + bias_ref[hl, :, wb:]
        m = jnp.maximum(s_main.max(axis=1, keepdims=True), s_tail.max(axis=1, keepdims=True))
        e_main = jnp.exp(s_main - m)
        e_tail = jnp.exp(s_tail - m)
        ssum = e_main.sum(axis=1, keepdims=True) + e_tail.sum(axis=1, keepdims=True)
        o = (jnp.dot(e_main.astype(bf16), v2[head, :].astype(bf16), preferred_element_type=f32)
             + jnp.dot(e_tail.astype(bf16), vt_ref[0, :, cols].astype(bf16), preferred_element_type=f32)) / ssum
        lse = m + jnp.log(ssum)
        lses = [lse[g * t_new:(g + 1) * t_new] for g in range(n_dil)]
        top = functools.reduce(jnp.maximum, lses)
        wts = [jnp.exp(x - top) for x in lses]
        num = sum(w * o[g * t_new:(g + 1) * t_new] for g, w in enumerate(wts))
        o_ref[0, hl] = num / sum(wts) * _silu(gate_ref[0, hl])


def _attend_sample_b(q16, buf_k, buf_v, layer, k_tail, v_tail, bias_tab, gate4, t_new):
    dbatch, n_heads, rows, _ = q16.shape
    wb = buf_k.shape[2]
    hgs = 8 if n_heads % 8 == 0 else n_heads
    grouped = lambda a: a.reshape(a.shape[0], dbatch, wb, n_heads // hgs, hgs, HEAD_DIM)
    bufspec = pl.BlockSpec((None, None, wb, None, hgs, HEAD_DIM), lambda b, hg: (layer, b, 0, hg, 0, 0))
    tailspec = pl.BlockSpec((1, LANE, hgs * HEAD_DIM), lambda b, hg: (b, 0, hg))
    hspec = lambda r: pl.BlockSpec((1, hgs, r, HEAD_DIM), lambda b, hg: (b, hg, 0, 0))
    return pl.pallas_call(
        functools.partial(_attend_sample_b_kernel, n_dil=len(DIL_RATES), t_new=t_new, scale=HEAD_DIM ** -0.5),
        name="attend_sample_b", grid=(dbatch, n_heads // hgs),
        in_specs=[hspec(rows), bufspec, bufspec, tailspec, tailspec,
                  pl.BlockSpec((hgs, rows, wb + LANE), lambda b, hg: (hg, 0, 0)),
                  hspec(t_new)],
        out_specs=hspec(t_new),
        out_shape=jax.ShapeDtypeStruct((dbatch, n_heads, t_new, HEAD_DIM), f32),
        compiler_params=_params(2),
    )(q16, grouped(buf_k), grouped(buf_v), k_tail, v_tail, bias_tab, gate4)


SHIFT_RING = 4
SHIFT_LAG = 2
SHIFT_CHUNK_BYTES = 5 << 20


def _shift_kernel(bk_ref, bv_ref, nk_ref, nv_ref, ok_ref, ov_ref, ring, in_sem, out_sem, new_sem,
                  *, layer, dbatch, rows, shift, chunk):
    reads, writes, tails = [], [], []
    n_chunks = (rows - shift) // chunk
    for a, (src, new, dst) in enumerate(((bk_ref, nk_ref, ok_ref), (bv_ref, nv_ref, ov_ref))):
        for b in range(dbatch):
            for c in range(n_chunks):
                slot = len(reads) % SHIFT_RING
                reads.append(pltpu.make_async_copy(src.at[layer, b, pl.ds(shift + c * chunk, chunk)],
                                                   ring.at[slot], in_sem.at[slot]))
                writes.append(pltpu.make_async_copy(ring.at[slot], dst.at[b, pl.ds(c * chunk, chunk)],
                                                    out_sem.at[slot]))
            tails.append(pltpu.make_async_copy(new.at[b], dst.at[b, pl.ds(rows - shift, shift)], new_sem.at[a, b]))
    for t in tails:
        t.start()
    n = len(reads)
    for i in range(n + SHIFT_LAG):
        if i < n:
            if i >= SHIFT_RING:
                writes[i - SHIFT_RING].wait()
            reads[i].start()
        if 0 <= i - SHIFT_LAG < n:
            reads[i - SHIFT_LAG].wait()
            writes[i - SHIFT_LAG].start()
    for i in range(max(0, n - SHIFT_RING), n):
        writes[i].wait()
    for t in tails:
        t.wait()


def _shift_buffers(buf_k, buf_v, layer, new_k, new_v):
    _, dbatch, rows, width = buf_k.shape
    shift = new_k.shape[1]
    body = (rows - shift) // shift
    per = max(u for u in range(1, body + 1) if body % u == 0 and u * shift * width * 4 <= SHIFT_CHUNK_BYTES)
    chunk = per * shift
    anyspec = pl.BlockSpec(memory_space=pl.ANY)
    out = jax.ShapeDtypeStruct((dbatch, rows, width), buf_k.dtype)
    return pl.pallas_call(
        functools.partial(_shift_kernel, layer=layer, dbatch=dbatch, rows=rows, shift=shift, chunk=chunk),
        name="shift_buffers",
        in_specs=[anyspec] * 4, out_specs=[anyspec] * 2, out_shape=[out, out],
        scratch_shapes=[pltpu.VMEM((SHIFT_RING, chunk, width), buf_k.dtype),
                        pltpu.SemaphoreType.DMA((SHIFT_RING,)), pltpu.SemaphoreType.DMA((SHIFT_RING,)),
                        pltpu.SemaphoreType.DMA((2, dbatch))],
        compiler_params=pltpu.CompilerParams(vmem_limit_bytes=VMEM_LIMIT),
    )(buf_k, buf_v, new_k, new_v)


def _offsets(sizes):
    return [int(x) for x in np.cumsum((0,) + tuple(sizes))[:-1]]


def _pad_rows(a, rows):
    return jnp.pad(a, ((0, 0), (0, rows - a.shape[1])) + ((0, 0),) * (a.ndim - 2))


def _layer_a(xp, xs, dims, j, cache_k, cache_v, cache_ki, page_table, w_in, w_out, ln_g, ln_b, rel_bias):
    batch, seq, dbatch, t_new = dims
    (xp, xpb), (xs, xsb) = xp, xs
    d_model = xp.shape[1]
    n_heads = d_model // HEAD_DIM
    kv, hi = KV_HEADS_A, IDX_HEADS
    group = n_heads // kv
    branch = n_heads * HEAD_DIM
    sizes = (branch, kv * HEAD_DIM, kv * HEAD_DIM, branch, hi * IDX_DIM, IDX_DIM, hi)
    offs = _offsets(sizes)
    tail_w = 2 * LANE
    w_tail = jnp.pad(w_in[j:j + 1, :, offs[5]:], ((0, 0), (0, 0), (0, tail_w - IDX_DIM - hi)))

    mm = lambda slot, dt, nm: _matmul(xpb, xsb, w_in, j, offs[slot], sizes[slot], dt, f"proj_a_{nm}")
    (q, q_s), (k_p, k_s), (v_p, v_s) = mm(0, bf16, "q"), mm(1, f32, "k"), mm(2, f32, "v")
    (gate, gate_s), (qi, qi_s) = mm(3, f32, "gate"), mm(4, bf16, "qi")
    kw, kw_s = _matmul(xpb, xsb, w_tail, 0, 0, tail_w, f32, "proj_a_kiwi")
    ki_p, wi = kw[:, :IDX_DIM], kw[:, IDX_DIM:IDX_DIM + hi]
    ki_s, wi_s = kw_s[:, :IDX_DIM], kw_s[:, IDX_DIM:IDX_DIM + hi]

    dist = (np.arange(ATTN_A_TQ)[:, None] + seq - ATTN_A_TQ) - np.arange(seq + ATTN_A_CHUNK - ATTN_A_TQ)[None, :]
    bias_p = _bias_table(rel_bias, _bucket_codes(dist))
    maskadd = _select_prompt(qi, ki_p, wi.T, batch, seq, min(TOPK_MAX, seq // 4))
    og = _attend_prompt(q, k_p, v_p, maskadd, bias_p, gate, batch, seq)

    q, gate, qi, wi = q_s, gate_s, qi_s, wi_s
    n_pages = page_table.shape[1]
    past = n_pages * PAGE_SIZE
    n_keys = past + t_new
    n_lanes = past + PAGE_SIZE
    cscale = (IDX_DIM ** -0.5) * (hi ** -0.5)
    qs = qi.reshape(dbatch, t_new * hi, IDX_DIM)
    wbc = jnp.broadcast_to((wi * cscale).reshape(dbatch, t_new * hi, 1), (dbatch, t_new * hi, LANE))
    ki_tail = _pad_rows(ki_s.reshape(dbatch, t_new, IDX_DIM), PAGE_SIZE)
    scores = _sample_scores(page_table, qs, wbc, cache_ki, j, ki_tail, t_new)
    mask_s = _select_sample(scores.reshape(dbatch * t_new, n_lanes), past, t_new, min(TOPK_MAX, n_keys // 4))

    def to_rows(a):
        a = a.reshape(dbatch, t_new, kv, group, HEAD_DIM)
        return a.transpose(0, 2, 3, 1, 4).reshape(dbatch, kv * group * t_new, HEAD_DIM)

    def to_page(a):
        return _pad_rows(a.reshape(dbatch, t_new, kv, HEAD_DIM), PAGE_SIZE).reshape(dbatch, PAGE_SIZE * kv, HEAD_DIM)

    rows = group * t_new
    dist = (past + np.arange(t_new)[:, None]) - np.arange(n_lanes)[None, :]
    bias_s = _bias_table(rel_bias, _bucket_codes(dist)).reshape(kv * rows, n_lanes)
    mask_g = jnp.broadcast_to(mask_s.reshape(dbatch, 1, t_new, n_lanes), (dbatch, group, t_new, n_lanes))
    mask_g = mask_g.reshape(dbatch, rows, n_lanes)
    pool = cache_k.shape[1]
    paged = lambda c: c.reshape(c.shape[0], pool, PAGE_SIZE * kv, HEAD_DIM)
    og2 = _attend_sample_a(page_table, to_rows(q), bias_s, mask_g, paged(cache_k), paged(cache_v), j,
                           to_page(k_s), to_page(v_s), to_rows(gate))
    og_s = og2.reshape(dbatch, kv, group, t_new, HEAD_DIM).transpose(0, 3, 1, 2, 4)
    og_s = og_s.reshape(dbatch * t_new, branch).astype(bf16)
    y, y_s = _matmul(og, og_s, w_out, j, 0, d_model, f32, "out_a")
    xp_new = _layernorm(y, xp, ln_g[j], ln_b[j])
    xs_new = _layernorm(y_s, xs, ln_g[j], ln_b[j])

    shp = lambda a, b_, t_, *rest: a.reshape(b_, t_, *rest)
    outs = (shp(k_p, batch, seq, kv, HEAD_DIM), shp(v_p, batch, seq, kv, HEAD_DIM), shp(ki_p, batch, seq, IDX_DIM),
            shp(k_s, dbatch, t_new, kv, HEAD_DIM), shp(v_s, dbatch, t_new, kv, HEAD_DIM),
            shp(ki_s, dbatch, t_new, IDX_DIM))
    return xp_new, xs_new, outs


def _layer_b(xp, xs, dims, j, buf_k, buf_v, w_in, w_out, ln_g, ln_b, rel_bias):
    batch, seq, dbatch, t_new = dims
    (xp, xpb), (xs, xsb) = xp, xs
    d_model = xp.shape[1]
    n_heads = d_model // HEAD_DIM
    branch = n_heads * HEAD_DIM
    n_dil = len(DIL_RATES)
    sizes = (n_dil * branch, branch, branch, branch)
    offs = _offsets(sizes)

    mm = lambda slot, dt, nm: _matmul(xpb, xsb, w_in, j, offs[slot], sizes[slot], dt, f"proj_b_{nm}")
    (q, q_s), (gate, gate_s) = mm(0, bf16, "q"), mm(3, f32, "gate")
    k_p, k_s, k_heads = _matmul_heads(xpb, xsb, w_in, j, offs[1], n_heads, "proj_b_k")
    v_p, v_s, v_heads = _matmul_heads(xpb, xsb, w_in, j, offs[2], n_heads, "proj_b_v")

    m =(np.arange(BLK)[:, None] + BLK) - np.arange(2 * BLK)[None, :]
    codes = [_bucket_codes(d * np.clip(m, 0, w // d), (m >= 0) & (m <= w // d))
             for d, w in zip(DIL_RATES, DIL_WINDOWS)]
    bias_p = _bias_table(rel_bias, np.concatenate(codes, axis=1))
    og = _attend_prompt_b(q, k_p, v_p, gate, bias_p, batch, seq)

    q, gate = q_s, gate_s
    wb = buf_k.shape[2]
    rows = -(-(n_dil * t_new) // 16) * 16
    pos = np.arange(wb + LANE)[None, :]
    codes = []
    for d, w in zip(DIL_RATES, DIL_WINDOWS):
        dist = wb + np.arange(t_new)[:, None] - pos
        codes.append(_bucket_codes(dist, (dist >= 0) & (dist % d == 0) & (dist // d <= w // d) & (pos < wb + t_new)))
    codes.append(np.full((rows - n_dil * t_new, wb + LANE), BUCKET_ZERO, np.int32))
    bias_s = _bias_table(rel_bias, np.concatenate(codes, axis=0))
    q16 = q.reshape(dbatch, t_new, n_dil, n_heads, HEAD_DIM).transpose(0, 3, 2, 1, 4)
    q16 = q16.reshape(dbatch, n_heads, n_dil * t_new, HEAD_DIM)
    q16 = jnp.pad(q16, ((0, 0), (0, 0), (0, rows - n_dil * t_new), (0, 0)))
    gate4 = gate.reshape(dbatch, t_new, n_heads, HEAD_DIM).transpose(0, 2, 1, 3)
    k_tail = _pad_rows(k_s.reshape(dbatch, t_new, branch), LANE)
    v_tail = _pad_rows(v_s.reshape(dbatch, t_new, branch), LANE)
    og4 = _attend_sample_b(q16, buf_k, buf_v, j, k_tail, v_tail, bias_s, gate4, t_new)
    og_s = og4.transpose(0, 2, 1, 3).reshape(dbatch * t_new, branch).astype(bf16)
    y, y_s = _matmul(og, og_s, w_out, j, 0, d_model, f32, "out_b")
    xp_new = _layernorm(y, xp, ln_g[j], ln_b[j])
    xs_new = _layernorm(y_s, xs, ln_g[j], ln_b[j])

    flat = lambda buf: buf.reshape(buf.shape[0], dbatch, wb * n_heads, HEAD_DIM)
    new_k, new_v = _shift_buffers(flat(buf_k), flat(buf_v), j, k_s.reshape(dbatch, t_new * n_heads, HEAD_DIM),
                                  v_s.reshape(dbatch, t_new * n_heads, HEAD_DIM))
    wp = min(W_MAX, seq)
    k_win = k_heads.reshape(batch, seq, n_heads, HEAD_DIM)[:, seq - wp:]
    v_win = v_heads.reshape(batch, seq, n_heads, HEAD_DIM)[:, seq - wp:]
    return xp_new, xs_new, (k_win, v_win, new_k.reshape(dbatch, wb, n_heads, HEAD_DIM),
                            new_v.reshape(dbatch, wb, n_heads, HEAD_DIM))


def kernel(x_prompt, x_sample, cache_k_a, cache_v_a, cache_kidx_a, cache_k_b, cache_v_b, page_table, rel_bias,
           w_in_a, w_out_a, ln_g_a, ln_b_a, w_in_b, w_out_b, ln_g_b, ln_b_b):
    batch, seq, d_model = x_prompt.shape
    dbatch, t_new, _ = x_sample.shape
    dims = (batch, seq, dbatch, t_new)
    with_bf16 = lambda a: (a, a.astype(bf16))
    xp = with_bf16(x_prompt.reshape(batch * seq, d_model))
    xs = with_bf16(x_sample.reshape(dbatch * t_new, d_model))
    outs_a, outs_b = [], []
    for layer in range(DEPTH):
        j = layer // 2
        if layer % 2 == 0:
            xp, xs, o = _layer_a(xp, xs, dims, j, cache_k_a, cache_v_a, cache_kidx_a, page_table,
                                 w_in_a, w_out_a, ln_g_a, ln_b_a, rel_bias)
            outs_a.append(o)
        else:
            xp, xs, o = _layer_b(xp, xs, dims, j, cache_k_b, cache_v_b, w_in_b, w_out_b, ln_g_b, ln_b_b, rel_bias)
            outs_b.append(o)
    stack = lambda group, idx: group[0][idx][None] if len(group) == 1 else jnp.stack([o[idx] for o in group])
    return (xp[0].reshape(batch, seq, d_model), xs[0].reshape(dbatch, t_new, d_model),
            *(stack(outs_a, n) for n in range(6)), *(stack(outs_b, n) for n in range(4)))
```

```python
import functools
import math

import jax
import jax.numpy as jnp
import numpy as np
from jax import lax
from jax.experimental import pallas as pl
from jax.experimental.pallas import tpu as pltpu

HEAD_DIM = 128
KV_HEADS_A = 8
IDX_HEADS = 64
IDX_DIM = 128
TOPK_MAX = 256
DIL_WINDOWS = (128, 512, 2048)
DIL_RATES = (1, 4, 16)
W_MAX = max(DIL_WINDOWS)
BLK = 128
N_BUCKETS = 32
REL_MAX_DIST = 2048
DEPTH = 2
ALPHA = (2 * DEPTH) ** 0.25
LN_EPS = 1e-5
PAGE_SIZE = 128

LANE = 128
NEG_INF = float("-inf")
INT_MIN = -(2 ** 31)
M_INIT = -1e30
BUCKET_ZERO = -1
BUCKET_MASKED = -2
VMEM_LIMIT = 56 * 1024 * 1024

f32 = jnp.float32
bf16 = jnp.bfloat16
i32 = jnp.int32


def _params(n_axes, vmem=VMEM_LIMIT):
    return pltpu.CompilerParams(dimension_semantics=("arbitrary",) * n_axes, vmem_limit_bytes=vmem)


def _nt_dot(a, b):
    return lax.dot_general(a, b, (((1,), (1,)), ((), ())), preferred_element_type=f32)


def _silu(x):
    return x * (1.0 / (1.0 + jnp.exp(-x)))


def _softmax_step(s, m_old, l_old):
    m_new = jnp.maximum(m_old, s.max(axis=1, keepdims=True))
    alpha = jnp.exp(m_old - m_new)
    p = jnp.exp(s - m_new)
    return m_new, alpha * l_old + p.sum(axis=1, keepdims=True), alpha, p


def _mm_kernel(x_ref, xs_ref, w_ref, o_ref, os_ref, wbf_ref, *, chunk, transposed):
    mul = _nt_dot if transposed else functools.partial(jnp.dot, preferred_element_type=f32)

    @pl.when(pl.program_id(1) == 0)
    def _():
        def body(r, c):
            rows = pl.ds(pl.multiple_of(r * chunk, chunk), chunk)
            wbf_ref[rows, :] = w_ref[rows, :].astype(bf16)
            return c
        lax.fori_loop(0, w_ref.shape[0] // chunk, body, 0)
        os_ref[...] = mul(xs_ref[...], wbf_ref[...]).astype(os_ref.dtype)

    o_ref[...] = mul(x_ref[...], wbf_ref[...]).astype(o_ref.dtype)


def _pick_tile(n, candidates):
    for c in candidates:
        if n % c == 0:
            return c
    raise ValueError(f"no tile for {n}")


def _matmul(x_bf, xs_bf, w, layer, col_off, ncols, out_dtype, name, transposed=False):
    m, k = x_bf.shape
    ms = xs_bf.shape[0]
    tn = _pick_tile(math.gcd(col_off, ncols) if col_off else ncols, (512, 256, 128))
    tm = _pick_tile(m, (1024, 512, 256, 128))
    off_blocks = col_off // tn
    if transposed:
        wblock, wspec, chunk = (tn, k), pl.BlockSpec((None, tn, k), lambda n, i: (layer, n + off_blocks, 0)), LANE
    else:
        wblock, wspec = (k, tn), pl.BlockSpec((None, k, tn), lambda n, i: (layer, 0, n + off_blocks))
        chunk = _pick_tile(k, (256, 128))
    return pl.pallas_call(
        functools.partial(_mm_kernel, chunk=chunk, transposed=transposed), name=name,
        grid=(ncols // tn, m // tm),
        in_specs=[pl.BlockSpec((tm, k), lambda n, i: (i, 0)),
                  pl.BlockSpec((ms, k), lambda n, i: (0, 0)),
                  wspec],
        out_specs=[pl.BlockSpec((tm, tn), lambda n, i: (i, n)),
                   pl.BlockSpec((ms, tn), lambda n, i: (0, n))],
        out_shape=[jax.ShapeDtypeStruct((m, ncols), out_dtype), jax.ShapeDtypeStruct((ms, ncols), out_dtype)],
        scratch_shapes=[pltpu.VMEM(wblock, bf16)],
        compiler_params=_params(2),
    )(x_bf, xs_bf, w)


def _mm_heads_kernel(x_ref, xs_ref, w_ref, o_ref, os_ref, oh_ref, wbf_ref, *, k_chunk):
    @pl.when(pl.program_id(1) == 0)
    def _():
        def body(r, c):
            rows = pl.ds(pl.multiple_of(r * k_chunk, k_chunk), k_chunk)
            wbf_ref[rows, :] = w_ref[rows, :].astype(bf16)
            return c
        lax.fori_loop(0, w_ref.shape[0] // k_chunk, body, 0)
        os_ref[...] = jnp.dot(xs_ref[...], wbf_ref[...], preferred_element_type=f32)

    res = jnp.dot(x_ref[...], wbf_ref[...], preferred_element_type=f32)
    o_ref[...] = res
    tm, nh, dh = oh_ref.shape
    flat = oh_ref.reshape(tm * nh, dh)
    for hh in range(nh):
        flat[pl.ds(hh, tm, stride=nh), :] = res[:, hh * dh:(hh + 1) * dh]


def _matmul_heads(x_bf, xs_bf, w, layer, col_off, n_heads, name):
    m, k = x_bf.shape
    ms = xs_bf.shape[0]
    nh = 8 if n_heads % 8 == 0 else n_heads
    tn = nh * HEAD_DIM
    ncols = n_heads * HEAD_DIM
    tm = _pick_tile(m, (512, 256, 128))
    off_blocks = col_off // tn
    k_chunk = _pick_tile(k, (256, 128))
    return pl.pallas_call(
        functools.partial(_mm_heads_kernel, k_chunk=k_chunk), name=name,
        grid=(ncols // tn, m // tm),
        in_specs=[pl.BlockSpec((tm, k), lambda n, i: (i, 0)),
                  pl.BlockSpec((ms, k), lambda n, i: (0, 0)),
                  pl.BlockSpec((None, k, tn), lambda n, i: (layer, 0, n + off_blocks),
                               pipeline_mode=pl.Buffered(1))],
        out_specs=[pl.BlockSpec((tm, tn), lambda n, i: (i, n)),
                   pl.BlockSpec((ms, tn), lambda n, i: (0, n)),
                   pl.BlockSpec((tm, nh, HEAD_DIM), lambda n, i: (i, n, 0))],
        out_shape=[jax.ShapeDtypeStruct((m, ncols), f32), jax.ShapeDtypeStruct((ms, ncols), f32),
                   jax.ShapeDtypeStruct((m, n_heads, HEAD_DIM), f32)],
        scratch_shapes=[pltpu.VMEM((k, tn), bf16)],
        compiler_params=_params(2),
    )(x_bf, xs_bf, w)


def _ln_kernel(y_ref, x_ref, g_ref, b_ref, o_ref, obf_ref):
    z = ALPHA * x_ref[...] + y_ref[...]
    mu = jnp.mean(z, axis=-1, keepdims=True)
    zc = z - mu
    var = jnp.mean(zc * zc, axis=-1, keepdims=True)
    out = zc * lax.rsqrt(var + LN_EPS) * g_ref[...] + b_ref[...]
    o_ref[...] = out
    obf_ref[...] = out.astype(bf16)


def _layernorm(y, x, g, b):
    m, d = x.shape
    tm = _pick_tile(m, (128, 64, 32, 16, 8))
    row = pl.BlockSpec((tm, d), lambda i: (i, 0))
    vec = pl.BlockSpec((1, d), lambda i: (0, 0))
    return pl.pallas_call(
        _ln_kernel, grid=(m // tm,), name="layernorm",
        in_specs=[row, row, vec, vec], out_specs=[row, row],
        out_shape=[jax.ShapeDtypeStruct((m, d), f32), jax.ShapeDtypeStruct((m, d), bf16)],
        compiler_params=_params(1),
    )(y, x, g.reshape(1, d), b.reshape(1, d))


def _rel_bucket_np(n):
    max_exact = N_BUCKETS // 2
    nf = np.maximum(n, 1).astype(np.float64)
    large = max_exact + (np.log(nf / max_exact) / math.log(REL_MAX_DIST / max_exact)
                         * (N_BUCKETS - max_exact)).astype(np.int64)
    large = np.minimum(large, N_BUCKETS - 1)
    return np.where(n < max_exact, n, large).astype(np.int32)


def _bucket_codes(dist, valid=None):
    code = np.where(dist >= 0, _rel_bucket_np(np.maximum(dist, 0)), BUCKET_ZERO)
    if valid is not None:
        code = np.where(valid, code, BUCKET_MASKED)
    return code.astype(np.int32)


def _bias_kernel(tbl_ref, bm_ref, o_ref, *, cw):
    h = pl.program_id(0)
    row = tbl_ref[pl.ds(h, 1), :]

    def body(c, carry):
        for t in range(cw // LANE):
            cols = pl.ds(pl.multiple_of(c * cw + t * LANE, LANE), LANE)
            code = bm_ref[:, cols]
            val = jnp.take_along_axis(jnp.broadcast_to(row, code.shape), jnp.maximum(code, 0), axis=1)
            o_ref[0, :, cols] = jnp.where(code >= 0, val, jnp.where(code == BUCKET_MASKED, NEG_INF, 0.0))
        return carry

    lax.fori_loop(0, bm_ref.shape[1] // cw, body, 0, unroll=8)


def _bias_table(rel_bias, codes):
    r, c = codes.shape
    n_heads = rel_bias.shape[1]
    units = c // LANE
    per = max(1, 16 // -(-r // 8))
    cw = LANE * max(u for u in range(1, units + 1) if units % u == 0 and u <= per)
    return pl.pallas_call(
        functools.partial(_bias_kernel, cw=cw), name="bias_table",
        grid=(n_heads,),
        in_specs=[pl.BlockSpec((n_heads, LANE), lambda h: (0, 0)),
                  pl.BlockSpec((r, c), lambda h: (0, 0))],
        out_specs=pl.BlockSpec((1, r, c), lambda h: (h, 0, 0)),
        out_shape=jax.ShapeDtypeStruct((n_heads, r, c), f32),
        compiler_params=_params(1),
    )(jnp.pad(rel_bias.T, ((0, 0), (0, LANE - N_BUCKETS))), jnp.asarray(codes))


def _sortable_key(x):
    bits = pltpu.bitcast(x, i32)
    return bits ^ ((bits >> 31) & 0x7FFFFFFF)


def _kth_largest_key(count_ge, shape, k_sel):
    def body(it, t):
        cand = t + lax.shift_left(jnp.int32(1), 31 - it)
        return jnp.where(count_ge(cand) >= k_sel, cand, t)
    return lax.fori_loop(0, 32, body, jnp.full(shape, INT_MIN, i32))


def _select_prompt_kernel(qi_ref, ki_ref, wt_ref, m_ref, kibf_ref, key_ref, *, n_idx_heads, k_sel, ck, hb, cscale):
    i = pl.program_id(1)
    tq = qi_ref.shape[0]

    @pl.when(i == 0)
    def _():
        kibf_ref[...] = ki_ref[...].astype(bf16)

    nck = ((i + 1) * tq + ck - 1) // ck
    t_row = i * tq + lax.broadcasted_iota(i32, (1, tq), 1)

    def chunk_body(c, carry):
        koff = pl.multiple_of(c * ck, ck)
        kc = kibf_ref[pl.ds(koff, ck), :]

        def heads_body(hg, acc):
            hoff = hg * (hb * IDX_DIM)
            qstack = jnp.concatenate(
                [qi_ref[:, pl.ds(pl.multiple_of(hoff + j * IDX_DIM, IDX_DIM), IDX_DIM)] for j in range(hb)], axis=0)
            s = _nt_dot(kc, qstack)
            w = wt_ref[pl.ds(pl.multiple_of(hg * hb, hb), hb), :] * cscale
            for j in range(hb):
                acc = acc + jnp.maximum(s[:, j * tq:(j + 1) * tq], 0.0) * w[j:j + 1, :]
            return acc

        acc = lax.fori_loop(0, n_idx_heads // hb, heads_body, jnp.zeros((ck, tq), f32), unroll=2)
        l_col = koff + lax.broadcasted_iota(i32, (ck, 1), 0)
        key_ref[pl.ds(koff, ck), :] = jnp.where(l_col <= t_row, _sortable_key(acc), INT_MIN)
        return carry

    lax.fori_loop(0, nck, chunk_body, 0)

    def count_ge(cand):
        def body(c, cnt):
            kc = key_ref[pl.ds(pl.multiple_of(c * ck, ck), ck), :]
            hit = jnp.where(kc >= cand, 1, 0).astype(i32)
            return cnt + hit.reshape(ck // 8, 8, tq).sum(axis=0)
        cnt = lax.fori_loop(0, nck, body, jnp.zeros((8, tq), i32))
        return cnt.sum(axis=0, keepdims=True)

    thr = _kth_largest_key(count_ge, (1, tq), k_sel)
    thr = jnp.maximum(thr, INT_MIN + 1)

    m_ref[...] = jnp.full(m_ref.shape, NEG_INF, m_ref.dtype)

    def out_body(c, carry):
        koff = pl.multiple_of(c * ck, ck)
        sel = jnp.where(key_ref[pl.ds(koff, ck), :] >= thr, 0.0, NEG_INF).astype(f32)
        m_ref[:, pl.ds(koff, ck)] = sel.T.astype(m_ref.dtype)
        return carry

    lax.fori_loop(0, nck, out_body, 0)


def _select_prompt(qi, ki, wi_t, batch, seq, k_sel):
    n_idx_heads = wi_t.shape[0]
    tq, ck = BLK, 256
    hb = _pick_tile(n_idx_heads, (16, 8, 4, 2, 1))
    nblk = seq // tq
    cscale = (IDX_DIM ** -0.5) * (n_idx_heads ** -0.5)
    return pl.pallas_call(
        functools.partial(_select_prompt_kernel, n_idx_heads=n_idx_heads, k_sel=k_sel, ck=ck, hb=hb, cscale=cscale),
        name="select_prompt", grid=(batch, nblk),
        in_specs=[pl.BlockSpec((tq, n_idx_heads * IDX_DIM), lambda b, i: (b * nblk + i, 0)),
                  pl.BlockSpec((seq, IDX_DIM), lambda b, i: (b, 0)),
                  pl.BlockSpec((n_idx_heads, tq), lambda b, i: (0, b * nblk + i))],
        out_specs=pl.BlockSpec((tq, seq), lambda b, i: (b * nblk + i, 0)),
        out_shape=jax.ShapeDtypeStruct((batch * seq, seq), bf16),
        scratch_shapes=[pltpu.VMEM((seq, IDX_DIM), bf16), pltpu.VMEM((seq, tq), i32)],
        compiler_params=_params(2),
    )(qi, ki, wi_t)


def _attend_prompt_kernel(q_ref, k_ref, v_ref, mk_ref, g_ref, gate_ref, o_ref,
                          kbf_ref, vbf_ref, s_ref, mp_ref, l_ref, acc_ref, *, group, ck, sub, scale):
    i = pl.program_id(2)
    nblk = pl.num_programs(2)
    tq = q_ref.shape[0]
    rows = group * tq

    @pl.when(i == 0)
    def _():
        kbf_ref[...] = k_ref[...].astype(bf16)
        vbf_ref[...] = v_ref[...].astype(bf16)

    nck = ((i + 1) * tq + ck - 1) // ck
    goff0 = (nblk - 1 - i) * tq
    qs = jnp.concatenate([q_ref[:, g * HEAD_DIM:(g + 1) * HEAD_DIM] for g in range(group)], axis=0)

    mp_ref[...] = jnp.full(mp_ref.shape, M_INIT, f32)

    def logits_body(c, carry):
        for u in range(ck // sub):
            koff = pl.multiple_of(c * ck + u * sub, sub)
            goff = pl.multiple_of(goff0 + koff, LANE)
            s = _nt_dot(qs, kbf_ref[pl.ds(koff, sub), :]) * scale
            bias = jnp.concatenate([g_ref[g, :, pl.ds(goff, sub)] for g in range(group)], axis=0)
            sel = mk_ref[:, pl.ds(koff, sub)].astype(f32)
            s = s + bias + jnp.concatenate([sel] * group, axis=0)
            s_ref[:, pl.ds(koff, sub)] = s
            mp_ref[...] = functools.reduce(jnp.maximum, [mp_ref[...]] + [s[:, t * LANE:(t + 1) * LANE]
                                                                         for t in range(sub // LANE)])
        return carry

    lax.fori_loop(0, nck, logits_body, 0)
    m_all = jnp.broadcast_to(mp_ref[...].max(axis=1, keepdims=True), (rows, LANE))

    l_ref[...] = jnp.zeros(l_ref.shape, f32)
    acc_ref[...] = jnp.zeros(acc_ref.shape, f32)

    def pv_body(c, carry):
        for u in range(ck // sub):
            koff = pl.multiple_of(c * ck + u * sub, sub)
            ps = [jnp.exp(s_ref[:, pl.ds(pl.multiple_of(koff + t * LANE, LANE), LANE)] - m_all)
                  for t in range(sub // LANE)]
            l_ref[...] += functools.reduce(jnp.add, ps)
            p = jnp.concatenate(ps, axis=1).astype(bf16)
            acc_ref[...] += jnp.dot(p, vbf_ref[pl.ds(koff, sub), :], preferred_element_type=f32)
        return carry

    lax.fori_loop(0, nck, pv_body, 0)
    o = acc_ref[...] / l_ref[...].sum(axis=1, keepdims=True)
    for g in range(group):
        cols = slice(g * HEAD_DIM, (g + 1) * HEAD_DIM)
        o_ref[:, cols] = (o[g * tq:(g + 1) * tq] * _silu(gate_ref[:, cols])).astype(o_ref.dtype)


ATTN_A_CHUNK = 512
ATTN_A_TQ = 256


def _attend_prompt(q, k, v, maskadd, bias_tab, gate, batch, seq):
    kv = k.shape[1] // HEAD_DIM
    group = q.shape[1] // (kv * HEAD_DIM)
    tq, ck = ATTN_A_TQ, ATTN_A_CHUNK
    nblk = seq // tq
    gw = group * HEAD_DIM
    qspec = pl.BlockSpec((tq, gw), lambda kh, b, i: (b * nblk + i, kh))
    kvspec = pl.BlockSpec((seq, HEAD_DIM), lambda kh, b, i: (b, kh))
    return pl.pallas_call(
        functools.partial(_attend_prompt_kernel, group=group, ck=ck, sub=256, scale=HEAD_DIM ** -0.5),
        name="attend_prompt_a", grid=(kv, batch, nblk),
        in_specs=[qspec, kvspec, kvspec,
                  pl.BlockSpec((tq, seq), lambda kh, b, i: (b * nblk + i, 0)),
                  pl.BlockSpec((group, tq, bias_tab.shape[2]), lambda kh, b, i: (kh, 0, 0)),
                  qspec],
        out_specs=qspec,
        out_shape=jax.ShapeDtypeStruct(q.shape, bf16),
        scratch_shapes=[pltpu.VMEM((seq, HEAD_DIM), bf16), pltpu.VMEM((seq, HEAD_DIM), bf16),
                        pltpu.VMEM((group * tq, seq), f32), pltpu.VMEM((group * tq, LANE), f32),
                        pltpu.VMEM((group * tq, LANE), f32), pltpu.VMEM((group * tq, HEAD_DIM), f32)],
        compiler_params=_params(3),
    )(q, k, v, maskadd, bias_tab, gate)


def _score_page(qs, wb, kpage, t_new):
    s = _nt_dot(qs, kpage.astype(bf16))
    x = jnp.maximum(s, 0.0) * wb
    return x.reshape(t_new, x.shape[0] // t_new, x.shape[1]).sum(axis=1)


def _sample_scores_kernel(pt_ref, qs_ref, wb_ref, *refs, pg, t_new):
    o_ref = refs[pg]
    for r in range(pg):
        o_ref[0, :, r * PAGE_SIZE:(r + 1) * PAGE_SIZE] = _score_page(qs_ref[0], wb_ref[0], refs[r][...], t_new)


def _sample_scores_tail_kernel(qs_ref, wb_ref, kt_ref, o_ref, *, t_new):
    o_ref[0] = _score_page(qs_ref[0], wb_ref[0], kt_ref[0], t_new)


def _sample_scores(page_table, qs, wb, cache_ki, layer, ki_tail, t_new):
    dbatch, rows, _ = qs.shape
    n_pages = page_table.shape[1]
    pg = _pick_tile(n_pages, (8, 4, 2, 1))
    qspec = pl.BlockSpec((1, rows, IDX_DIM), lambda b, s, pt: (b, 0, 0))
    page_specs = [pl.BlockSpec((None, None, PAGE_SIZE, IDX_DIM),
                               lambda b, s, pt, r=r: (layer, pt[b, s * pg + r], 0, 0)) for r in range(pg)]
    main = pl.pallas_call(
        functools.partial(_sample_scores_kernel, pg=pg, t_new=t_new), name="sample_scores",
        grid_spec=pltpu.PrefetchScalarGridSpec(
            num_scalar_prefetch=1, grid=(dbatch, n_pages // pg),
            in_specs=[qspec, qspec] + page_specs,
            out_specs=pl.BlockSpec((1, t_new, pg * PAGE_SIZE), lambda b, s, pt: (b, 0, s))),
        out_shape=jax.ShapeDtypeStruct((dbatch, t_new, n_pages * PAGE_SIZE), f32),
        compiler_params=_params(2),
    )(page_table, qs, wb, *([cache_ki] * pg))
    spec3 = lambda d1, d2: pl.BlockSpec((1, d1, d2), lambda b: (b, 0, 0))
    tail = pl.pallas_call(
        functools.partial(_sample_scores_tail_kernel, t_new=t_new), name="sample_scores_tail",
        grid=(dbatch,),
        in_specs=[spec3(rows, IDX_DIM), spec3(rows, IDX_DIM), spec3(PAGE_SIZE, IDX_DIM)],
        out_specs=spec3(t_new, PAGE_SIZE),
        out_shape=jax.ShapeDtypeStruct((dbatch, t_new, PAGE_SIZE), f32),
        compiler_params=_params(1),
    )(qs, wb, ki_tail)
    return jnp.concatenate([main, tail], axis=2)


def _select_sample_kernel(s_ref, o_ref, key_ref, *, past, t_new, k_sel):
    rows, n = s_ref.shape
    t = past + lax.broadcasted_iota(i32, (rows, 1), 0) % t_new
    l = lax.broadcasted_iota(i32, (1, n), 1)
    key_ref[...] = jnp.where(l <= t, _sortable_key(s_ref[...]), INT_MIN)

    def count_ge(cand):
        return jnp.where(key_ref[...] >= cand, 1, 0).astype(i32).sum(axis=1, keepdims=True)

    thr = jnp.maximum(_kth_largest_key(count_ge, (rows, 1), k_sel), INT_MIN + 1)
    o_ref[...] = jnp.where(key_ref[...] >= thr, 0.0, NEG_INF).astype(f32)


def _select_sample(scores, past, t_new, k_sel):
    return pl.pallas_call(
        functools.partial(_select_sample_kernel, past=past, t_new=t_new, k_sel=k_sel), name="select_sample",
        out_shape=jax.ShapeDtypeStruct(scores.shape, f32),
        scratch_shapes=[pltpu.VMEM(scores.shape, i32)],
        compiler_params=pltpu.CompilerParams(vmem_limit_bytes=VMEM_LIMIT),
    )(scores)


def _attend_sample_a_kernel(pt_ref, q_ref, bias_ref, mask_ref, biast_ref, maskt_ref, gate_ref, *refs, pg, kv, scale):
    kpages, vpages = refs[:pg], refs[pg:2 * pg]
    kt_ref, vt_ref, o_ref, m_ref, l_ref, acc_ref = refs[2 * pg:]
    s_id = pl.program_id(1)
    rows = q_ref.shape[1] // kv

    @pl.when(s_id == 0)
    def _():
        m_ref[...] = jnp.full(m_ref.shape, M_INIT, f32)
        l_ref[...] = jnp.zeros(l_ref.shape, f32)
        acc_ref[...] = jnp.zeros(acc_ref.shape, f32)

    def head_rows(page_ref, kh):
        return page_ref[pl.ds(kh, PAGE_SIZE, stride=kv), :].astype(bf16)

    def update(k_refs, v_refs, bias, mask):
        s = jnp.concatenate(
            [jnp.concatenate([_nt_dot(q_ref[0, kh * rows:(kh + 1) * rows, :], head_rows(kr, kh)) for kr in k_refs],
                             axis=1) for kh in range(kv)], axis=0)
        s = s * scale + bias + jnp.concatenate([mask] * kv, axis=0)
        m_new, l_new, alpha, p = _softmax_step(s, m_ref[...], l_ref[...])
        p = p.astype(bf16)
        pv = []
        for kh in range(kv):
            pk = p[kh * rows:(kh + 1) * rows]
            pv.append(sum(jnp.dot(pk[:, r * PAGE_SIZE:(r + 1) * PAGE_SIZE], head_rows(vr, kh),
                                  preferred_element_type=f32) for r, vr in enumerate(v_refs)))
        acc_ref[...] = alpha * acc_ref[...] + jnp.concatenate(pv, axis=0)
        m_ref[...] = m_new
        l_ref[...] = l_new

    update(kpages, vpages, bias_ref[...], mask_ref[0])

    @pl.when(s_id == pl.num_programs(1) - 1)
    def _():
        update([kt_ref.at[0]], [vt_ref.at[0]], biast_ref[...], maskt_ref[0])
        o_ref[0] = acc_ref[...] / l_ref[...] * _silu(gate_ref[0])


def _attend_sample_a(page_table, q2, bias_tab, mask, cache_k, cache_v, layer, k_tail, v_tail, gate2):
    dbatch, qrows, _ = q2.shape
    kv = cache_k.shape[2] // PAGE_SIZE
    rows = qrows // kv
    n_pages = page_table.shape[1]
    past = n_pages * PAGE_SIZE
    pg = _pick_tile(n_pages, (4, 2, 1))
    qspec = pl.BlockSpec((1, qrows, HEAD_DIM), lambda b, s, pt: (b, 0, 0))
    kspecs = [pl.BlockSpec((None, None, PAGE_SIZE * kv, HEAD_DIM),
                           lambda b, s, pt, r=r: (layer, pt[b, s * pg + r], 0, 0)) for r in range(pg)]
    tailspec = pl.BlockSpec((1, PAGE_SIZE * kv, HEAD_DIM), lambda b, s, pt: (b, 0, 0))
    return pl.pallas_call(
        functools.partial(_attend_sample_a_kernel, pg=pg, kv=kv, scale=HEAD_DIM ** -0.5), name="attend_sample_a",
        grid_spec=pltpu.PrefetchScalarGridSpec(
            num_scalar_prefetch=1, grid=(dbatch, n_pages // pg),
            in_specs=[qspec,
                      pl.BlockSpec((qrows, pg * PAGE_SIZE), lambda b, s, pt: (0, s)),
                      pl.BlockSpec((1, rows, pg * PAGE_SIZE), lambda b, s, pt: (b, 0, s)),
                      pl.BlockSpec((qrows, PAGE_SIZE), lambda b, s, pt: (0, past // PAGE_SIZE)),
                      pl.BlockSpec((1, rows, PAGE_SIZE), lambda b, s, pt: (b, 0, past // PAGE_SIZE)),
                      qspec] + kspecs + kspecs + [tailspec, tailspec],
            out_specs=qspec,
            scratch_shapes=[pltpu.VMEM((qrows, 1), f32), pltpu.VMEM((qrows, 1), f32),
                            pltpu.VMEM((qrows, HEAD_DIM), f32)]),
        out_shape=jax.ShapeDtypeStruct(q2.shape, f32),
        compiler_params=_params(2),
    )(page_table, q2, bias_tab, mask, bias_tab, mask, gate2,
      *([cache_k] * pg), *([cache_v] * pg), k_tail, v_tail)


def _attend_prompt_b_kernel(q0_ref, q1_ref, q2_ref, k_ref, v_ref, gate_ref, gb_ref, *rest, rates, scale, unroll, shift):
    if shift is None:
        o_ref, qf_ref, og_ref, lse_ref = rest
    else:
        bk_ref, bv_ref, nk_ref, nv_ref, o_ref, ok_ref, ov_ref, qf_ref, og_ref, lse_ref, *ring_refs = rest
        step = pl.program_id(0) * pl.num_programs(1) + pl.program_id(1)
        _shift_ring_step(step, pl.num_programs(0) * pl.num_programs(1), (bk_ref, bv_ref), (nk_ref, nv_ref),
                         (ok_ref, ov_ref), *ring_refs, **shift)
    seq = k_ref.shape[0]
    col = lax.broadcasted_iota(i32, (BLK, 2 * BLK), 1)

    for g, (q_ref, d) in enumerate(zip((q0_ref, q1_ref, q2_ref), rates)):
        qf_ref[...] = q_ref[...].astype(f32)
        nb = seq // d // BLK
        bias = gb_ref[0, :, g * 2 * BLK:(g + 1) * 2 * BLK]

        def blk_body(it, carry, g=g, d=d, nb=nb, bias=bias):
            r = it // nb
            blk = it % nb
            start = r + blk * (BLK * d)
            start_prev = r + jnp.maximum(blk - 1, 0) * (BLK * d)
            cur = pl.ds(start, BLK, stride=d)
            prev = pl.ds(start_prev, BLK, stride=d)
            qb = qf_ref[cur, :].astype(bf16)
            kcat = jnp.concatenate([k_ref[prev, :], k_ref[cur, :]], axis=0).astype(bf16)
            vcat = jnp.concatenate([v_ref[prev, :], v_ref[cur, :]], axis=0).astype(bf16)
            s = _nt_dot(qb, kcat) * scale + bias
            s = jnp.where((col >= BLK) | (blk > 0), s, NEG_INF)
            m = s.max(axis=1, keepdims=True)
            e = jnp.exp(s - m)
            ssum = e.sum(axis=1, keepdims=True)
            og_ref[g, cur, :] = jnp.dot(e.astype(bf16), vcat, preferred_element_type=f32) / ssum
            lse_ref[g, cur, :] = jnp.broadcast_to(m + jnp.log(ssum), (BLK, HEAD_DIM))
            return carry

        lax.fori_loop(0, d * nb, blk_body, 0, unroll=unroll)

    def merge_body(cb, carry):
        rows = pl.ds(pl.multiple_of(cb * BLK, BLK), BLK)
        lses = [lse_ref[g, rows, :] for g in range(len(rates))]
        top = functools.reduce(jnp.maximum, lses)
        wts = [jnp.exp(x - top) for x in lses]
        num = sum(w * og_ref[g, rows, :] for g, w in enumerate(wts))
        gt = gate_ref[rows, :]
        o_ref[rows, :] = (num / sum(wts) * _silu(gt)).astype(o_ref.dtype)
        return carry

    lax.fori_loop(0, seq // BLK, merge_body, 0)


def _attend_prompt_b(q, k, v, gate, bias_tab, batch, seq, buffers=None):
    n_heads = k.shape[1] // HEAD_DIM
    n_dil = len(DIL_RATES)
    blk = (seq, HEAD_DIM)
    qspecs = [pl.BlockSpec(blk, lambda b, h, g=g: (b, g * n_heads + h)) for g in range(n_dil)]
    hspec = pl.BlockSpec(blk, lambda b, h: (b, h))
    in_specs = qspecs + [hspec, hspec, hspec, pl.BlockSpec((1, BLK, bias_tab.shape[2]), lambda b, h: (h, 0, 0))]
    out_specs, out_shape = [hspec], [jax.ShapeDtypeStruct(k.shape, bf16)]
    scratch = [pltpu.VMEM(blk, f32), pltpu.VMEM((n_dil,) + blk, f32), pltpu.VMEM((n_dil,) + blk, f32)]
    args, shift = [q, q, q, k, v, gate, bias_tab], None
    if buffers is not None:
        buf_k, buf_v, layer, new_k, new_v = buffers
        plan = _shift_plan(buf_k, new_k)
        if 2 * plan["dbatch"] * ((plan["rows"] - plan["shift"]) // plan["chunk"]) + SHIFT_RING <= batch * n_heads:
            shift = dict(plan, layer=layer)
            anyspec = pl.BlockSpec(memory_space=pl.ANY)
            out = jax.ShapeDtypeStruct((plan["dbatch"], plan["rows"], buf_k.shape[3]), buf_k.dtype)
            in_specs += [anyspec] * 4
            out_specs += [anyspec] * 2
            out_shape += [out, out]
            scratch += _shift_scratch(plan, buf_k)
            args += [buf_k, buf_v, new_k, new_v]
    res = pl.pallas_call(
        functools.partial(_attend_prompt_b_kernel, rates=DIL_RATES, scale=HEAD_DIM ** -0.5, unroll=16, shift=shift),
        name="attend_prompt_b", grid=(batch, n_heads),
        in_specs=in_specs, out_specs=out_specs, out_shape=out_shape, scratch_shapes=scratch,
        compiler_params=_params(2),
    )(*args)
    if shift is not None:
        return tuple(res)
    if buffers is not None:
        return (res[0],) + tuple(_shift_buffers(*buffers))
    return res[0], None, None


def _attend_sample_b_kernel(q_ref, k_ref, v_ref, kt_ref, vt_ref, bias_ref, gate_ref, o_ref, *, n_dil, t_new, scale):
    wb, hgs, _ = k_ref.shape
    k2 = k_ref.reshape(wb * hgs, HEAD_DIM)
    v2 = v_ref.reshape(wb * hgs, HEAD_DIM)
    for hl in range(hgs):
        cols = slice(hl * HEAD_DIM, (hl + 1) * HEAD_DIM)
        head = pl.ds(hl, wb, stride=hgs)
        q = q_ref[0, hl]
        s_main = _nt_dot(q, k2[head, :].astype(bf16)) * scale + bias_ref[hl, :, :wb]
        s_tail = _nt_dot(q, kt_ref[0, :, cols].astype(bf16)) * scale + bias_ref[hl, :, wb:]
        m = jnp.maximum(s_main.max(axis=1, keepdims=True), s_tail.max(axis=1, keepdims=True))
        e_main = jnp.exp(s_main - m)
        e_tail = jnp.exp(s_tail - m)
        ssum = e_main.sum(axis=1, keepdims=True) + e_tail.sum(axis=1, keepdims=True)
        o = (jnp.dot(e_main.astype(bf16), v2[head, :].astype(bf16), preferred_element_type=f32)
             + jnp.dot(e_tail.astype(bf16), vt_ref[0, :, cols].astype(bf16), preferred_element_type=f32)) / ssum
        lse = m + jnp.log(ssum)
        lses = [lse[g * t_new:(g + 1) * t_new] for g in range(n_dil)]
        top = functools.reduce(jnp.maximum, lses)
        wts = [jnp.exp(x - top) for x in lses]
        num = sum(w * o[g * t_new:(g + 1) * t_new] for g, w in enumerate(wts))
        o_ref[0, hl] = num / sum(wts) * _silu(gate_ref[0, hl])


def _attend_sample_b(q16, buf_k, buf_v, layer, k_tail, v_tail, bias_tab, gate4, t_new):
    dbatch, n_heads, rows, _ = q16.shape
    wb = buf_k.shape[2]
    hgs = 8 if n_heads % 8 == 0 else n_heads
    grouped = lambda a: a.reshape(a.shape[0], dbatch, wb, n_heads // hgs, hgs, HEAD_DIM)
    bufspec = pl.BlockSpec((None, None, wb, None, hgs, HEAD_DIM), lambda b, hg: (layer, b, 0, hg, 0, 0))
    tailspec = pl.BlockSpec((1, LANE, hgs * HEAD_DIM), lambda b, hg: (b, 0, hg))
    hspec = lambda r: pl.BlockSpec((1, hgs, r, HEAD_DIM), lambda b, hg: (b, hg, 0, 0))
    return pl.pallas_call(
        functools.partial(_attend_sample_b_kernel, n_dil=len(DIL_RATES), t_new=t_new, scale=HEAD_DIM ** -0.5),
        name="attend_sample_b", grid=(dbatch, n_heads // hgs),
        in_specs=[hspec(rows), bufspec, bufspec, tailspec, tailspec,
                  pl.BlockSpec((hgs, rows, wb + LANE), lambda b, hg: (hg, 0, 0)),
                  hspec(t_new)],
        out_specs=hspec(t_new),
        out_shape=jax.ShapeDtypeStruct((dbatch, n_heads, t_new, HEAD_DIM), f32),
        compiler_params=_params(2),
    )(q16, grouped(buf_k), grouped(buf_v), k_tail, v_tail, bias_tab, gate4)


SHIFT_RING = 4
SHIFT_LAG = 2
SHIFT_CHUNK_BYTES = 5 << 20


def _shift_kernel(bk_ref, bv_ref, nk_ref, nv_ref, ok_ref, ov_ref, ring, in_sem, out_sem, new_sem,
                  *, layer, dbatch, rows, shift, chunk):
    reads, writes, tails = [], [], []
    n_chunks = (rows - shift) // chunk
    for a, (src, new, dst) in enumerate(((bk_ref, nk_ref, ok_ref), (bv_ref, nv_ref, ov_ref))):
        for b in range(dbatch):
            for c in range(n_chunks):
                slot = len(reads) % SHIFT_RING
                reads.append(pltpu.make_async_copy(src.at[layer, b, pl.ds(shift + c * chunk, chunk)],
                                                   ring.at[slot], in_sem.at[slot]))
                writes.append(pltpu.make_async_copy(ring.at[slot], dst.at[b, pl.ds(c * chunk, chunk)],
                                                    out_sem.at[slot]))
            tails.append(pltpu.make_async_copy(new.at[b], dst.at[b, pl.ds(rows - shift, shift)], new_sem.at[a, b]))
    for t in tails:
        t.start()
    n = len(reads)
    for i in range(n + SHIFT_LAG):
        if i < n:
            if i >= SHIFT_RING:
                writes[i - SHIFT_RING].wait()
            reads[i].start()
        if 0 <= i - SHIFT_LAG < n:
            reads[i - SHIFT_LAG].wait()
            writes[i - SHIFT_LAG].start()
    for i in range(max(0, n - SHIFT_RING), n):
        writes[i].wait()
    for t in tails:
        t.wait()


def _shift_plan(buf, new):
    _, dbatch, rows, width = buf.shape
    shift = new.shape[1]
    body = (rows - shift) // shift
    per = max(u for u in range(1, body + 1)
              if body % u == 0 and u * shift * width * buf.dtype.itemsize <= SHIFT_CHUNK_BYTES)
    return dict(dbatch=dbatch, rows=rows, shift=shift, chunk=per * shift)


def _shift_scratch(plan, buf):
    return [pltpu.VMEM((SHIFT_RING, plan["chunk"], buf.shape[3]), buf.dtype),
            pltpu.SemaphoreType.DMA((SHIFT_RING,)), pltpu.SemaphoreType.DMA((SHIFT_RING,)),
            pltpu.SemaphoreType.DMA((2, plan["dbatch"]))]


def _shift_ring_step(t, n_steps, bufs, news, outs, ring, in_sem, out_sem, new_sem, *, layer, dbatch, rows, shift, chunk):
    per_b = (rows - shift) // chunk
    per_arr = dbatch * per_b
    n = len(bufs) * per_arr

    def for_chunk(i, read, action):
        slot = i % SHIFT_RING
        b, c = (i % per_arr) // per_b, i % per_b
        for a in range(len(bufs)):
            @pl.when(i // per_arr == a)
            def _(a=a):
                if read:
                    action(pltpu.make_async_copy(bufs[a].at[layer, b, pl.ds(shift + c * chunk, chunk)],
                                                 ring.at[slot], in_sem.at[slot]))
                else:
                    action(pltpu.make_async_copy(ring.at[slot], outs[a].at[b, pl.ds(c * chunk, chunk)],
                                                 out_sem.at[slot]))

    tails = [pltpu.make_async_copy(news[a].at[b], outs[a].at[b, pl.ds(rows - shift, shift)], new_sem.at[a, b])
             for a in range(len(bufs)) for b in range(dbatch)]

    @pl.when(t == 0)
    def _():
        for cp in tails:
            cp.start()

    @pl.when((t >= SHIFT_RING) & (t < n + SHIFT_RING))
    def _():
        for_chunk(t - SHIFT_RING, False, lambda cp: cp.wait())

    @pl.when(t < n)
    def _():
        for_chunk(t, True, lambda cp: cp.start())

    @pl.when((t >= SHIFT_LAG) & (t < n + SHIFT_LAG))
    def _():
        for_chunk(t - SHIFT_LAG, True, lambda cp: cp.wait())
        for_chunk(t - SHIFT_LAG, False, lambda cp: cp.start())

    @pl.when(t == n_steps - 1)
    def _():
        for cp in tails:
            cp.wait()


def _shift_buffers(buf_k, buf_v, layer, new_k, new_v):
    plan = _shift_plan(buf_k, new_k)
    anyspec = pl.BlockSpec(memory_space=pl.ANY)
    out = jax.ShapeDtypeStruct((plan["dbatch"], plan["rows"], buf_k.shape[3]), buf_k.dtype)
    return pl.pallas_call(
        functools.partial(_shift_kernel, layer=layer, **plan), name="shift_buffers",
        in_specs=[anyspec] * 4, out_specs=[anyspec] * 2, out_shape=[out, out],
        scratch_shapes=_shift_scratch(plan, buf_k),
        compiler_params=pltpu.CompilerParams(vmem_limit_bytes=VMEM_LIMIT),
    )(buf_k, buf_v, new_k, new_v)


def _offsets(sizes):
    return [int(x) for x in np.cumsum((0,) + tuple(sizes))[:-1]]


def _pad_rows(a, rows):
    return jnp.pad(a, ((0, 0), (0, rows - a.shape[1])) + ((0, 0),) * (a.ndim - 2))


def _layer_a(xp, xs, dims, j, cache_k, cache_v, cache_ki, page_table, w_in, w_out, ln_g, ln_b, rel_bias):
    batch, seq, dbatch, t_new = dims
    (xp, xpb), (xs, xsb) = xp, xs
    d_model = xp.shape[1]
    n_heads = d_model // HEAD_DIM
    kv, hi = KV_HEADS_A, IDX_HEADS
    group = n_heads // kv
    branch = n_heads * HEAD_DIM
    sizes = (branch, kv * HEAD_DIM, kv * HEAD_DIM, branch, hi * IDX_DIM, IDX_DIM, hi)
    offs = _offsets(sizes)
    tail_w = 2 * LANE
    w_t = w_in.shape[2] % LANE != 0
    w_use = jnp.swapaxes(w_in, 1, 2) if w_t else w_in
    tail_pad = ((0, 0), (0, tail_w - IDX_DIM - hi), (0, 0)) if w_t else ((0, 0), (0, 0), (0, tail_w - IDX_DIM - hi))
    w_tail = jnp.pad(w_use[j:j + 1, offs[5]:, :] if w_t else w_use[j:j + 1, :, offs[5]:], tail_pad)

    mm = lambda slot, dt, nm: _matmul(xpb, xsb, w_use, j, offs[slot], sizes[slot], dt, f"proj_a_{nm}", w_t)
    (q, q_s), (k_p, k_s), (v_p, v_s) = mm(0, bf16, "q"), mm(1, f32, "k"), mm(2, f32, "v")
    (gate, gate_s), (qi, qi_s) = mm(3, f32, "gate"), mm(4, bf16, "qi")
    kw, kw_s = _matmul(xpb, xsb, w_tail, 0, 0, tail_w, f32, "proj_a_kiwi", w_t)
    ki_p, wi = kw[:, :IDX_DIM], kw[:, IDX_DIM:IDX_DIM + hi]
    ki_s, wi_s = kw_s[:, :IDX_DIM], kw_s[:, IDX_DIM:IDX_DIM + hi]

    dist = (np.arange(ATTN_A_TQ)[:, None] + seq - ATTN_A_TQ) - np.arange(seq + ATTN_A_CHUNK - ATTN_A_TQ)[None, :]
    bias_p = _bias_table(rel_bias, _bucket_codes(dist))
    maskadd = _select_prompt(qi, ki_p, wi.T, batch, seq, min(TOPK_MAX, seq // 4))
    og = _attend_prompt(q, k_p, v_p, maskadd, bias_p, gate, batch, seq)

    q, gate, qi, wi = q_s, gate_s, qi_s, wi_s
    n_pages = page_table.shape[1]
    past = n_pages * PAGE_SIZE
    n_keys = past + t_new
    n_lanes = past + PAGE_SIZE
    cscale = (IDX_DIM ** -0.5) * (hi ** -0.5)
    qs = qi.reshape(dbatch, t_new * hi, IDX_DIM)
    wbc = jnp.broadcast_to((wi * cscale).reshape(dbatch, t_new * hi, 1), (dbatch, t_new * hi, LANE))
    ki_tail = _pad_rows(ki_s.reshape(dbatch, t_new, IDX_DIM), PAGE_SIZE)
    scores = _sample_scores(page_table, qs, wbc, cache_ki, j, ki_tail, t_new)
    mask_s = _select_sample(scores.reshape(dbatch * t_new, n_lanes), past, t_new, min(TOPK_MAX, n_keys // 4))

    def to_rows(a):
        a = a.reshape(dbatch, t_new, kv, group, HEAD_DIM)
        return a.transpose(0, 2, 3, 1, 4).reshape(dbatch, kv * group * t_new, HEAD_DIM)

    def to_page(a):
        return _pad_rows(a.reshape(dbatch, t_new, kv, HEAD_DIM), PAGE_SIZE).reshape(dbatch, PAGE_SIZE * kv, HEAD_DIM)

    rows = group * t_new
    dist = (past + np.arange(t_new)[:, None]) - np.arange(n_lanes)[None, :]
    bias_s = _bias_table(rel_bias, _bucket_codes(dist)).reshape(kv * rows, n_lanes)
    mask_g = jnp.broadcast_to(mask_s.reshape(dbatch, 1, t_new, n_lanes), (dbatch, group, t_new, n_lanes))
    mask_g = mask_g.reshape(dbatch, rows, n_lanes)
    pool = cache_k.shape[1]
    paged = lambda c: c.reshape(c.shape[0], pool, PAGE_SIZE * kv, HEAD_DIM)
    og2 = _attend_sample_a(page_table, to_rows(q), bias_s, mask_g, paged(cache_k), paged(cache_v), j,
                           to_page(k_s), to_page(v_s), to_rows(gate))
    og_s = og2.reshape(dbatch, kv, group, t_new, HEAD_DIM).transpose(0, 3, 1, 2, 4)
    og_s = og_s.reshape(dbatch * t_new, branch).astype(bf16)
    y, y_s = _matmul(og, og_s, w_out, j, 0, d_model, f32, "out_a")
    xp_new = _layernorm(y, xp, ln_g[j], ln_b[j])
    xs_new = _layernorm(y_s, xs, ln_g[j], ln_b[j])

    shp = lambda a, b_, t_, *rest: a.reshape(b_, t_, *rest)
    outs = (shp(k_p, batch, seq, kv, HEAD_DIM), shp(v_p, batch, seq, kv, HEAD_DIM), shp(ki_p, batch, seq, IDX_DIM),
            shp(k_s, dbatch, t_new, kv, HEAD_DIM), shp(v_s, dbatch, t_new, kv, HEAD_DIM),
            shp(ki_s, dbatch, t_new, IDX_DIM))
    return xp_new, xs_new, outs


def _layer_b(xp, xs, dims, j, buf_k, buf_v, w_in, w_out, ln_g, ln_b, rel_bias):
    batch, seq, dbatch, t_new = dims
    (xp, xpb), (xs, xsb) = xp, xs
    d_model = xp.shape[1]
    n_heads = d_model // HEAD_DIM
    branch = n_heads * HEAD_DIM
    n_dil = len(DIL_RATES)
    sizes = (n_dil * branch, branch, branch, branch)
    offs = _offsets(sizes)

    mm = lambda slot, dt, nm: _matmul(xpb, xsb, w_in, j, offs[slot], sizes[slot], dt, f"proj_b_{nm}")
    (q, q_s), (gate, gate_s) = mm(0, bf16, "q"), mm(3, f32, "gate")
    k_p, k_s, k_heads = _matmul_heads(xpb, xsb, w_in, j, offs[1], n_heads, "proj_b_k")
    v_p, v_s, v_heads = _matmul_heads(xpb, xsb, w_in, j, offs[2], n_heads, "proj_b_v")

    m =(np.arange(BLK)[:, None] + BLK) - np.arange(2 * BLK)[None, :]
    codes = [_bucket_codes(d * np.clip(m, 0, w // d), (m >= 0) & (m <= w // d))
             for d, w in zip(DIL_RATES, DIL_WINDOWS)]
    bias_p = _bias_table(rel_bias, np.concatenate(codes, axis=1))
    wb = buf_k.shape[2]
    flat = lambda buf: buf.reshape(buf.shape[0], dbatch, wb * n_heads, HEAD_DIM)
    og, new_k, new_v = _attend_prompt_b(q, k_p, v_p, gate, bias_p, batch, seq,
                                        (flat(buf_k), flat(buf_v), j, k_s.reshape(dbatch, t_new * n_heads, HEAD_DIM),
                                         v_s.reshape(dbatch, t_new * n_heads, HEAD_DIM)))

    q, gate = q_s, gate_s
    rows = -(-(n_dil * t_new) // 16) * 16
    pos = np.arange(wb + LANE)[None, :]
    codes = []
    for d, w in zip(DIL_RATES, DIL_WINDOWS):
        dist = wb + np.arange(t_new)[:, None] - pos
        codes.append(_bucket_codes(dist, (dist >= 0) & (dist % d == 0) & (dist // d <= w // d) & (pos < wb + t_new)))
    codes.append(np.full((rows - n_dil * t_new, wb + LANE), BUCKET_ZERO, np.int32))
    bias_s = _bias_table(rel_bias, np.concatenate(codes, axis=0))
    q16 = q.reshape(dbatch, t_new, n_dil, n_heads, HEAD_DIM).transpose(0, 3, 2, 1, 4)
    q16 = q16.reshape(dbatch, n_heads, n_dil * t_new, HEAD_DIM)
    q16 = jnp.pad(q16, ((0, 0), (0, 0), (0, rows - n_dil * t_new), (0, 0)))
    gate4 = gate.reshape(dbatch, t_new, n_heads, HEAD_DIM).transpose(0, 2, 1, 3)
    k_tail = _pad_rows(k_s.reshape(dbatch, t_new, branch), LANE)
    v_tail = _pad_rows(v_s.reshape(dbatch, t_new, branch), LANE)
    og4 = _attend_sample_b(q16, buf_k, buf_v, j, k_tail, v_tail, bias_s, gate4, t_new)
    og_s = og4.transpose(0, 2, 1, 3).reshape(dbatch * t_new, branch).astype(bf16)
    y, y_s = _matmul(og, og_s, w_out, j, 0, d_model, f32, "out_b")
    xp_new = _layernorm(y, xp, ln_g[j], ln_b[j])
    xs_new = _layernorm(y_s, xs, ln_g[j], ln_b[j])

    wp = min(W_MAX, seq)
    k_win = k_heads.reshape(batch, seq, n_heads, HEAD_DIM)[:, seq - wp:]
    v_win = v_heads.reshape(batch, seq, n_heads, HEAD_DIM)[:, seq - wp:]
    return xp_new, xs_new, (k_win, v_win, new_k.reshape(dbatch, wb, n_heads, HEAD_DIM),
                            new_v.reshape(dbatch, wb, n_heads, HEAD_DIM))


def kernel(x_prompt, x_sample, cache_k_a, cache_v_a, cache_kidx_a, cache_k_b, cache_v_b, page_table, rel_bias,
           w_in_a, w_out_a, ln_g_a, ln_b_a, w_in_b, w_out_b, ln_g_b, ln_b_b):
    batch, seq, d_model = x_prompt.shape
    dbatch, t_new, _ = x_sample.shape
    dims = (batch, seq, dbatch, t_new)
    with_bf16 = lambda a: (a, a.astype(bf16))
    xp = with_bf16(x_prompt.reshape(batch * seq, d_model))
    xs = with_bf16(x_sample.reshape(dbatch * t_new, d_model))
    outs_a, outs_b = [], []
    for layer in range(DEPTH):
        j = layer // 2
        if layer % 2 == 0:
            xp, xs, o = _layer_a(xp, xs, dims, j, cache_k_a, cache_v_a, cache_kidx_a, page_table,
                                 w_in_a, w_out_a, ln_g_a, ln_b_a, rel_bias)
            outs_a.append(o)
        else:
            xp, xs, o = _layer_b(xp, xs, dims, j, cache_k_b, cache_v_b, w_in_b, w_out_b, ln_g_b, ln_b_b, rel_bias)
            outs_b.append(o)
    stack = lambda group, idx: group[0][idx][None] if len(group) == 1 else jnp.stack([o[idx] for o in group])
    return (xp[0].reshape(batch, seq, d_model), xs[0].reshape(dbatch, t_new, d_model),
            *(stack(outs_a, n) for n in range(6)), *(stack(outs_b, n) for n in range(4)))
```

```python
import functools
import math

import jax
import jax.numpy as jnp
import numpy as np
from jax import lax
from jax.experimental import pallas as pl
from jax.experimental.pallas import tpu as pltpu

HEAD_DIM = 128
KV_HEADS_A = 8
IDX_HEADS = 64
IDX_DIM = 128
TOPK_MAX = 256
DIL_WINDOWS = (128, 512, 2048)
DIL_RATES = (1, 4, 16)
W_MAX = max(DIL_WINDOWS)
BLK = 128
N_BUCKETS = 32
REL_MAX_DIST = 2048
DEPTH = 2
ALPHA = (2 * DEPTH) ** 0.25
LN_EPS = 1e-5
PAGE_SIZE = 128

LANE = 128
NEG_INF = float("-inf")
INT_MIN = -(2 ** 31)
M_INIT = -1e30
BUCKET_ZERO = -1
BUCKET_MASKED = -2
VMEM_LIMIT = 56 * 1024 * 1024

f32 = jnp.float32
bf16 = jnp.bfloat16
i32 = jnp.int32


def _params(n_axes, vmem=VMEM_LIMIT):
    return pltpu.CompilerParams(dimension_semantics=("arbitrary",) * n_axes, vmem_limit_bytes=vmem)


def _nt_dot(a, b):
    return lax.dot_general(a, b, (((1,), (1,)), ((), ())), preferred_element_type=f32)


def _silu(x):
    return x * (1.0 / (1.0 + jnp.exp(-x)))


def _softmax_step(s, m_old, l_old):
    m_new = jnp.maximum(m_old, s.max(axis=1, keepdims=True))
    alpha = jnp.exp(m_old - m_new)
    p = jnp.exp(s - m_new)
    return m_new, alpha * l_old + p.sum(axis=1, keepdims=True), alpha, p


def _mm_kernel(x_ref, xs_ref, w_ref, *rest, chunk, transposed, residual):
    if residual:
        r_ref, rs_ref, o_ref, os_ref, wbf_ref = rest
    else:
        o_ref, os_ref, wbf_ref = rest
    mul = _nt_dot if transposed else functools.partial(jnp.dot, preferred_element_type=f32)

    @pl.when(pl.program_id(1) == 0)
    def _():
        def body(r, c):
            rows = pl.ds(pl.multiple_of(r * chunk, chunk), chunk)
            wbf_ref[rows, :] = w_ref[rows, :].astype(bf16)
            return c
        lax.fori_loop(0, w_ref.shape[0] // chunk, body, 0)
        ys = mul(xs_ref[...], wbf_ref[...])
        os_ref[...] = (ALPHA * rs_ref[...] + ys if residual else ys).astype(os_ref.dtype)

    y = mul(x_ref[...], wbf_ref[...])
    o_ref[...] = (ALPHA * r_ref[...] + y if residual else y).astype(o_ref.dtype)


def _pick_tile(n, candidates):
    for c in candidates:
        if n % c == 0:
            return c
    raise ValueError(f"no tile for {n}")


def _matmul(x_bf, xs_bf, w, layer, col_off, ncols, out_dtype, name, transposed=False, residual=None):
    m, k = x_bf.shape
    ms = xs_bf.shape[0]
    tn = _pick_tile(math.gcd(col_off, ncols) if col_off else ncols, (512, 256, 128))
    tm = _pick_tile(m, (1024, 512, 256, 128))
    off_blocks = col_off // tn
    if transposed:
        wblock, wspec, chunk = (tn, k), pl.BlockSpec((None, tn, k), lambda n, i: (layer, n + off_blocks, 0)), LANE
    else:
        wblock, wspec = (k, tn), pl.BlockSpec((None, k, tn), lambda n, i: (layer, 0, n + off_blocks))
        chunk = _pick_tile(k, (256, 128))
    ospecs = [pl.BlockSpec((tm, tn), lambda n, i: (i, n)), pl.BlockSpec((ms, tn), lambda n, i: (0, n))]
    return pl.pallas_call(
        functools.partial(_mm_kernel, chunk=chunk, transposed=transposed, residual=residual is not None), name=name,
        grid=(ncols // tn, m // tm),
        in_specs=[pl.BlockSpec((tm, k), lambda n, i: (i, 0)),
                  pl.BlockSpec((ms, k), lambda n, i: (0, 0)),
                  wspec] + (ospecs if residual is not None else []),
        out_specs=ospecs,
        out_shape=[jax.ShapeDtypeStruct((m, ncols), out_dtype), jax.ShapeDtypeStruct((ms, ncols), out_dtype)],
        scratch_shapes=[pltpu.VMEM(wblock, bf16)],
        compiler_params=_params(2),
    )(x_bf, xs_bf, w, *(residual or ()))


def _mm_heads_kernel(x_ref, xs_ref, w_ref, o_ref, os_ref, oh_ref, wbf_ref, *, k_chunk):
    @pl.when(pl.program_id(1) == 0)
    def _():
        def body(r, c):
            rows = pl.ds(pl.multiple_of(r * k_chunk, k_chunk), k_chunk)
            wbf_ref[rows, :] = w_ref[rows, :].astype(bf16)
            return c
        lax.fori_loop(0, w_ref.shape[0] // k_chunk, body, 0)
        os_ref[...] = jnp.dot(xs_ref[...], wbf_ref[...], preferred_element_type=f32)

    res = jnp.dot(x_ref[...], wbf_ref[...], preferred_element_type=f32)
    o_ref[...] = res
    tm, nh, dh = oh_ref.shape
    flat = oh_ref.reshape(tm * nh, dh)
    for hh in range(nh):
        flat[pl.ds(hh, tm, stride=nh), :] = res[:, hh * dh:(hh + 1) * dh]


def _matmul_heads(x_bf, xs_bf, w, layer, col_off, n_heads, name):
    m, k = x_bf.shape
    ms = xs_bf.shape[0]
    nh = 8 if n_heads % 8 == 0 else n_heads
    tn = nh * HEAD_DIM
    ncols = n_heads * HEAD_DIM
    tm = _pick_tile(m, (512, 256, 128))
    off_blocks = col_off // tn
    k_chunk = _pick_tile(k, (256, 128))
    return pl.pallas_call(
        functools.partial(_mm_heads_kernel, k_chunk=k_chunk), name=name,
        grid=(ncols // tn, m // tm),
        in_specs=[pl.BlockSpec((tm, k), lambda n, i: (i, 0)),
                  pl.BlockSpec((ms, k), lambda n, i: (0, 0)),
                  pl.BlockSpec((None, k, tn), lambda n, i: (layer, 0, n + off_blocks),
                               pipeline_mode=pl.Buffered(1))],
        out_specs=[pl.BlockSpec((tm, tn), lambda n, i: (i, n)),
                   pl.BlockSpec((ms, tn), lambda n, i: (0, n)),
                   pl.BlockSpec((tm, nh, HEAD_DIM), lambda n, i: (i, n, 0))],
        out_shape=[jax.ShapeDtypeStruct((m, ncols), f32), jax.ShapeDtypeStruct((ms, ncols), f32),
                   jax.ShapeDtypeStruct((m, n_heads, HEAD_DIM), f32)],
        scratch_shapes=[pltpu.VMEM((k, tn), bf16)],
        compiler_params=_params(2),
    )(x_bf, xs_bf, w)


def _ln_kernel(z_ref, g_ref, b_ref, o_ref, *obf_ref):
    z = z_ref[...]
    mu = jnp.mean(z, axis=-1, keepdims=True)
    zc = z - mu
    var = jnp.mean(zc * zc, axis=-1, keepdims=True)
    out = zc * lax.rsqrt(var + LN_EPS) * g_ref[...] + b_ref[...]
    o_ref[...] = out
    for ref in obf_ref:
        ref[...] = out.astype(bf16)


def _layernorm(z, g, b, with_bf16):
    m, d = z.shape
    tm = _pick_tile(m, (128, 64, 32, 16, 8))
    row = pl.BlockSpec((tm, d), lambda i: (i, 0))
    vec = pl.BlockSpec((1, d), lambda i: (0, 0))
    n_out = 2 if with_bf16 else 1
    res = pl.pallas_call(
        _ln_kernel, grid=(m // tm,), name="layernorm",
        in_specs=[row, vec, vec], out_specs=[row] * n_out,
        out_shape=[jax.ShapeDtypeStruct((m, d), f32), jax.ShapeDtypeStruct((m, d), bf16)][:n_out],
        compiler_params=_params(1),
    )(z, g.reshape(1, d), b.reshape(1, d))
    return (res[0], res[1] if with_bf16 else None)


def _rel_bucket_np(n):
    max_exact = N_BUCKETS // 2
    nf = np.maximum(n, 1).astype(np.float64)
    large = max_exact + (np.log(nf / max_exact) / math.log(REL_MAX_DIST / max_exact)
                         * (N_BUCKETS - max_exact)).astype(np.int64)
    large = np.minimum(large, N_BUCKETS - 1)
    return np.where(n < max_exact, n, large).astype(np.int32)


def _bucket_codes(dist, valid=None):
    code = np.where(dist >= 0, _rel_bucket_np(np.maximum(dist, 0)), BUCKET_ZERO)
    if valid is not None:
        code = np.where(valid, code, BUCKET_MASKED)
    return code.astype(np.int32)


def _bias_kernel(tbl_ref, bm_ref, o_ref, *, cw):
    h = pl.program_id(0)
    row = tbl_ref[pl.ds(h, 1), :]

    def body(c, carry):
        for t in range(cw // LANE):
            cols = pl.ds(pl.multiple_of(c * cw + t * LANE, LANE), LANE)
            code = bm_ref[:, cols]
            val = jnp.take_along_axis(jnp.broadcast_to(row, code.shape), jnp.maximum(code, 0), axis=1)
            o_ref[0, :, cols] = jnp.where(code >= 0, val, jnp.where(code == BUCKET_MASKED, NEG_INF, 0.0))
        return carry

    lax.fori_loop(0, bm_ref.shape[1] // cw, body, 0, unroll=8)


def _bias_table(rel_bias, codes):
    r, c = codes.shape
    n_heads = rel_bias.shape[1]
    units = c // LANE
    per = max(1, 16 // -(-r // 8))
    cw = LANE * max(u for u in range(1, units + 1) if units % u == 0 and u <= per)
    return pl.pallas_call(
        functools.partial(_bias_kernel, cw=cw), name="bias_table",
        grid=(n_heads,),
        in_specs=[pl.BlockSpec((n_heads, LANE), lambda h: (0, 0)),
                  pl.BlockSpec((r, c), lambda h: (0, 0))],
        out_specs=pl.BlockSpec((1, r, c), lambda h: (h, 0, 0)),
        out_shape=jax.ShapeDtypeStruct((n_heads, r, c), f32),
        compiler_params=_params(1),
    )(jnp.pad(rel_bias.T, ((0, 0), (0, LANE - N_BUCKETS))), jnp.asarray(codes))


def _sortable_key(x):
    bits = pltpu.bitcast(x, i32)
    return bits ^ ((bits >> 31) & 0x7FFFFFFF)


def _kth_largest_key(count_ge, shape, k_sel):
    def body(it, t):
        cand = t + lax.shift_left(jnp.int32(1), 31 - it)
        return jnp.where(count_ge(cand) >= k_sel, cand, t)
    return lax.fori_loop(0, 32, body, jnp.full(shape, INT_MIN, i32))


def _select_prompt_kernel(qi_ref, ki_ref, wt_ref, m_ref, kibf_ref, key_ref, *, n_idx_heads, k_sel, ck, hb, cscale):
    i = pl.program_id(1)
    tq = qi_ref.shape[0]

    @pl.when(i == 0)
    def _():
        kibf_ref[...] = ki_ref[...].astype(bf16)

    nck = ((i + 1) * tq + ck - 1) // ck
    t_row = i * tq + lax.broadcasted_iota(i32, (1, tq), 1)

    def chunk_body(c, carry):
        koff = pl.multiple_of(c * ck, ck)
        kc = kibf_ref[pl.ds(koff, ck), :]

        def heads_body(hg, acc):
            hoff = hg * (hb * IDX_DIM)
            qstack = jnp.concatenate(
                [qi_ref[:, pl.ds(pl.multiple_of(hoff + j * IDX_DIM, IDX_DIM), IDX_DIM)] for j in range(hb)], axis=0)
            s = _nt_dot(kc, qstack)
            w = wt_ref[pl.ds(pl.multiple_of(hg * hb, hb), hb), :] * cscale
            for j in range(hb):
                acc = acc + jnp.maximum(s[:, j * tq:(j + 1) * tq], 0.0) * w[j:j + 1, :]
            return acc

        acc = lax.fori_loop(0, n_idx_heads // hb, heads_body, jnp.zeros((ck, tq), f32), unroll=2)
        l_col = koff + lax.broadcasted_iota(i32, (ck, 1), 0)
        key_ref[pl.ds(koff, ck), :] = jnp.where(l_col <= t_row, _sortable_key(acc), INT_MIN)
        return carry

    lax.fori_loop(0, nck, chunk_body, 0)

    def count_ge(cand):
        def body(c, cnt):
            kc = key_ref[pl.ds(pl.multiple_of(c * ck, ck), ck), :]
            hit = jnp.where(kc >= cand, 1, 0).astype(i32)
            return cnt + hit.reshape(ck // 8, 8, tq).sum(axis=0)
        cnt = lax.fori_loop(0, nck, body, jnp.zeros((8, tq), i32))
        return cnt.sum(axis=0, keepdims=True)

    thr = _kth_largest_key(count_ge, (1, tq), k_sel)
    thr = jnp.maximum(thr, INT_MIN + 1)
    n_ge = count_ge(thr)
    ties = jnp.max(n_ge) > k_sel

    m_ref[...] = jnp.full(m_ref.shape, NEG_INF, m_ref.dtype)

    def write_chunk(koff, sel):
        m_ref[:, pl.ds(koff, ck)] = jnp.where(sel, 0.0, NEG_INF).astype(f32).T.astype(m_ref.dtype)

    @pl.when(jnp.logical_not(ties))
    def _():
        def out_body(c, carry):
            koff = pl.multiple_of(c * ck, ck)
            write_chunk(koff, key_ref[pl.ds(koff, ck), :] >= thr)
            return carry
        lax.fori_loop(0, nck, out_body, 0)

    @pl.when(ties)
    def _():
        quota = (k_sel - count_ge(thr + 1)).astype(f32)
        tri = (lax.broadcasted_iota(i32, (ck, ck), 0) >= lax.broadcasted_iota(i32, (ck, ck), 1)).astype(bf16)

        def out_body(c, seen):
            koff = pl.multiple_of(c * ck, ck)
            key = key_ref[pl.ds(koff, ck), :]
            eq = key == thr
            rank = jnp.dot(tri, jnp.where(eq, 1.0, 0.0).astype(bf16), preferred_element_type=f32) + seen
            write_chunk(koff, jnp.where(eq, jnp.where(rank <= quota, thr, thr - 1), key) >= thr)
            return rank[ck - 1:ck, :]
        lax.fori_loop(0, nck, out_body, jnp.zeros((1, tq), f32))


def _select_prompt(qi, ki, wi_t, batch, seq, k_sel):
    n_idx_heads = wi_t.shape[0]
    tq, ck = BLK, 256
    hb = _pick_tile(n_idx_heads, (16, 8, 4, 2, 1))
    nblk = seq // tq
    cscale = (IDX_DIM ** -0.5) * (n_idx_heads ** -0.5)
    return pl.pallas_call(
        functools.partial(_select_prompt_kernel, n_idx_heads=n_idx_heads, k_sel=k_sel, ck=ck, hb=hb, cscale=cscale),
        name="select_prompt", grid=(batch, nblk),
        in_specs=[pl.BlockSpec((tq, n_idx_heads * IDX_DIM), lambda b, i: (b * nblk + i, 0)),
                  pl.BlockSpec((seq, IDX_DIM), lambda b, i: (b, 0)),
                  pl.BlockSpec((n_idx_heads, tq), lambda b, i: (0, b * nblk + i))],
        out_specs=pl.BlockSpec((tq, seq), lambda b, i: (b * nblk + i, 0)),
        out_shape=jax.ShapeDtypeStruct((batch * seq, seq), bf16),
        scratch_shapes=[pltpu.VMEM((seq, IDX_DIM), bf16), pltpu.VMEM((seq, tq), i32)],
        compiler_params=_params(2),
    )(qi, ki, wi_t)


def _attend_prompt_kernel(q_ref, k_ref, v_ref, mk_ref, g_ref, gate_ref, o_ref,
                          kbf_ref, vbf_ref, s_ref, mp_ref, l_ref, acc_ref, *, group, ck, sub, scale):
    i = pl.program_id(2)
    nblk = pl.num_programs(2)
    tq = q_ref.shape[0]
    rows = group * tq

    @pl.when(i == 0)
    def _():
        kbf_ref[...] = k_ref[...].astype(bf16)
        vbf_ref[...] = v_ref[...].astype(bf16)

    nck = ((i + 1) * tq + ck - 1) // ck
    goff0 = (nblk - 1 - i) * tq
    qs = jnp.concatenate([q_ref[:, g * HEAD_DIM:(g + 1) * HEAD_DIM] for g in range(group)], axis=0)

    mp_ref[...] = jnp.full(mp_ref.shape, M_INIT, f32)

    def logits_body(c, carry):
        for u in range(ck // sub):
            koff = pl.multiple_of(c * ck + u * sub, sub)
            goff = pl.multiple_of(goff0 + koff, LANE)
            s = _nt_dot(qs, kbf_ref[pl.ds(koff, sub), :]) * scale
            bias = jnp.concatenate([g_ref[g, :, pl.ds(goff, sub)] for g in range(group)], axis=0)
            sel = mk_ref[:, pl.ds(koff, sub)].astype(f32)
            s = s + bias + jnp.concatenate([sel] * group, axis=0)
            s_ref[:, pl.ds(koff, sub)] = s
            mp_ref[...] = functools.reduce(jnp.maximum, [mp_ref[...]] + [s[:, t * LANE:(t + 1) * LANE]
                                                                         for t in range(sub // LANE)])
        return carry

    lax.fori_loop(0, nck, logits_body, 0)
    m_all = jnp.broadcast_to(mp_ref[...].max(axis=1, keepdims=True), (rows, LANE))

    l_ref[...] = jnp.zeros(l_ref.shape, f32)
    acc_ref[...] = jnp.zeros(acc_ref.shape, f32)

    def pv_body(c, carry):
        for u in range(ck // sub):
            koff = pl.multiple_of(c * ck + u * sub, sub)
            ps = [jnp.exp(s_ref[:, pl.ds(pl.multiple_of(koff + t * LANE, LANE), LANE)] - m_all)
                  for t in range(sub // LANE)]
            l_ref[...] += functools.reduce(jnp.add, ps)
            p = jnp.concatenate(ps, axis=1).astype(bf16)
            acc_ref[...] += jnp.dot(p, vbf_ref[pl.ds(koff, sub), :], preferred_element_type=f32)
        return carry

    lax.fori_loop(0, nck, pv_body, 0)
    o = acc_ref[...] / l_ref[...].sum(axis=1, keepdims=True)
    for g in range(group):
        cols = slice(g * HEAD_DIM, (g + 1) * HEAD_DIM)
        o_ref[:, cols] = (o[g * tq:(g + 1) * tq] * _silu(gate_ref[:, cols])).astype(o_ref.dtype)


ATTN_A_CHUNK = 512
ATTN_A_TQ = 256


def _attend_prompt(q, k, v, maskadd, bias_tab, gate, batch, seq):
    kv = k.shape[1] // HEAD_DIM
    group = q.shape[1] // (kv * HEAD_DIM)
    tq, ck = ATTN_A_TQ, ATTN_A_CHUNK
    nblk = seq // tq
    gw = group * HEAD_DIM
    qspec = pl.BlockSpec((tq, gw), lambda kh, b, i: (b * nblk + i, kh))
    kvspec = pl.BlockSpec((seq, HEAD_DIM), lambda kh, b, i: (b, kh))
    return pl.pallas_call(
        functools.partial(_attend_prompt_kernel, group=group, ck=ck, sub=256, scale=HEAD_DIM ** -0.5),
        name="attend_prompt_a", grid=(kv, batch, nblk),
        in_specs=[qspec, kvspec, kvspec,
                  pl.BlockSpec((tq, seq), lambda kh, b, i: (b * nblk + i, 0)),
                  pl.BlockSpec((group, tq, bias_tab.shape[2]), lambda kh, b, i: (kh, 0, 0)),
                  qspec],
        out_specs=qspec,
        out_shape=jax.ShapeDtypeStruct(q.shape, bf16),
        scratch_shapes=[pltpu.VMEM((seq, HEAD_DIM), bf16), pltpu.VMEM((seq, HEAD_DIM), bf16),
                        pltpu.VMEM((group * tq, seq), f32), pltpu.VMEM((group * tq, LANE), f32),
                        pltpu.VMEM((group * tq, LANE), f32), pltpu.VMEM((group * tq, HEAD_DIM), f32)],
        compiler_params=_params(3),
    )(q, k, v, maskadd, bias_tab, gate)


def _score_page(qs, wb, kpage, t_new):
    s = _nt_dot(qs, kpage.astype(bf16))
    x = jnp.maximum(s, 0.0) * wb
    return x.reshape(t_new, x.shape[0] // t_new, x.shape[1]).sum(axis=1)


def _sample_scores_kernel(pt_ref, qs_ref, wb_ref, *refs, pg, t_new):
    o_ref = refs[pg]
    for r in range(pg):
        o_ref[0, :, r * PAGE_SIZE:(r + 1) * PAGE_SIZE] = _score_page(qs_ref[0], wb_ref[0], refs[r][...], t_new)


def _sample_scores_tail_kernel(qs_ref, wb_ref, kt_ref, o_ref, *, t_new):
    o_ref[0] = _score_page(qs_ref[0], wb_ref[0], kt_ref[0], t_new)


def _sample_scores(page_table, qs, wb, cache_ki, layer, ki_tail, t_new):
    dbatch, rows, _ = qs.shape
    n_pages = page_table.shape[1]
    pg = _pick_tile(n_pages, (16, 8, 4, 2, 1))
    qspec = pl.BlockSpec((1, rows, IDX_DIM), lambda b, s, pt: (b, 0, 0))
    page_specs = [pl.BlockSpec((None, None, PAGE_SIZE, IDX_DIM),
                               lambda b, s, pt, r=r: (layer, pt[b, s * pg + r], 0, 0)) for r in range(pg)]
    main = pl.pallas_call(
        functools.partial(_sample_scores_kernel, pg=pg, t_new=t_new), name="sample_scores",
        grid_spec=pltpu.PrefetchScalarGridSpec(
            num_scalar_prefetch=1, grid=(dbatch, n_pages // pg),
            in_specs=[qspec, qspec] + page_specs,
            out_specs=pl.BlockSpec((1, t_new, pg * PAGE_SIZE), lambda b, s, pt: (b, 0, s))),
        out_shape=jax.ShapeDtypeStruct((dbatch, t_new, n_pages * PAGE_SIZE), f32),
        compiler_params=_params(2),
    )(page_table, qs, wb, *([cache_ki] * pg))
    spec3 = lambda d1, d2: pl.BlockSpec((1, d1, d2), lambda b: (b, 0, 0))
    tail = pl.pallas_call(
        functools.partial(_sample_scores_tail_kernel, t_new=t_new), name="sample_scores_tail",
        grid=(dbatch,),
        in_specs=[spec3(rows, IDX_DIM), spec3(rows, IDX_DIM), spec3(PAGE_SIZE, IDX_DIM)],
        out_specs=spec3(t_new, PAGE_SIZE),
        out_shape=jax.ShapeDtypeStruct((dbatch, t_new, PAGE_SIZE), f32),
        compiler_params=_params(1),
    )(qs, wb, ki_tail)
    return jnp.concatenate([main, tail], axis=2)


def _select_sample_kernel(s_ref, o_ref, key_ref, *, past, t_new, k_sel):
    rows, n = s_ref.shape
    t = past + lax.broadcasted_iota(i32, (rows, 1), 0) % t_new
    l = lax.broadcasted_iota(i32, (1, n), 1)
    key_ref[...] = jnp.where(l <= t, _sortable_key(s_ref[...]), INT_MIN)

    def count_ge(cand):
        return jnp.where(key_ref[...] >= cand, 1, 0).astype(i32).sum(axis=1, keepdims=True)

    thr = jnp.maximum(_kth_largest_key(count_ge, (rows, 1), k_sel), INT_MIN + 1)
    quota = (k_sel - count_ge(thr + 1)).astype(f32)
    tri = (lax.broadcasted_iota(i32, (LANE, LANE), 0) <= lax.broadcasted_iota(i32, (LANE, LANE), 1)).astype(bf16)

    def tile_body(c, seen):
        lanes = pl.ds(pl.multiple_of(c * LANE, LANE), LANE)
        key = key_ref[:, lanes]
        eq = key == thr
        rank = jnp.dot(jnp.where(eq, 1.0, 0.0).astype(bf16), tri, preferred_element_type=f32) + seen
        kept = jnp.where(eq, jnp.where(rank <= quota, thr, thr - 1), key) >= thr
        o_ref[:, lanes] = jnp.where(kept, 0.0, NEG_INF).astype(f32)
        return rank.max(axis=1, keepdims=True)

    lax.fori_loop(0, n // LANE, tile_body, jnp.zeros((rows, 1), f32), unroll=3)


def _select_sample(scores, past, t_new, k_sel):
    return pl.pallas_call(
        functools.partial(_select_sample_kernel, past=past, t_new=t_new, k_sel=k_sel), name="select_sample",
        out_shape=jax.ShapeDtypeStruct(scores.shape, f32),
        scratch_shapes=[pltpu.VMEM(scores.shape, i32)],
        compiler_params=pltpu.CompilerParams(vmem_limit_bytes=VMEM_LIMIT),
    )(scores)


def _attend_sample_a_kernel(pt_ref, q_ref, bias_ref, mask_ref, biast_ref, maskt_ref, gate_ref, *refs, pg, kv, scale):
    kpages, vpages = refs[:pg], refs[pg:2 * pg]
    kt_ref, vt_ref, o_ref, m_ref, l_ref, acc_ref = refs[2 * pg:]
    s_id = pl.program_id(1)
    rows = q_ref.shape[1] // kv

    @pl.when(s_id == 0)
    def _():
        m_ref[...] = jnp.full(m_ref.shape, M_INIT, f32)
        l_ref[...] = jnp.zeros(l_ref.shape, f32)
        acc_ref[...] = jnp.zeros(acc_ref.shape, f32)

    def head_rows(page_ref, kh):
        return page_ref[pl.ds(kh, PAGE_SIZE, stride=kv), :].astype(bf16)

    def update(k_refs, v_refs, bias, mask):
        s = jnp.concatenate(
            [jnp.concatenate([_nt_dot(q_ref[0, kh * rows:(kh + 1) * rows, :], head_rows(kr, kh)) for kr in k_refs],
                             axis=1) for kh in range(kv)], axis=0)
        s = s * scale + bias + jnp.concatenate([mask] * kv, axis=0)
        m_new, l_new, alpha, p = _softmax_step(s, m_ref[...], l_ref[...])
        p = p.astype(bf16)
        pv = []
        for kh in range(kv):
            pk = p[kh * rows:(kh + 1) * rows]
            pv.append(sum(jnp.dot(pk[:, r * PAGE_SIZE:(r + 1) * PAGE_SIZE], head_rows(vr, kh),
                                  preferred_element_type=f32) for r, vr in enumerate(v_refs)))
        acc_ref[...] = alpha * acc_ref[...] + jnp.concatenate(pv, axis=0)
        m_ref[...] = m_new
        l_ref[...] = l_new

    update(kpages, vpages, bias_ref[...], mask_ref[0])

    @pl.when(s_id == pl.num_programs(1) - 1)
    def _():
        update([kt_ref.at[0]], [vt_ref.at[0]], biast_ref[...], maskt_ref[0])
        o_ref[0] = acc_ref[...] / l_ref[...] * _silu(gate_ref[0])


def _attend_sample_a(page_table, q2, bias_tab, mask, cache_k, cache_v, layer, k_tail, v_tail, gate2):
    dbatch, qrows, _ = q2.shape
    kv = cache_k.shape[2] // PAGE_SIZE
    rows = qrows // kv
    n_pages = page_table.shape[1]
    past = n_pages * PAGE_SIZE
    pg = _pick_tile(n_pages, (4, 2, 1))
    qspec = pl.BlockSpec((1, qrows, HEAD_DIM), lambda b, s, pt: (b, 0, 0))
    kspecs = [pl.BlockSpec((None, None, PAGE_SIZE * kv, HEAD_DIM),
                           lambda b, s, pt, r=r: (layer, pt[b, s * pg + r], 0, 0)) for r in range(pg)]
    tailspec = pl.BlockSpec((1, PAGE_SIZE * kv, HEAD_DIM), lambda b, s, pt: (b, 0, 0))
    return pl.pallas_call(
        functools.partial(_attend_sample_a_kernel, pg=pg, kv=kv, scale=HEAD_DIM ** -0.5), name="attend_sample_a",
        grid_spec=pltpu.PrefetchScalarGridSpec(
            num_scalar_prefetch=1, grid=(dbatch, n_pages // pg),
            in_specs=[qspec,
                      pl.BlockSpec((qrows, pg * PAGE_SIZE), lambda b, s, pt: (0, s)),
                      pl.BlockSpec((1, rows, pg * PAGE_SIZE), lambda b, s, pt: (b, 0, s)),
                      pl.BlockSpec((qrows, PAGE_SIZE), lambda b, s, pt: (0, past // PAGE_SIZE)),
                      pl.BlockSpec((1, rows, PAGE_SIZE), lambda b, s, pt: (b, 0, past // PAGE_SIZE)),
                      qspec] + kspecs + kspecs + [tailspec, tailspec],
            out_specs=qspec,
            scratch_shapes=[pltpu.VMEM((qrows, 1), f32), pltpu.VMEM((qrows, 1), f32),
                            pltpu.VMEM((qrows, HEAD_DIM), f32)]),
        out_shape=jax.ShapeDtypeStruct(q2.shape, f32),
        compiler_params=_params(2),
    )(page_table, q2, bias_tab, mask, bias_tab, mask, gate2,
      *([cache_k] * pg), *([cache_v] * pg), k_tail, v_tail)


def _attend_prompt_b_kernel(q0_ref, q1_ref, q2_ref, k_ref, v_ref, gate_ref, gb_ref, *rest, rates, scale, unroll, shift):
    if shift is None:
        o_ref, qf_ref, og_ref, lse_ref = rest
    else:
        bk_ref, bv_ref, nk_ref, nv_ref, o_ref, ok_ref, ov_ref, qf_ref, og_ref, lse_ref, *ring_refs = rest
        step = pl.program_id(0) * pl.num_programs(1) + pl.program_id(1)
        _shift_ring_step(step, pl.num_programs(0) * pl.num_programs(1), (bk_ref, bv_ref), (nk_ref, nv_ref),
                         (ok_ref, ov_ref), *ring_refs, **shift)
    seq = k_ref.shape[0]
    col = lax.broadcasted_iota(i32, (BLK, 2 * BLK), 1)

    for g, (q_ref, d) in enumerate(zip((q0_ref, q1_ref, q2_ref), rates)):
        qf_ref[...] = q_ref[...].astype(f32)
        nb = seq // d // BLK
        bias = gb_ref[0, :, g * 2 * BLK:(g + 1) * 2 * BLK]

        def blk_body(it, carry, g=g, d=d, nb=nb, bias=bias):
            r = it // nb
            blk = it % nb
            start = r + blk * (BLK * d)
            start_prev = r + jnp.maximum(blk - 1, 0) * (BLK * d)
            cur = pl.ds(start, BLK, stride=d)
            prev = pl.ds(start_prev, BLK, stride=d)
            qb = qf_ref[cur, :].astype(bf16)
            kcat = jnp.concatenate([k_ref[prev, :], k_ref[cur, :]], axis=0).astype(bf16)
            vcat = jnp.concatenate([v_ref[prev, :], v_ref[cur, :]], axis=0).astype(bf16)
            s = _nt_dot(qb, kcat) * scale + bias
            s = jnp.where((col >= BLK) | (blk > 0), s, NEG_INF)
            m = s.max(axis=1, keepdims=True)
            e = jnp.exp(s - m)
            ssum = e.sum(axis=1, keepdims=True)
            og_ref[g, cur, :] = jnp.dot(e.astype(bf16), vcat, preferred_element_type=f32) / ssum
            lse_ref[g, cur, :] = jnp.broadcast_to(m + jnp.log(ssum), (BLK, HEAD_DIM))
            return carry

        lax.fori_loop(0, d * nb, blk_body, 0, unroll=unroll)

    def merge_body(cb, carry):
        rows = pl.ds(pl.multiple_of(cb * BLK, BLK), BLK)
        lses = [lse_ref[g, rows, :] for g in range(len(rates))]
        top = functools.reduce(jnp.maximum, lses)
        wts = [jnp.exp(x - top) for x in lses]
        num = sum(w * og_ref[g, rows, :] for g, w in enumerate(wts))
        gt = gate_ref[rows, :]
        o_ref[rows, :] = (num / sum(wts) * _silu(gt)).astype(o_ref.dtype)
        return carry

    lax.fori_loop(0, seq // BLK, merge_body, 0)


def _attend_prompt_b(q, k, v, gate, bias_tab, batch, seq, buffers=None):
    n_heads = k.shape[1] // HEAD_DIM
    n_dil = len(DIL_RATES)
    blk = (seq, HEAD_DIM)
    qspecs = [pl.BlockSpec(blk, lambda b, h, g=g: (b, g * n_heads + h)) for g in range(n_dil)]
    hspec = pl.BlockSpec(blk, lambda b, h: (b, h))
    in_specs = qspecs + [hspec, hspec, hspec, pl.BlockSpec((1, BLK, bias_tab.shape[2]), lambda b, h: (h, 0, 0))]
    out_specs, out_shape = [hspec], [jax.ShapeDtypeStruct(k.shape, bf16)]
    scratch = [pltpu.VMEM(blk, f32), pltpu.VMEM((n_dil,) + blk, f32), pltpu.VMEM((n_dil,) + blk, f32)]
    args, shift = [q, q, q, k, v, gate, bias_tab], None
    if buffers is not None:
        buf_k, buf_v, layer, new_k, new_v = buffers
        plan = _shift_plan(buf_k, new_k)
        if 2 * plan["dbatch"] * ((plan["rows"] - plan["shift"]) // plan["chunk"]) + SHIFT_RING <= batch * n_heads:
            shift = dict(plan, layer=layer)
            anyspec = pl.BlockSpec(memory_space=pl.ANY)
            out = jax.ShapeDtypeStruct((plan["dbatch"], plan["rows"], buf_k.shape[3]), buf_k.dtype)
            in_specs += [anyspec] * 4
            out_specs += [anyspec] * 2
            out_shape += [out, out]
            scratch += _shift_scratch(plan, buf_k)
            args += [buf_k, buf_v, new_k, new_v]
    res = pl.pallas_call(
        functools.partial(_attend_prompt_b_kernel, rates=DIL_RATES, scale=HEAD_DIM ** -0.5, unroll=16, shift=shift),
        name="attend_prompt_b", grid=(batch, n_heads),
        in_specs=in_specs, out_specs=out_specs, out_shape=out_shape, scratch_shapes=scratch,
        compiler_params=_params(2),
    )(*args)
    if shift is not None:
        return tuple(res)
    if buffers is not None:
        return (res[0],) + tuple(_shift_buffers(*buffers))
    return res[0], None, None


def _attend_sample_b_kernel(q_ref, k_ref, v_ref, kt_ref, vt_ref, bias_ref, gate_ref, o_ref, *, n_dil, t_new, scale):
    wb, hgs, _ = k_ref.shape
    k2 = k_ref.reshape(wb * hgs, HEAD_DIM)
    v2 = v_ref.reshape(wb * hgs, HEAD_DIM)
    for hl in range(hgs):
        cols = slice(hl * HEAD_DIM, (hl + 1) * HEAD_DIM)
        head = pl.ds(hl, wb, stride=hgs)
        q = q_ref[0, hl]
        s_main = _nt_dot(q, k2[head, :].astype(bf16)) * scale + bias_ref[hl, :, :wb]
        s_tail = _nt_dot(q, kt_ref[0, :, cols].astype(bf16)) * scale + bias_ref[hl, :, wb:]
        m = jnp.maximum(s_main.max(axis=1, keepdims=True), s_tail.max(axis=1, keepdims=True))
        e_main = jnp.exp(s_main - m)
        e_tail = jnp.exp(s_tail - m)
        ssum = e_main.sum(axis=1, keepdims=True) + e_tail.sum(axis=1, keepdims=True)
        o = (jnp.dot(e_main.astype(bf16), v2[head, :].astype(bf16), preferred_element_type=f32)
             + jnp.dot(e_tail.astype(bf16), vt_ref[0, :, cols].astype(bf16), preferred_element_type=f32)) / ssum
        lse = m + jnp.log(ssum)
        lses = [lse[g * t_new:(g + 1) * t_new] for g in range(n_dil)]
        top = functools.reduce(jnp.maximum, lses)
        wts = [jnp.exp(x - top) for x in lses]
        num = sum(w * o[g * t_new:(g + 1) * t_new] for g, w in enumerate(wts))
        o_ref[0, hl] = num / sum(wts) * _silu(gate_ref[0, hl])


def _attend_sample_b(q16, buf_k, buf_v, layer, k_tail, v_tail, bias_tab, gate4, t_new):
    dbatch, n_heads, rows, _ = q16.shape
    wb = buf_k.shape[2]
    hgs = 8 if n_heads % 8 == 0 else n_heads
    grouped = lambda a: a.reshape(a.shape[0], dbatch, wb, n_heads // hgs, hgs, HEAD_DIM)
    bufspec = pl.BlockSpec((None, None, wb, None, hgs, HEAD_DIM), lambda b, hg: (layer, b, 0, hg, 0, 0))
    tailspec = pl.BlockSpec((1, LANE, hgs * HEAD_DIM), lambda b, hg: (b, 0, hg))
    hspec = lambda r: pl.BlockSpec((1, hgs, r, HEAD_DIM), lambda b, hg: (b, hg, 0, 0))
    return pl.pallas_call(
        functools.partial(_attend_sample_b_kernel, n_dil=len(DIL_RATES), t_new=t_new, scale=HEAD_DIM ** -0.5),
        name="attend_sample_b", grid=(dbatch, n_heads // hgs),
        in_specs=[hspec(rows), bufspec, bufspec, tailspec, tailspec,
                  pl.BlockSpec((hgs, rows, wb + LANE), lambda b, hg: (hg, 0, 0)),
                  hspec(t_new)],
        out_specs=hspec(t_new),
        out_shape=jax.ShapeDtypeStruct((dbatch, n_heads, t_new, HEAD_DIM), f32),
        compiler_params=_params(2),
    )(q16, grouped(buf_k), grouped(buf_v), k_tail, v_tail, bias_tab, gate4)


SHIFT_RING = 4
SHIFT_LAG = 2
SHIFT_CHUNK_BYTES = 5 << 20


def _shift_kernel(bk_ref, bv_ref, nk_ref, nv_ref, ok_ref, ov_ref, ring, in_sem, out_sem, new_sem,
                  *, layer, dbatch, rows, shift, chunk):
    reads, writes, tails = [], [], []
    n_chunks = (rows - shift) // chunk
    for a, (src, new, dst) in enumerate(((bk_ref, nk_ref, ok_ref), (bv_ref, nv_ref, ov_ref))):
        for b in range(dbatch):
            for c in range(n_chunks):
                slot = len(reads) % SHIFT_RING
                reads.append(pltpu.make_async_copy(src.at[layer, b, pl.ds(shift + c * chunk, chunk)],
                                                   ring.at[slot], in_sem.at[slot]))
                writes.append(pltpu.make_async_copy(ring.at[slot], dst.at[b, pl.ds(c * chunk, chunk)],
                                                    out_sem.at[slot]))
            tails.append(pltpu.make_async_copy(new.at[b], dst.at[b, pl.ds(rows - shift, shift)], new_sem.at[a, b]))
    for t in tails:
        t.start()
    n = len(reads)
    for i in range(n + SHIFT_LAG):
        if i < n:
            if i >= SHIFT_RING:
                writes[i - SHIFT_RING].wait()
            reads[i].start()
        if 0 <= i - SHIFT_LAG < n:
            reads[i - SHIFT_LAG].wait()
            writes[i - SHIFT_LAG].start()
    for i in range(max(0, n - SHIFT_RING), n):
        writes[i].wait()
    for t in tails:
        t.wait()


def _shift_plan(buf, new):
    _, dbatch, rows, width = buf.shape
    shift = new.shape[1]
    body = (rows - shift) // shift
    per = max(u for u in range(1, body + 1)
              if body % u == 0 and u * shift * width * buf.dtype.itemsize <= SHIFT_CHUNK_BYTES)
    return dict(dbatch=dbatch, rows=rows, shift=shift, chunk=per * shift)


def _shift_scratch(plan, buf):
    return [pltpu.VMEM((SHIFT_RING, plan["chunk"], buf.shape[3]), buf.dtype),
            pltpu.SemaphoreType.DMA((SHIFT_RING,)), pltpu.SemaphoreType.DMA((SHIFT_RING,)),
            pltpu.SemaphoreType.DMA((2, plan["dbatch"]))]


def _shift_ring_step(t, n_steps, bufs, news, outs, ring, in_sem, out_sem, new_sem, *, layer, dbatch, rows, shift, chunk):
    per_b = (rows - shift) // chunk
    per_arr = dbatch * per_b
    n = len(bufs) * per_arr

    def for_chunk(i, read, action):
        slot = i % SHIFT_RING
        b, c = (i % per_arr) // per_b, i % per_b
        for a in range(len(bufs)):
            @pl.when(i // per_arr == a)
            def _(a=a):
                if read:
                    action(pltpu.make_async_copy(bufs[a].at[layer, b, pl.ds(shift + c * chunk, chunk)],
                                                 ring.at[slot], in_sem.at[slot]))
                else:
                    action(pltpu.make_async_copy(ring.at[slot], outs[a].at[b, pl.ds(c * chunk, chunk)],
                                                 out_sem.at[slot]))

    tails = [pltpu.make_async_copy(news[a].at[b], outs[a].at[b, pl.ds(rows - shift, shift)], new_sem.at[a, b])
             for a in range(len(bufs)) for b in range(dbatch)]

    @pl.when(t == 0)
    def _():
        for cp in tails:
            cp.start()

    @pl.when((t >= SHIFT_RING) & (t < n + SHIFT_RING))
    def _():
        for_chunk(t - SHIFT_RING, False, lambda cp: cp.wait())

    @pl.when(t < n)
    def _():
        for_chunk(t, True, lambda cp: cp.start())

    @pl.when((t >= SHIFT_LAG) & (t < n + SHIFT_LAG))
    def _():
        for_chunk(t - SHIFT_LAG, True, lambda cp: cp.wait())
        for_chunk(t - SHIFT_LAG, False, lambda cp: cp.start())

    @pl.when(t == n_steps - 1)
    def _():
        for cp in tails:
            cp.wait()


def _shift_buffers(buf_k, buf_v, layer, new_k, new_v):
    plan = _shift_plan(buf_k, new_k)
    anyspec = pl.BlockSpec(memory_space=pl.ANY)
    out = jax.ShapeDtypeStruct((plan["dbatch"], plan["rows"], buf_k.shape[3]), buf_k.dtype)
    return pl.pallas_call(
        functools.partial(_shift_kernel, layer=layer, **plan), name="shift_buffers",
        in_specs=[anyspec] * 4, out_specs=[anyspec] * 2, out_shape=[out, out],
        scratch_shapes=_shift_scratch(plan, buf_k),
        compiler_params=pltpu.CompilerParams(vmem_limit_bytes=VMEM_LIMIT),
    )(buf_k, buf_v, new_k, new_v)


def _offsets(sizes):
    return [int(x) for x in np.cumsum((0,) + tuple(sizes))[:-1]]


def _pad_rows(a, rows):
    return jnp.pad(a, ((0, 0), (0, rows - a.shape[1])) + ((0, 0),) * (a.ndim - 2))


def _layer_a(xp, xs, dims, j, last, cache_k, cache_v, cache_ki, page_table, w_in, w_out, ln_g, ln_b, rel_bias):
    batch, seq, dbatch, t_new = dims
    (xp, xpb), (xs, xsb) = xp, xs
    d_model = xp.shape[1]
    n_heads = d_model // HEAD_DIM
    kv, hi = KV_HEADS_A, IDX_HEADS
    group = n_heads // kv
    branch = n_heads * HEAD_DIM
    sizes = (branch, kv * HEAD_DIM, kv * HEAD_DIM, branch, hi * IDX_DIM, IDX_DIM, hi)
    offs = _offsets(sizes)
    tail_w = 2 * LANE
    w_t = w_in.shape[2] % LANE != 0
    w_use = jnp.swapaxes(w_in, 1, 2) if w_t else w_in
    tail_pad = ((0, 0), (0, tail_w - IDX_DIM - hi), (0, 0)) if w_t else ((0, 0), (0, 0), (0, tail_w - IDX_DIM - hi))
    w_tail = jnp.pad(w_use[j:j + 1, offs[5]:, :] if w_t else w_use[j:j + 1, :, offs[5]:], tail_pad)

    mm = lambda slot, dt, nm: _matmul(xpb, xsb, w_use, j, offs[slot], sizes[slot], dt, f"proj_a_{nm}", w_t)
    (q, q_s), (k_p, k_s), (v_p, v_s) = mm(0, bf16, "q"), mm(1, f32, "k"), mm(2, f32, "v")
    (gate, gate_s), (qi, qi_s) = mm(3, f32, "gate"), mm(4, bf16, "qi")
    kw, kw_s = _matmul(xpb, xsb, w_tail, 0, 0, tail_w, f32, "proj_a_kiwi", w_t)
    ki_p, wi = kw[:, :IDX_DIM], kw[:, IDX_DIM:IDX_DIM + hi]
    ki_s, wi_s = kw_s[:, :IDX_DIM], kw_s[:, IDX_DIM:IDX_DIM + hi]

    dist = (np.arange(ATTN_A_TQ)[:, None] + seq - ATTN_A_TQ) - np.arange(seq + ATTN_A_CHUNK - ATTN_A_TQ)[None, :]
    bias_p = _bias_table(rel_bias, _bucket_codes(dist))
    maskadd = _select_prompt(qi, ki_p, wi.T, batch, seq, min(TOPK_MAX, seq // 4))
    og = _attend_prompt(q, k_p, v_p, maskadd, bias_p, gate, batch, seq)

    q, gate, qi, wi = q_s, gate_s, qi_s, wi_s
    n_pages = page_table.shape[1]
    past = n_pages * PAGE_SIZE
    n_keys = past + t_new
    n_lanes = past + PAGE_SIZE
    cscale = (IDX_DIM ** -0.5) * (hi ** -0.5)
    qs = qi.reshape(dbatch, t_new * hi, IDX_DIM)
    wbc = jnp.broadcast_to((wi * cscale).reshape(dbatch, t_new * hi, 1), (dbatch, t_new * hi, LANE))
    ki_tail = _pad_rows(ki_s.reshape(dbatch, t_new, IDX_DIM), PAGE_SIZE)
    scores = _sample_scores(page_table, qs, wbc, cache_ki, j, ki_tail, t_new)
    mask_s = _select_sample(scores.reshape(dbatch * t_new, n_lanes), past, t_new, min(TOPK_MAX, n_keys // 4))

    def to_rows(a):
        a = a.reshape(dbatch, t_new, kv, group, HEAD_DIM)
        return a.transpose(0, 2, 3, 1, 4).reshape(dbatch, kv * group * t_new, HEAD_DIM)

    def to_page(a):
        return _pad_rows(a.reshape(dbatch, t_new, kv, HEAD_DIM), PAGE_SIZE).reshape(dbatch, PAGE_SIZE * kv, HEAD_DIM)

    rows = group * t_new
    dist = (past + np.arange(t_new)[:, None]) - np.arange(n_lanes)[None, :]
    bias_s = _bias_table(rel_bias, _bucket_codes(dist)).reshape(kv * rows, n_lanes)
    mask_g = jnp.broadcast_to(mask_s.reshape(dbatch, 1, t_new, n_lanes), (dbatch, group, t_new, n_lanes))
    mask_g = mask_g.reshape(dbatch, rows, n_lanes)
    pool = cache_k.shape[1]
    paged = lambda c: c.reshape(c.shape[0], pool, PAGE_SIZE * kv, HEAD_DIM)
    og2 = _attend_sample_a(page_table, to_rows(q), bias_s, mask_g, paged(cache_k), paged(cache_v), j,
                           to_page(k_s), to_page(v_s), to_rows(gate))
    og_s = og2.reshape(dbatch, kv, group, t_new, HEAD_DIM).transpose(0, 3, 1, 2, 4)
    og_s = og_s.reshape(dbatch * t_new, branch).astype(bf16)
    z, z_s = _matmul(og, og_s, w_out, j, 0, d_model, f32, "out_a", residual=(xp, xs))
    xp_new = _layernorm(z, ln_g[j], ln_b[j], not last)
    xs_new = _layernorm(z_s, ln_g[j], ln_b[j], not last)

    shp = lambda a, b_, t_, *rest: a.reshape(b_, t_, *rest)
    outs = (shp(k_p, batch, seq, kv, HEAD_DIM), shp(v_p, batch, seq, kv, HEAD_DIM), shp(ki_p, batch, seq, IDX_DIM),
            shp(k_s, dbatch, t_new, kv, HEAD_DIM), shp(v_s, dbatch, t_new, kv, HEAD_DIM),
            shp(ki_s, dbatch, t_new, IDX_DIM))
    return xp_new, xs_new, outs


def _layer_b(xp, xs, dims, j, last, buf_k, buf_v, w_in, w_out, ln_g, ln_b, rel_bias):
    batch, seq, dbatch, t_new = dims
    (xp, xpb), (xs, xsb) = xp, xs
    d_model = xp.shape[1]
    n_heads = d_model // HEAD_DIM
    branch = n_heads * HEAD_DIM
    n_dil = len(DIL_RATES)
    sizes = (n_dil * branch, branch, branch, branch)
    offs = _offsets(sizes)

    mm = lambda slot, dt, nm: _matmul(xpb, xsb, w_in, j, offs[slot], sizes[slot], dt, f"proj_b_{nm}")
    (q, q_s), (gate, gate_s) = mm(0, bf16, "q"), mm(3, f32, "gate")
    k_p, k_s, k_heads = _matmul_heads(xpb, xsb, w_in, j, offs[1], n_heads, "proj_b_k")
    v_p, v_s, v_heads = _matmul_heads(xpb, xsb, w_in, j, offs[2], n_heads, "proj_b_v")

    m =(np.arange(BLK)[:, None] + BLK) - np.arange(2 * BLK)[None, :]
    codes = [_bucket_codes(d * np.clip(m, 0, w // d), (m >= 0) & (m <= w // d))
             for d, w in zip(DIL_RATES, DIL_WINDOWS)]
    bias_p = _bias_table(rel_bias, np.concatenate(codes, axis=1))
    wb = buf_k.shape[2]
    flat = lambda buf: buf.reshape(buf.shape[0], dbatch, wb * n_heads, HEAD_DIM)
    og, new_k, new_v = _attend_prompt_b(q, k_p, v_p, gate, bias_p, batch, seq,
                                        (flat(buf_k), flat(buf_v), j, k_s.reshape(dbatch, t_new * n_heads, HEAD_DIM),
                                         v_s.reshape(dbatch, t_new * n_heads, HEAD_DIM)))

    q, gate = q_s, gate_s
    rows = -(-(n_dil * t_new) // 16) * 16
    pos = np.arange(wb + LANE)[None, :]
    codes = []
    for d, w in zip(DIL_RATES, DIL_WINDOWS):
        dist = wb + np.arange(t_new)[:, None] - pos
        codes.append(_bucket_codes(dist, (dist >= 0) & (dist % d == 0) & (dist // d <= w // d) & (pos < wb + t_new)))
    codes.append(np.full((rows - n_dil * t_new, wb + LANE), BUCKET_ZERO, np.int32))
    bias_s = _bias_table(rel_bias, np.concatenate(codes, axis=0))
    q16 = q.reshape(dbatch, t_new, n_dil, n_heads, HEAD_DIM).transpose(0, 3, 2, 1, 4)
    q16 = q16.reshape(dbatch, n_heads, n_dil * t_new, HEAD_DIM)
    q16 = jnp.pad(q16, ((0, 0), (0, 0), (0, rows - n_dil * t_new), (0, 0)))
    gate4 = gate.reshape(dbatch, t_new, n_heads, HEAD_DIM).transpose(0, 2, 1, 3)
    k_tail = _pad_rows(k_s.reshape(dbatch, t_new, branch), LANE)
    v_tail = _pad_rows(v_s.reshape(dbatch, t_new, branch), LANE)
    og4 = _attend_sample_b(q16, buf_k, buf_v, j, k_tail, v_tail, bias_s, gate4, t_new)
    og_s = og4.transpose(0, 2, 1, 3).reshape(dbatch * t_new, branch).astype(bf16)
    z, z_s = _matmul(og, og_s, w_out, j, 0, d_model, f32, "out_b", residual=(xp, xs))
    xp_new = _layernorm(z, ln_g[j], ln_b[j], not last)
    xs_new = _layernorm(z_s, ln_g[j], ln_b[j], not last)

    wp = min(W_MAX, seq)
    k_win = k_heads.reshape(batch, seq, n_heads, HEAD_DIM)[:, seq - wp:]
    v_win = v_heads.reshape(batch, seq, n_heads, HEAD_DIM)[:, seq - wp:]
    return xp_new, xs_new, (k_win, v_win, new_k.reshape(dbatch, wb, n_heads, HEAD_DIM),
                            new_v.reshape(dbatch, wb, n_heads, HEAD_DIM))


def kernel(x_prompt, x_sample, cache_k_a, cache_v_a, cache_kidx_a, cache_k_b, cache_v_b, page_table, rel_bias,
           w_in_a, w_out_a, ln_g_a, ln_b_a, w_in_b, w_out_b, ln_g_b, ln_b_b):
    batch, seq, d_model = x_prompt.shape
    dbatch, t_new, _ = x_sample.shape
    dims = (batch, seq, dbatch, t_new)
    with_bf16 = lambda a: (a, a.astype(bf16))
    xp = with_bf16(x_prompt.reshape(batch * seq, d_model))
    xs = with_bf16(x_sample.reshape(dbatch * t_new, d_model))
    outs_a, outs_b = [], []
    for layer in range(DEPTH):
        j = layer // 2
        if layer % 2 == 0:
            xp, xs, o = _layer_a(xp, xs, dims, j, layer == DEPTH - 1, cache_k_a, cache_v_a, cache_kidx_a, page_table,
                                 w_in_a, w_out_a, ln_g_a, ln_b_a, rel_bias)
            outs_a.append(o)
        else:
            xp, xs, o = _layer_b(xp, xs, dims, j, layer == DEPTH - 1, cache_k_b, cache_v_b,
                                 w_in_b, w_out_b, ln_g_b, ln_b_b, rel_bias)
            outs_b.append(o)
    stack = lambda group, idx: group[0][idx][None] if len(group) == 1 else jnp.stack([o[idx] for o in group])
    return (xp[0].reshape(batch, seq, d_model), xs[0].reshape(dbatch, t_new, d_model),
            *(stack(outs_a, n) for n in range(6)), *(stack(outs_b, n) for n in range(4)))
```

```python
import functools
import math

import jax
import jax.numpy as jnp
import numpy as np
from jax import lax
from jax.experimental import pallas as pl
from jax.experimental.pallas import tpu as pltpu

HEAD_DIM = 128
KV_HEADS_A = 8
IDX_HEADS = 64
IDX_DIM = 128
TOPK_MAX = 256
DIL_WINDOWS = (128, 512, 2048)
DIL_RATES = (1, 4, 16)
W_MAX = max(DIL_WINDOWS)
BLK = 128
N_BUCKETS = 32
REL_MAX_DIST = 2048
DEPTH = 2
ALPHA = (2 * DEPTH) ** 0.25
LN_EPS = 1e-5
PAGE_SIZE = 128

LANE = 128
NEG_INF = float("-inf")
INT_MIN = -(2 ** 31)
M_INIT = -1e30
BUCKET_ZERO = -1
BUCKET_MASKED = -2
VMEM_LIMIT = 56 * 1024 * 1024

f32 = jnp.float32
bf16 = jnp.bfloat16
i32 = jnp.int32


def _params(n_axes, vmem=VMEM_LIMIT):
    return pltpu.CompilerParams(dimension_semantics=("arbitrary",) * n_axes, vmem_limit_bytes=vmem)


def _nt_dot(a, b):
    return lax.dot_general(a, b, (((1,), (1,)), ((), ())), preferred_element_type=f32)


def _silu(x):
    return x * (1.0 / (1.0 + jnp.exp(-x)))


def _softmax_step(s, m_old, l_old):
    m_new = jnp.maximum(m_old, s.max(axis=1, keepdims=True))
    alpha = jnp.exp(m_old - m_new)
    p = jnp.exp(s - m_new)
    return m_new, alpha * l_old + p.sum(axis=1, keepdims=True), alpha, p


def _mm_kernel(x_ref, xs_ref, w_ref, *rest, chunk, transposed, residual):
    if residual:
        r_ref, rs_ref, o_ref, os_ref, wbf_ref = rest
    else:
        o_ref, os_ref, wbf_ref = rest
    mul = _nt_dot if transposed else functools.partial(jnp.dot, preferred_element_type=f32)

    @pl.when(pl.program_id(1) == 0)
    def _():
        def body(r, c):
            rows = pl.ds(pl.multiple_of(r * chunk, chunk), chunk)
            wbf_ref[rows, :] = w_ref[rows, :].astype(bf16)
            return c
        lax.fori_loop(0, w_ref.shape[0] // chunk, body, 0)
        ys = mul(xs_ref[...], wbf_ref[...])
        os_ref[...] = (ALPHA * rs_ref[...] + ys if residual else ys).astype(os_ref.dtype)

    y = mul(x_ref[...], wbf_ref[...])
    o_ref[...] = (ALPHA * r_ref[...] + y if residual else y).astype(o_ref.dtype)


def _pick_tile(n, candidates):
    for c in candidates:
        if n % c == 0:
            return c
    raise ValueError(f"no tile for {n}")


def _matmul(x_bf, xs_bf, w, layer, col_off, ncols, out_dtype, name, transposed=False, residual=None):
    m, k = x_bf.shape
    ms = xs_bf.shape[0]
    tn = _pick_tile(math.gcd(col_off, ncols) if col_off else ncols, (512, 256, 128))
    tm = _pick_tile(m, (1024, 512, 256, 128))
    off_blocks = col_off // tn
    if transposed:
        wblock, wspec, chunk = (tn, k), pl.BlockSpec((None, tn, k), lambda n, i: (layer, n + off_blocks, 0)), LANE
    else:
        wblock, wspec = (k, tn), pl.BlockSpec((None, k, tn), lambda n, i: (layer, 0, n + off_blocks))
        chunk = _pick_tile(k, (256, 128))
    ospecs = [pl.BlockSpec((tm, tn), lambda n, i: (i, n)), pl.BlockSpec((ms, tn), lambda n, i: (0, n))]
    return pl.pallas_call(
        functools.partial(_mm_kernel, chunk=chunk, transposed=transposed, residual=residual is not None), name=name,
        grid=(ncols // tn, m // tm),
        in_specs=[pl.BlockSpec((tm, k), lambda n, i: (i, 0)),
                  pl.BlockSpec((ms, k), lambda n, i: (0, 0)),
                  wspec] + (ospecs if residual is not None else []),
        out_specs=ospecs,
        out_shape=[jax.ShapeDtypeStruct((m, ncols), out_dtype), jax.ShapeDtypeStruct((ms, ncols), out_dtype)],
        scratch_shapes=[pltpu.VMEM(wblock, bf16)],
        compiler_params=_params(2),
    )(x_bf, xs_bf, w, *(residual or ()))


def _mm_heads_kernel(x_ref, xs_ref, w_ref, o_ref, os_ref, oh_ref, wbf_ref, *, k_chunk):
    @pl.when(pl.program_id(1) == 0)
    def _():
        def body(r, c):
            rows = pl.ds(pl.multiple_of(r * k_chunk, k_chunk), k_chunk)
            wbf_ref[rows, :] = w_ref[rows, :].astype(bf16)
            return c
        lax.fori_loop(0, w_ref.shape[0] // k_chunk, body, 0)
        os_ref[...] = jnp.dot(xs_ref[...], wbf_ref[...], preferred_element_type=f32)

    res = jnp.dot(x_ref[...], wbf_ref[...], preferred_element_type=f32)
    o_ref[...] = res
    tm, nh, dh = oh_ref.shape
    flat = oh_ref.reshape(tm * nh, dh)
    for hh in range(nh):
        flat[pl.ds(hh, tm, stride=nh), :] = res[:, hh * dh:(hh + 1) * dh]


def _matmul_heads(x_bf, xs_bf, w, layer, col_off, n_heads, name):
    m, k = x_bf.shape
    ms = xs_bf.shape[0]
    nh = 8 if n_heads % 8 == 0 else n_heads
    tn = nh * HEAD_DIM
    ncols = n_heads * HEAD_DIM
    tm = _pick_tile(m, (512, 256, 128))
    off_blocks = col_off // tn
    k_chunk = _pick_tile(k, (256, 128))
    return pl.pallas_call(
        functools.partial(_mm_heads_kernel, k_chunk=k_chunk), name=name,
        grid=(ncols // tn, m // tm),
        in_specs=[pl.BlockSpec((tm, k), lambda n, i: (i, 0)),
                  pl.BlockSpec((ms, k), lambda n, i: (0, 0)),
                  pl.BlockSpec((None, k, tn), lambda n, i: (layer, 0, n + off_blocks),
                               pipeline_mode=pl.Buffered(1))],
        out_specs=[pl.BlockSpec((tm, tn), lambda n, i: (i, n)),
                   pl.BlockSpec((ms, tn), lambda n, i: (0, n)),
                   pl.BlockSpec((tm, nh, HEAD_DIM), lambda n, i: (i, n, 0))],
        out_shape=[jax.ShapeDtypeStruct((m, ncols), f32), jax.ShapeDtypeStruct((ms, ncols), f32),
                   jax.ShapeDtypeStruct((m, n_heads, HEAD_DIM), f32)],
        scratch_shapes=[pltpu.VMEM((k, tn), bf16)],
        compiler_params=_params(2),
    )(x_bf, xs_bf, w)


def _ln_kernel(z_ref, g_ref, b_ref, o_ref, *obf_ref):
    z = z_ref[...]
    mu = jnp.mean(z, axis=-1, keepdims=True)
    zc = z - mu
    var = jnp.mean(zc * zc, axis=-1, keepdims=True)
    out = zc * lax.rsqrt(var + LN_EPS) * g_ref[...] + b_ref[...]
    o_ref[...] = out
    for ref in obf_ref:
        ref[...] = out.astype(bf16)


def _layernorm(z, g, b, with_bf16):
    m, d = z.shape
    tm = _pick_tile(m, (128, 64, 32, 16, 8))
    row = pl.BlockSpec((tm, d), lambda i: (i, 0))
    vec = pl.BlockSpec((1, d), lambda i: (0, 0))
    n_out = 2 if with_bf16 else 1
    res = pl.pallas_call(
        _ln_kernel, grid=(m // tm,), name="layernorm",
        in_specs=[row, vec, vec], out_specs=[row] * n_out,
        out_shape=[jax.ShapeDtypeStruct((m, d), f32), jax.ShapeDtypeStruct((m, d), bf16)][:n_out],
        compiler_params=_params(1),
    )(z, g.reshape(1, d), b.reshape(1, d))
    return (res[0], res[1] if with_bf16 else None)


def _rel_bucket_np(n):
    max_exact = N_BUCKETS // 2
    nf = np.maximum(n, 1).astype(np.float64)
    large = max_exact + (np.log(nf / max_exact) / math.log(REL_MAX_DIST / max_exact)
                         * (N_BUCKETS - max_exact)).astype(np.int64)
    large = np.minimum(large, N_BUCKETS - 1)
    return np.where(n < max_exact, n, large).astype(np.int32)


def _bucket_codes(dist, valid=None):
    code = np.where(dist >= 0, _rel_bucket_np(np.maximum(dist, 0)), BUCKET_ZERO)
    if valid is not None:
        code = np.where(valid, code, BUCKET_MASKED)
    return code.astype(np.int32)


def _bias_kernel(tbl_ref, bm_ref, o_ref, *, cw):
    h = pl.program_id(0)
    row = tbl_ref[pl.ds(h, 1), :]

    def body(c, carry):
        for t in range(cw // LANE):
            cols = pl.ds(pl.multiple_of(c * cw + t * LANE, LANE), LANE)
            code = bm_ref[:, cols]
            val = jnp.take_along_axis(jnp.broadcast_to(row, code.shape), jnp.maximum(code, 0), axis=1)
            o_ref[0, :, cols] = jnp.where(code >= 0, val, jnp.where(code == BUCKET_MASKED, NEG_INF, 0.0))
        return carry

    lax.fori_loop(0, bm_ref.shape[1] // cw, body, 0, unroll=8)


def _bias_table(rel_bias, codes):
    r, c = codes.shape
    n_heads = rel_bias.shape[1]
    units = c // LANE
    per = max(1, 16 // -(-r // 8))
    cw = LANE * max(u for u in range(1, units + 1) if units % u == 0 and u <= per)
    return pl.pallas_call(
        functools.partial(_bias_kernel, cw=cw), name="bias_table",
        grid=(n_heads,),
        in_specs=[pl.BlockSpec((n_heads, LANE), lambda h: (0, 0)),
                  pl.BlockSpec((r, c), lambda h: (0, 0))],
        out_specs=pl.BlockSpec((1, r, c), lambda h: (h, 0, 0)),
        out_shape=jax.ShapeDtypeStruct((n_heads, r, c), f32),
        compiler_params=_params(1),
    )(jnp.pad(rel_bias.T, ((0, 0), (0, LANE - N_BUCKETS))), jnp.asarray(codes))


def _sortable_key(x):
    bits = pltpu.bitcast(x, i32)
    return bits ^ ((bits >> 31) & 0x7FFFFFFF)


def _kth_largest_key(count_ge, shape, k_sel):
    def body(it, t):
        cand = t + lax.shift_left(jnp.int32(1), 31 - it)
        return jnp.where(count_ge(cand) >= k_sel, cand, t)
    return lax.fori_loop(0, 32, body, jnp.full(shape, INT_MIN, i32))


def _select_prompt_kernel(qi_ref, ki_ref, wt_ref, m_ref, kibf_ref, key_ref, *, n_idx_heads, k_sel, ck, hb, cscale):
    i = pl.program_id(1)
    tq = qi_ref.shape[0]

    @pl.when(i == 0)
    def _():
        kibf_ref[...] = ki_ref[...].astype(bf16)

    nck = ((i + 1) * tq + ck - 1) // ck
    t_row = i * tq + lax.broadcasted_iota(i32, (1, tq), 1)

    def chunk_body(c, carry):
        koff = pl.multiple_of(c * ck, ck)
        kc = kibf_ref[pl.ds(koff, ck), :]

        def heads_body(hg, acc):
            hoff = hg * (hb * IDX_DIM)
            qstack = jnp.concatenate(
                [qi_ref[:, pl.ds(pl.multiple_of(hoff + j * IDX_DIM, IDX_DIM), IDX_DIM)] for j in range(hb)], axis=0)
            s = _nt_dot(kc, qstack)
            w = wt_ref[pl.ds(pl.multiple_of(hg * hb, hb), hb), :] * cscale
            for j in range(hb):
                acc = acc + jnp.maximum(s[:, j * tq:(j + 1) * tq], 0.0) * w[j:j + 1, :]
            return acc

        acc = lax.fori_loop(0, n_idx_heads // hb, heads_body, jnp.zeros((ck, tq), f32), unroll=2)
        l_col = koff + lax.broadcasted_iota(i32, (ck, 1), 0)
        key_ref[pl.ds(koff, ck), :] = jnp.where(l_col <= t_row, _sortable_key(acc), INT_MIN)
        return carry

    lax.fori_loop(0, nck, chunk_body, 0)

    def count_ge(cand):
        def body(c, cnt):
            kc = key_ref[pl.ds(pl.multiple_of(c * ck, ck), ck), :]
            hit = jnp.where(kc >= cand, 1, 0).astype(i32)
            return cnt + hit.reshape(ck // 8, 8, tq).sum(axis=0)
        cnt = lax.fori_loop(0, nck, body, jnp.zeros((8, tq), i32))
        return cnt.sum(axis=0, keepdims=True)

    thr = _kth_largest_key(count_ge, (1, tq), k_sel)
    thr = jnp.maximum(thr, INT_MIN + 1)
    n_ge = count_ge(thr)
    ties = jnp.max(n_ge) > k_sel

    m_ref[...] = jnp.full(m_ref.shape, NEG_INF, m_ref.dtype)

    def write_chunk(koff, sel):
        m_ref[:, pl.ds(koff, ck)] = jnp.where(sel, 0.0, NEG_INF).astype(f32).T.astype(m_ref.dtype)

    @pl.when(jnp.logical_not(ties))
    def _():
        def out_body(c, carry):
            koff = pl.multiple_of(c * ck, ck)
            write_chunk(koff, key_ref[pl.ds(koff, ck), :] >= thr)
            return carry
        lax.fori_loop(0, nck, out_body, 0)

    @pl.when(ties)
    def _():
        quota = (k_sel - count_ge(thr + 1)).astype(f32)
        tri = (lax.broadcasted_iota(i32, (ck, ck), 0) >= lax.broadcasted_iota(i32, (ck, ck), 1)).astype(bf16)

        def out_body(c, seen):
            koff = pl.multiple_of(c * ck, ck)
            key = key_ref[pl.ds(koff, ck), :]
            eq = key == thr
            rank = jnp.dot(tri, jnp.where(eq, 1.0, 0.0).astype(bf16), preferred_element_type=f32) + seen
            write_chunk(koff, jnp.where(eq, jnp.where(rank <= quota, thr, thr - 1), key) >= thr)
            return rank[ck - 1:ck, :]
        lax.fori_loop(0, nck, out_body, jnp.zeros((1, tq), f32))


SELECT_KEY_CHUNK = 256
SELECT_HEADS_PER_STEP = 16


def _select_prompt(qi, ki, wi_t, batch, seq, k_sel):
    n_idx_heads = wi_t.shape[0]
    tq, ck = BLK, SELECT_KEY_CHUNK
    hb = _pick_tile(n_idx_heads, tuple(SELECT_HEADS_PER_STEP >> s for s in range(SELECT_HEADS_PER_STEP.bit_length())))
    nblk = seq // tq
    cscale = (IDX_DIM ** -0.5) * (n_idx_heads ** -0.5)
    return pl.pallas_call(
        functools.partial(_select_prompt_kernel, n_idx_heads=n_idx_heads, k_sel=k_sel, ck=ck, hb=hb, cscale=cscale),
        name="select_prompt", grid=(batch, nblk),
        in_specs=[pl.BlockSpec((tq, n_idx_heads * IDX_DIM), lambda b, i: (b * nblk + i, 0)),
                  pl.BlockSpec((seq, IDX_DIM), lambda b, i: (b, 0)),
                  pl.BlockSpec((n_idx_heads, tq), lambda b, i: (0, b * nblk + i))],
        out_specs=pl.BlockSpec((tq, seq), lambda b, i: (b * nblk + i, 0)),
        out_shape=jax.ShapeDtypeStruct((batch * seq, seq), bf16),
        scratch_shapes=[pltpu.VMEM((seq, IDX_DIM), bf16), pltpu.VMEM((seq, tq), i32)],
        compiler_params=_params(2),
    )(qi, ki, wi_t)


def _attend_prompt_kernel(q_ref, k_ref, v_ref, mk_ref, g_ref, gate_ref, o_ref,
                          kbf_ref, vbf_ref, s_ref, mp_ref, l_ref, acc_ref, *, group, ck, sub, scale):
    i = pl.program_id(2)
    nblk = pl.num_programs(2)
    tq = q_ref.shape[0]
    rows = group * tq

    @pl.when(i == 0)
    def _():
        kbf_ref[...] = k_ref[...].astype(bf16)
        vbf_ref[...] = v_ref[...].astype(bf16)

    nck = ((i + 1) * tq + ck - 1) // ck
    goff0 = (nblk - 1 - i) * tq
    qs = jnp.concatenate([q_ref[:, g * HEAD_DIM:(g + 1) * HEAD_DIM] for g in range(group)], axis=0)

    mp_ref[...] = jnp.full(mp_ref.shape, M_INIT, f32)

    def logits_body(c, carry):
        for u in range(ck // sub):
            koff = pl.multiple_of(c * ck + u * sub, sub)
            goff = pl.multiple_of(goff0 + koff, LANE)
            s = _nt_dot(qs, kbf_ref[pl.ds(koff, sub), :]) * scale
            bias = jnp.concatenate([g_ref[g, :, pl.ds(goff, sub)] for g in range(group)], axis=0)
            sel = mk_ref[:, pl.ds(koff, sub)].astype(f32)
            s = s + bias + jnp.concatenate([sel] * group, axis=0)
            s_ref[:, pl.ds(koff, sub)] = s
            mp_ref[...] = functools.reduce(jnp.maximum, [mp_ref[...]] + [s[:, t * LANE:(t + 1) * LANE]
                                                                         for t in range(sub // LANE)])
        return carry

    lax.fori_loop(0, nck, logits_body, 0)
    m_all = jnp.broadcast_to(mp_ref[...].max(axis=1, keepdims=True), (rows, LANE))

    l_ref[...] = jnp.zeros(l_ref.shape, f32)
    acc_ref[...] = jnp.zeros(acc_ref.shape, f32)

    def pv_body(c, carry):
        for u in range(ck // sub):
            koff = pl.multiple_of(c * ck + u * sub, sub)
            ps = [jnp.exp(s_ref[:, pl.ds(pl.multiple_of(koff + t * LANE, LANE), LANE)] - m_all)
                  for t in range(sub // LANE)]
            l_ref[...] += functools.reduce(jnp.add, ps)
            p = jnp.concatenate(ps, axis=1).astype(bf16)
            acc_ref[...] += jnp.dot(p, vbf_ref[pl.ds(koff, sub), :], preferred_element_type=f32)
        return carry

    lax.fori_loop(0, nck, pv_body, 0)
    o = acc_ref[...] / l_ref[...].sum(axis=1, keepdims=True)
    for g in range(group):
        cols = slice(g * HEAD_DIM, (g + 1) * HEAD_DIM)
        o_ref[:, cols] = (o[g * tq:(g + 1) * tq] * _silu(gate_ref[:, cols])).astype(o_ref.dtype)


ATTN_A_CHUNK = 512
ATTN_A_SUB = 256
ATTN_A_TQ = 256


def _attend_prompt(q, k, v, maskadd, bias_tab, gate, batch, seq):
    kv = k.shape[1] // HEAD_DIM
    group = q.shape[1] // (kv * HEAD_DIM)
    tq, ck = ATTN_A_TQ, ATTN_A_CHUNK
    nblk = seq // tq
    gw = group * HEAD_DIM
    qspec = pl.BlockSpec((tq, gw), lambda kh, b, i: (b * nblk + i, kh))
    kvspec = pl.BlockSpec((seq, HEAD_DIM), lambda kh, b, i: (b, kh))
    return pl.pallas_call(
        functools.partial(_attend_prompt_kernel, group=group, ck=ck, sub=ATTN_A_SUB, scale=HEAD_DIM ** -0.5),
        name="attend_prompt_a", grid=(kv, batch, nblk),
        in_specs=[qspec, kvspec, kvspec,
                  pl.BlockSpec((tq, seq), lambda kh, b, i: (b * nblk + i, 0)),
                  pl.BlockSpec((group, tq, bias_tab.shape[2]), lambda kh, b, i: (kh, 0, 0)),
                  qspec],
        out_specs=qspec,
        out_shape=jax.ShapeDtypeStruct(q.shape, bf16),
        scratch_shapes=[pltpu.VMEM((seq, HEAD_DIM), bf16), pltpu.VMEM((seq, HEAD_DIM), bf16),
                        pltpu.VMEM((group * tq, seq), f32), pltpu.VMEM((group * tq, LANE), f32),
                        pltpu.VMEM((group * tq, LANE), f32), pltpu.VMEM((group * tq, HEAD_DIM), f32)],
        compiler_params=_params(3),
    )(q, k, v, maskadd, bias_tab, gate)


def _score_page(qs, wb, kpage, t_new):
    s = _nt_dot(qs, kpage.astype(bf16))
    x = jnp.maximum(s, 0.0) * wb
    return x.reshape(t_new, x.shape[0] // t_new, x.shape[1]).sum(axis=1)


def _sample_scores_kernel(pt_ref, qs_ref, wb_ref, *refs, pg, t_new):
    o_ref = refs[pg]
    for r in range(pg):
        o_ref[0, :, r * PAGE_SIZE:(r + 1) * PAGE_SIZE] = _score_page(qs_ref[0], wb_ref[0], refs[r][...], t_new)


def _sample_scores_tail_kernel(qs_ref, wb_ref, kt_ref, o_ref, *, t_new):
    o_ref[0] = _score_page(qs_ref[0], wb_ref[0], kt_ref[0], t_new)


def _sample_scores(page_table, qs, wb, cache_ki, layer, ki_tail, t_new):
    dbatch, rows, _ = qs.shape
    n_pages = page_table.shape[1]
    pg = _pick_tile(n_pages, (16, 8, 4, 2, 1))
    qspec = pl.BlockSpec((1, rows, IDX_DIM), lambda b, s, pt: (b, 0, 0))
    page_specs = [pl.BlockSpec((None, None, PAGE_SIZE, IDX_DIM),
                               lambda b, s, pt, r=r: (layer, pt[b, s * pg + r], 0, 0)) for r in range(pg)]
    main = pl.pallas_call(
        functools.partial(_sample_scores_kernel, pg=pg, t_new=t_new), name="sample_scores",
        grid_spec=pltpu.PrefetchScalarGridSpec(
            num_scalar_prefetch=1, grid=(dbatch, n_pages // pg),
            in_specs=[qspec, qspec] + page_specs,
            out_specs=pl.BlockSpec((1, t_new, pg * PAGE_SIZE), lambda b, s, pt: (b, 0, s))),
        out_shape=jax.ShapeDtypeStruct((dbatch, t_new, n_pages * PAGE_SIZE), f32),
        compiler_params=_params(2),
    )(page_table, qs, wb, *([cache_ki] * pg))
    spec3 = lambda d1, d2: pl.BlockSpec((1, d1, d2), lambda b: (b, 0, 0))
    tail = pl.pallas_call(
        functools.partial(_sample_scores_tail_kernel, t_new=t_new), name="sample_scores_tail",
        grid=(dbatch,),
        in_specs=[spec3(rows, IDX_DIM), spec3(rows, IDX_DIM), spec3(PAGE_SIZE, IDX_DIM)],
        out_specs=spec3(t_new, PAGE_SIZE),
        out_shape=jax.ShapeDtypeStruct((dbatch, t_new, PAGE_SIZE), f32),
        compiler_params=_params(1),
    )(qs, wb, ki_tail)
    return jnp.concatenate([main, tail], axis=2)


def _select_sample_kernel(s_ref, o_ref, key_ref, *, past, t_new, k_sel):
    rows, n = s_ref.shape
    t = past + lax.broadcasted_iota(i32, (rows, 1), 0) % t_new
    l = lax.broadcasted_iota(i32, (1, n), 1)
    key_ref[...] = jnp.where(l <= t, _sortable_key(s_ref[...]), INT_MIN)

    def count_ge(cand):
        return jnp.where(key_ref[...] >= cand, 1, 0).astype(i32).sum(axis=1, keepdims=True)

    thr = jnp.maximum(_kth_largest_key(count_ge, (rows, 1), k_sel), INT_MIN + 1)
    quota = (k_sel - count_ge(thr + 1)).astype(f32)
    tri = (lax.broadcasted_iota(i32, (LANE, LANE), 0) <= lax.broadcasted_iota(i32, (LANE, LANE), 1)).astype(bf16)

    def tile_body(c, seen):
        lanes = pl.ds(pl.multiple_of(c * LANE, LANE), LANE)
        key = key_ref[:, lanes]
        eq = key == thr
        rank = jnp.dot(jnp.where(eq, 1.0, 0.0).astype(bf16), tri, preferred_element_type=f32) + seen
        kept = jnp.where(eq, jnp.where(rank <= quota, thr, thr - 1), key) >= thr
        o_ref[:, lanes] = jnp.where(kept, 0.0, NEG_INF).astype(f32)
        return rank.max(axis=1, keepdims=True)

    lax.fori_loop(0, n // LANE, tile_body, jnp.zeros((rows, 1), f32), unroll=3)


def _select_sample(scores, past, t_new, k_sel):
    return pl.pallas_call(
        functools.partial(_select_sample_kernel, past=past, t_new=t_new, k_sel=k_sel), name="select_sample",
        out_shape=jax.ShapeDtypeStruct(scores.shape, f32),
        scratch_shapes=[pltpu.VMEM(scores.shape, i32)],
        compiler_params=pltpu.CompilerParams(vmem_limit_bytes=VMEM_LIMIT),
    )(scores)


def _attend_sample_a_kernel(pt_ref, q_ref, bias_ref, mask_ref, biast_ref, maskt_ref, gate_ref, *refs, pg, kv, scale):
    kpages, vpages = refs[:pg], refs[pg:2 * pg]
    kt_ref, vt_ref, o_ref, m_ref, l_ref, acc_ref = refs[2 * pg:]
    s_id = pl.program_id(1)
    rows = q_ref.shape[1] // kv

    @pl.when(s_id == 0)
    def _():
        m_ref[...] = jnp.full(m_ref.shape, M_INIT, f32)
        l_ref[...] = jnp.zeros(l_ref.shape, f32)
        acc_ref[...] = jnp.zeros(acc_ref.shape, f32)

    def head_rows(page_ref, kh):
        return page_ref[pl.ds(kh, PAGE_SIZE, stride=kv), :].astype(bf16)

    def update(k_refs, v_refs, bias, mask):
        s = jnp.concatenate(
            [jnp.concatenate([_nt_dot(q_ref[0, kh * rows:(kh + 1) * rows, :], head_rows(kr, kh)) for kr in k_refs],
                             axis=1) for kh in range(kv)], axis=0)
        s = s * scale + bias + jnp.concatenate([mask] * kv, axis=0)
        m_new, l_new, alpha, p = _softmax_step(s, m_ref[...], l_ref[...])
        p = p.astype(bf16)
        pv = []
        for kh in range(kv):
            pk = p[kh * rows:(kh + 1) * rows]
            pv.append(sum(jnp.dot(pk[:, r * PAGE_SIZE:(r + 1) * PAGE_SIZE], head_rows(vr, kh),
                                  preferred_element_type=f32) for r, vr in enumerate(v_refs)))
        acc_ref[...] = alpha * acc_ref[...] + jnp.concatenate(pv, axis=0)
        m_ref[...] = m_new
        l_ref[...] = l_new

    update(kpages, vpages, bias_ref[...], mask_ref[0])

    @pl.when(s_id == pl.num_programs(1) - 1)
    def _():
        update([kt_ref.at[0]], [vt_ref.at[0]], biast_ref[...], maskt_ref[0])
        o_ref[0] = acc_ref[...] / l_ref[...] * _silu(gate_ref[0])


def _attend_sample_a(page_table, q2, bias_tab, mask, cache_k, cache_v, layer, k_tail, v_tail, gate2):
    dbatch, qrows, _ = q2.shape
    kv = cache_k.shape[2] // PAGE_SIZE
    rows = qrows // kv
    n_pages = page_table.shape[1]
    past = n_pages * PAGE_SIZE
    pg = _pick_tile(n_pages, (8, 4, 2, 1))
    qspec = pl.BlockSpec((1, qrows, HEAD_DIM), lambda b, s, pt: (b, 0, 0))
    kspecs = [pl.BlockSpec((None, None, PAGE_SIZE * kv, HEAD_DIM),
                           lambda b, s, pt, r=r: (layer, pt[b, s * pg + r], 0, 0)) for r in range(pg)]
    tailspec = pl.BlockSpec((1, PAGE_SIZE * kv, HEAD_DIM), lambda b, s, pt: (b, 0, 0))
    return pl.pallas_call(
        functools.partial(_attend_sample_a_kernel, pg=pg, kv=kv, scale=HEAD_DIM ** -0.5), name="attend_sample_a",
        grid_spec=pltpu.PrefetchScalarGridSpec(
            num_scalar_prefetch=1, grid=(dbatch, n_pages // pg),
            in_specs=[qspec,
                      pl.BlockSpec((qrows, pg * PAGE_SIZE), lambda b, s, pt: (0, s)),
                      pl.BlockSpec((1, rows, pg * PAGE_SIZE), lambda b, s, pt: (b, 0, s)),
                      pl.BlockSpec((qrows, PAGE_SIZE), lambda b, s, pt: (0, past // PAGE_SIZE)),
                      pl.BlockSpec((1, rows, PAGE_SIZE), lambda b, s, pt: (b, 0, past // PAGE_SIZE)),
                      qspec] + kspecs + kspecs + [tailspec, tailspec],
            out_specs=qspec,
            scratch_shapes=[pltpu.VMEM((qrows, 1), f32), pltpu.VMEM((qrows, 1), f32),
                            pltpu.VMEM((qrows, HEAD_DIM), f32)]),
        out_shape=jax.ShapeDtypeStruct(q2.shape, f32),
        compiler_params=_params(2),
    )(page_table, q2, bias_tab, mask, bias_tab, mask, gate2,
      *([cache_k] * pg), *([cache_v] * pg), k_tail, v_tail)


def _attend_prompt_b_kernel(q0_ref, q1_ref, q2_ref, k_ref, v_ref, gate_ref, gb_ref, *rest, rates, scale, unroll, shift):
    if shift is None:
        o_ref, qf_ref, og_ref, lse_ref = rest
    else:
        bk_ref, bv_ref, nk_ref, nv_ref, o_ref, ok_ref, ov_ref, qf_ref, og_ref, lse_ref, *ring_refs = rest
        step = pl.program_id(0) * pl.num_programs(1) + pl.program_id(1)
        _shift_ring_step(step, pl.num_programs(0) * pl.num_programs(1), (bk_ref, bv_ref), (nk_ref, nv_ref),
                         (ok_ref, ov_ref), *ring_refs, **shift)
    seq = k_ref.shape[0]
    col = lax.broadcasted_iota(i32, (BLK, 2 * BLK), 1)

    for g, (q_ref, d) in enumerate(zip((q0_ref, q1_ref, q2_ref), rates)):
        qf_ref[...] = q_ref[...].astype(f32)
        nb = seq // d // BLK
        bias = gb_ref[0, :, g * 2 * BLK:(g + 1) * 2 * BLK]

        def blk_body(it, carry, g=g, d=d, nb=nb, bias=bias):
            r = it // nb
            blk = it % nb
            start = r + blk * (BLK * d)
            start_prev = r + jnp.maximum(blk - 1, 0) * (BLK * d)
            cur = pl.ds(start, BLK, stride=d)
            prev = pl.ds(start_prev, BLK, stride=d)
            qb = qf_ref[cur, :].astype(bf16)
            kcat = jnp.concatenate([k_ref[prev, :], k_ref[cur, :]], axis=0).astype(bf16)
            vcat = jnp.concatenate([v_ref[prev, :], v_ref[cur, :]], axis=0).astype(bf16)
            s = _nt_dot(qb, kcat) * scale + bias
            s = jnp.where((col >= BLK) | (blk > 0), s, NEG_INF)
            m = s.max(axis=1, keepdims=True)
            e = jnp.exp(s - m)
            ssum = e.sum(axis=1, keepdims=True)
            og_ref[g, cur, :] = jnp.dot(e.astype(bf16), vcat, preferred_element_type=f32) / ssum
            lse_ref[g, cur, :] = jnp.broadcast_to(m + jnp.log(ssum), (BLK, HEAD_DIM))
            return carry

        lax.fori_loop(0, d * nb, blk_body, 0, unroll=unroll)

    def merge_body(cb, carry):
        rows = pl.ds(pl.multiple_of(cb * BLK, BLK), BLK)
        lses = [lse_ref[g, rows, :] for g in range(len(rates))]
        top = functools.reduce(jnp.maximum, lses)
        wts = [jnp.exp(x - top) for x in lses]
        num = sum(w * og_ref[g, rows, :] for g, w in enumerate(wts))
        gt = gate_ref[rows, :]
        o_ref[rows, :] = (num / sum(wts) * _silu(gt)).astype(o_ref.dtype)
        return carry

    lax.fori_loop(0, seq // BLK, merge_body, 0)


DIL_BLOCK_UNROLL = 16


def _attend_prompt_b(q, k, v, gate, bias_tab, batch, seq, buffers=None):
    n_heads = k.shape[1] // HEAD_DIM
    n_dil = len(DIL_RATES)
    blk = (seq, HEAD_DIM)
    qspecs = [pl.BlockSpec(blk, lambda b, h, g=g: (b, g * n_heads + h)) for g in range(n_dil)]
    hspec = pl.BlockSpec(blk, lambda b, h: (b, h))
    in_specs = qspecs + [hspec, hspec, hspec, pl.BlockSpec((1, BLK, bias_tab.shape[2]), lambda b, h: (h, 0, 0))]
    out_specs, out_shape = [hspec], [jax.ShapeDtypeStruct(k.shape, bf16)]
    scratch = [pltpu.VMEM(blk, f32), pltpu.VMEM((n_dil,) + blk, f32), pltpu.VMEM((n_dil,) + blk, f32)]
    args, shift = [q, q, q, k, v, gate, bias_tab], None
    if buffers is not None:
        buf_k, buf_v, layer, new_k, new_v = buffers
        plan = _shift_plan(buf_k, new_k)
        if 2 * plan["dbatch"] * ((plan["rows"] - plan["shift"]) // plan["chunk"]) + SHIFT_RING <= batch * n_heads:
            shift = dict(plan, layer=layer)
            anyspec = pl.BlockSpec(memory_space=pl.ANY)
            out = jax.ShapeDtypeStruct((plan["dbatch"], plan["rows"], buf_k.shape[3]), buf_k.dtype)
            in_specs += [anyspec] * 4
            out_specs += [anyspec] * 2
            out_shape += [out, out]
            scratch += _shift_scratch(plan, buf_k)
            args += [buf_k, buf_v, new_k, new_v]
    res = pl.pallas_call(
        functools.partial(_attend_prompt_b_kernel, rates=DIL_RATES, scale=HEAD_DIM ** -0.5, unroll=DIL_BLOCK_UNROLL, shift=shift),
        name="attend_prompt_b", grid=(batch, n_heads),
        in_specs=in_specs, out_specs=out_specs, out_shape=out_shape, scratch_shapes=scratch,
        compiler_params=_params(2),
    )(*args)
    if shift is not None:
        return tuple(res)
    if buffers is not None:
        return (res[0],) + tuple(_shift_buffers(*buffers))
    return res[0], None, None


def _attend_sample_b_kernel(q_ref, k_ref, v_ref, kt_ref, vt_ref, bias_ref, gate_ref, o_ref, *, n_dil, t_new, scale):
    wb, hgs, _ = k_ref.shape
    k2 = k_ref.reshape(wb * hgs, HEAD_DIM)
    v2 = v_ref.reshape(wb * hgs, HEAD_DIM)
    for hl in range(hgs):
        cols = slice(hl * HEAD_DIM, (hl + 1) * HEAD_DIM)
        head = pl.ds(hl, wb, stride=hgs)
        q = q_ref[0, hl]
        s_main = _nt_dot(q, k2[head, :].astype(bf16)) * scale + bias_ref[hl, :, :wb]
        s_tail = _nt_dot(q, kt_ref[0, :, cols].astype(bf16)) * scale + bias_ref[hl, :, wb:]
        m = jnp.maximum(s_main.max(axis=1, keepdims=True), s_tail.max(axis=1, keepdims=True))
        e_main = jnp.exp(s_main - m)
        e_tail = jnp.exp(s_tail - m)
        ssum = e_main.sum(axis=1, keepdims=True) + e_tail.sum(axis=1, keepdims=True)
        o = (jnp.dot(e_main.astype(bf16), v2[head, :].astype(bf16), preferred_element_type=f32)
             + jnp.dot(e_tail.astype(bf16), vt_ref[0, :, cols].astype(bf16), preferred_element_type=f32)) / ssum
        lse = m + jnp.log(ssum)
        lses = [lse[g * t_new:(g + 1) * t_new] for g in range(n_dil)]
        top = functools.reduce(jnp.maximum, lses)
        wts = [jnp.exp(x - top) for x in lses]
        num = sum(w * o[g * t_new:(g + 1) * t_new] for g, w in enumerate(wts))
        o_ref[0, hl] = num / sum(wts) * _silu(gate_ref[0, hl])


def _attend_sample_b(q16, buf_k, buf_v, layer, k_tail, v_tail, bias_tab, gate4, t_new):
    dbatch, n_heads, rows, _ = q16.shape
    wb = buf_k.shape[2]
    hgs = 8 if n_heads % 8 == 0 else n_heads
    grouped = lambda a: a.reshape(a.shape[0], dbatch, wb, n_heads // hgs, hgs, HEAD_DIM)
    bufspec = pl.BlockSpec((None, None, wb, None, hgs, HEAD_DIM), lambda b, hg: (layer, b, 0, hg, 0, 0))
    tailspec = pl.BlockSpec((1, LANE, hgs * HEAD_DIM), lambda b, hg: (b, 0, hg))
    hspec = lambda r: pl.BlockSpec((1, hgs, r, HEAD_DIM), lambda b, hg: (b, hg, 0, 0))
    return pl.pallas_call(
        functools.partial(_attend_sample_b_kernel, n_dil=len(DIL_RATES), t_new=t_new, scale=HEAD_DIM ** -0.5),
        name="attend_sample_b", grid=(dbatch, n_heads // hgs),
        in_specs=[hspec(rows), bufspec, bufspec, tailspec, tailspec,
                  pl.BlockSpec((hgs, rows, wb + LANE), lambda b, hg: (hg, 0, 0)),
                  hspec(t_new)],
        out_specs=hspec(t_new),
        out_shape=jax.ShapeDtypeStruct((dbatch, n_heads, t_new, HEAD_DIM), f32),
        compiler_params=_params(2),
    )(q16, grouped(buf_k), grouped(buf_v), k_tail, v_tail, bias_tab, gate4)


SHIFT_RING = 4
SHIFT_LAG = 2
SHIFT_CHUNK_BYTES = 5 << 20


def _shift_kernel(bk_ref, bv_ref, nk_ref, nv_ref, ok_ref, ov_ref, ring, in_sem, out_sem, new_sem,
                  *, layer, dbatch, rows, shift, chunk):
    reads, writes, tails = [], [], []
    n_chunks = (rows - shift) // chunk
    for a, (src, new, dst) in enumerate(((bk_ref, nk_ref, ok_ref), (bv_ref, nv_ref, ov_ref))):
        for b in range(dbatch):
            for c in range(n_chunks):
                slot = len(reads) % SHIFT_RING
                reads.append(pltpu.make_async_copy(src.at[layer, b, pl.ds(shift + c * chunk, chunk)],
                                                   ring.at[slot], in_sem.at[slot]))
                writes.append(pltpu.make_async_copy(ring.at[slot], dst.at[b, pl.ds(c * chunk, chunk)],
                                                    out_sem.at[slot]))
            tails.append(pltpu.make_async_copy(new.at[b], dst.at[b, pl.ds(rows - shift, shift)], new_sem.at[a, b]))
    for t in tails:
        t.start()
    n = len(reads)
    for i in range(n + SHIFT_LAG):
        if i < n:
            if i >= SHIFT_RING:
                writes[i - SHIFT_RING].wait()
            reads[i].start()
        if 0 <= i - SHIFT_LAG < n:
            reads[i - SHIFT_LAG].wait()
            writes[i - SHIFT_LAG].start()
    for i in range(max(0, n - SHIFT_RING), n):
        writes[i].wait()
    for t in tails:
        t.wait()


def _shift_plan(buf, new):
    _, dbatch, rows, width = buf.shape
    shift = new.shape[1]
    body = (rows - shift) // shift
    per = max(u for u in range(1, body + 1)
              if body % u == 0 and u * shift * width * buf.dtype.itemsize <= SHIFT_CHUNK_BYTES)
    return dict(dbatch=dbatch, rows=rows, shift=shift, chunk=per * shift)


def _shift_scratch(plan, buf):
    return [pltpu.VMEM((SHIFT_RING, plan["chunk"], buf.shape[3]), buf.dtype),
            pltpu.SemaphoreType.DMA((SHIFT_RING,)), pltpu.SemaphoreType.DMA((SHIFT_RING,)),
            pltpu.SemaphoreType.DMA((2, plan["dbatch"]))]


def _shift_ring_step(t, n_steps, bufs, news, outs, ring, in_sem, out_sem, new_sem, *, layer, dbatch, rows, shift, chunk):
    per_b = (rows - shift) // chunk
    per_arr = dbatch * per_b
    n = len(bufs) * per_arr

    def for_chunk(i, read, action):
        slot = i % SHIFT_RING
        b, c = (i % per_arr) // per_b, i % per_b
        for a in range(len(bufs)):
            @pl.when(i // per_arr == a)
            def _(a=a):
                if read:
                    action(pltpu.make_async_copy(bufs[a].at[layer, b, pl.ds(shift + c * chunk, chunk)],
                                                 ring.at[slot], in_sem.at[slot]))
                else:
                    action(pltpu.make_async_copy(ring.at[slot], outs[a].at[b, pl.ds(c * chunk, chunk)],
                                                 out_sem.at[slot]))

    tails = [pltpu.make_async_copy(news[a].at[b], outs[a].at[b, pl.ds(rows - shift, shift)], new_sem.at[a, b])
             for a in range(len(bufs)) for b in range(dbatch)]

    @pl.when(t == 0)
    def _():
        for cp in tails:
            cp.start()

    @pl.when((t >= SHIFT_RING) & (t < n + SHIFT_RING))
    def _():
        for_chunk(t - SHIFT_RING, False, lambda cp: cp.wait())

    @pl.when(t < n)
    def _():
        for_chunk(t, True, lambda cp: cp.start())

    @pl.when((t >= SHIFT_LAG) & (t < n + SHIFT_LAG))
    def _():
        for_chunk(t - SHIFT_LAG, True, lambda cp: cp.wait())
        for_chunk(t - SHIFT_LAG, False, lambda cp: cp.start())

    @pl.when(t == n_steps - 1)
    def _():
        for cp in tails:
            cp.wait()


def _shift_buffers(buf_k, buf_v, layer, new_k, new_v):
    plan = _shift_plan(buf_k, new_k)
    anyspec = pl.BlockSpec(memory_space=pl.ANY)
    out = jax.ShapeDtypeStruct((plan["dbatch"], plan["rows"], buf_k.shape[3]), buf_k.dtype)
    return pl.pallas_call(
        functools.partial(_shift_kernel, layer=layer, **plan), name="shift_buffers",
        in_specs=[anyspec] * 4, out_specs=[anyspec] * 2, out_shape=[out, out],
        scratch_shapes=_shift_scratch(plan, buf_k),
        compiler_params=pltpu.CompilerParams(vmem_limit_bytes=VMEM_LIMIT),
    )(buf_k, buf_v, new_k, new_v)


def _offsets(sizes):
    return [int(x) for x in np.cumsum((0,) + tuple(sizes))[:-1]]


def _pad_rows(a, rows):
    return jnp.pad(a, ((0, 0), (0, rows - a.shape[1])) + ((0, 0),) * (a.ndim - 2))


def _layer_a(xp, xs, dims, j, last, cache_k, cache_v, cache_ki, page_table, w_in, w_out, ln_g, ln_b, rel_bias):
    batch, seq, dbatch, t_new = dims
    (xp, xpb), (xs, xsb) = xp, xs
    d_model = xp.shape[1]
    n_heads = d_model // HEAD_DIM
    kv, hi = KV_HEADS_A, IDX_HEADS
    group = n_heads // kv
    branch = n_heads * HEAD_DIM
    sizes = (branch, kv * HEAD_DIM, kv * HEAD_DIM, branch, hi * IDX_DIM, IDX_DIM, hi)
    offs = _offsets(sizes)
    tail_w = 2 * LANE
    w_t = w_in.shape[2] % LANE != 0
    w_use = jnp.swapaxes(w_in, 1, 2) if w_t else w_in
    tail_pad = ((0, 0), (0, tail_w - IDX_DIM - hi), (0, 0)) if w_t else ((0, 0), (0, 0), (0, tail_w - IDX_DIM - hi))
    w_tail = jnp.pad(w_use[j:j + 1, offs[5]:, :] if w_t else w_use[j:j + 1, :, offs[5]:], tail_pad)

    mm = lambda slot, dt, nm: _matmul(xpb, xsb, w_use, j, offs[slot], sizes[slot], dt, f"proj_a_{nm}", w_t)
    (q, q_s), (k_p, k_s), (v_p, v_s) = mm(0, bf16, "q"), mm(1, f32, "k"), mm(2, f32, "v")
    (gate, gate_s), (qi, qi_s) = mm(3, f32, "gate"), mm(4, bf16, "qi")
    kw, kw_s = _matmul(xpb, xsb, w_tail, 0, 0, tail_w, f32, "proj_a_kiwi", w_t)
    ki_p, wi = kw[:, :IDX_DIM], kw[:, IDX_DIM:IDX_DIM + hi]
    ki_s, wi_s = kw_s[:, :IDX_DIM], kw_s[:, IDX_DIM:IDX_DIM + hi]

    dist = (np.arange(ATTN_A_TQ)[:, None] + seq - ATTN_A_TQ) - np.arange(seq + ATTN_A_CHUNK - ATTN_A_TQ)[None, :]
    bias_p = _bias_table(rel_bias, _bucket_codes(dist))
    maskadd = _select_prompt(qi, ki_p, wi.T, batch, seq, min(TOPK_MAX, seq // 4))
    og = _attend_prompt(q, k_p, v_p, maskadd, bias_p, gate, batch, seq)

    q, gate, qi, wi = q_s, gate_s, qi_s, wi_s
    n_pages = page_table.shape[1]
    past = n_pages * PAGE_SIZE
    n_keys = past + t_new
    n_lanes = past + PAGE_SIZE
    cscale = (IDX_DIM ** -0.5) * (hi ** -0.5)
    qs = qi.reshape(dbatch, t_new * hi, IDX_DIM)
    wbc = jnp.broadcast_to((wi * cscale).reshape(dbatch, t_new * hi, 1), (dbatch, t_new * hi, LANE))
    ki_tail = _pad_rows(ki_s.reshape(dbatch, t_new, IDX_DIM), PAGE_SIZE)
    scores = _sample_scores(page_table, qs, wbc, cache_ki, j, ki_tail, t_new)
    mask_s = _select_sample(scores.reshape(dbatch * t_new, n_lanes), past, t_new, min(TOPK_MAX, n_keys // 4))

    def to_rows(a):
        a = a.reshape(dbatch, t_new, kv, group, HEAD_DIM)
        return a.transpose(0, 2, 3, 1, 4).reshape(dbatch, kv * group * t_new, HEAD_DIM)

    def to_page(a):
        return _pad_rows(a.reshape(dbatch, t_new, kv, HEAD_DIM), PAGE_SIZE).reshape(dbatch, PAGE_SIZE * kv, HEAD_DIM)

    rows = group * t_new
    dist = (past + np.arange(t_new)[:, None]) - np.arange(n_lanes)[None, :]
    bias_s = _bias_table(rel_bias, _bucket_codes(dist)).reshape(kv * rows, n_lanes)
    mask_g = jnp.broadcast_to(mask_s.reshape(dbatch, 1, t_new, n_lanes), (dbatch, group, t_new, n_lanes))
    mask_g = mask_g.reshape(dbatch, rows, n_lanes)
    pool = cache_k.shape[1]
    paged = lambda c: c.reshape(c.shape[0], pool, PAGE_SIZE * kv, HEAD_DIM)
    og2 = _attend_sample_a(page_table, to_rows(q), bias_s, mask_g, paged(cache_k), paged(cache_v), j,
                           to_page(k_s), to_page(v_s), to_rows(gate))
    og_s = og2.reshape(dbatch, kv, group, t_new, HEAD_DIM).transpose(0, 3, 1, 2, 4)
    og_s = og_s.reshape(dbatch * t_new, branch).astype(bf16)
    z, z_s = _matmul(og, og_s, w_out, j, 0, d_model, f32, "out_a", residual=(xp, xs))
    xp_new = _layernorm(z, ln_g[j], ln_b[j], not last)
    xs_new = _layernorm(z_s, ln_g[j], ln_b[j], not last)

    shp = lambda a, b_, t_, *rest: a.reshape(b_, t_, *rest)
    outs = (shp(k_p, batch, seq, kv, HEAD_DIM), shp(v_p, batch, seq, kv, HEAD_DIM), shp(ki_p, batch, seq, IDX_DIM),
            shp(k_s, dbatch, t_new, kv, HEAD_DIM), shp(v_s, dbatch, t_new, kv, HEAD_DIM),
            shp(ki_s, dbatch, t_new, IDX_DIM))
    return xp_new, xs_new, outs


def _layer_b(xp, xs, dims, j, last, buf_k, buf_v, w_in, w_out, ln_g, ln_b, rel_bias):
    batch, seq, dbatch, t_new = dims
    (xp, xpb), (xs, xsb) = xp, xs
    d_model = xp.shape[1]
    n_heads = d_model // HEAD_DIM
    branch = n_heads * HEAD_DIM
    n_dil = len(DIL_RATES)
    sizes = (n_dil * branch, branch, branch, branch)
    offs = _offsets(sizes)

    mm = lambda slot, dt, nm: _matmul(xpb, xsb, w_in, j, offs[slot], sizes[slot], dt, f"proj_b_{nm}")
    (q, q_s), (gate, gate_s) = mm(0, bf16, "q"), mm(3, f32, "gate")
    k_p, k_s, k_heads = _matmul_heads(xpb, xsb, w_in, j, offs[1], n_heads, "proj_b_k")
    v_p, v_s, v_heads = _matmul_heads(xpb, xsb, w_in, j, offs[2], n_heads, "proj_b_v")

    m =(np.arange(BLK)[:, None] + BLK) - np.arange(2 * BLK)[None, :]
    codes = [_bucket_codes(d * np.clip(m, 0, w // d), (m >= 0) & (m <= w // d))
             for d, w in zip(DIL_RATES, DIL_WINDOWS)]
    bias_p = _bias_table(rel_bias, np.concatenate(codes, axis=1))
    wb = buf_k.shape[2]
    flat = lambda buf: buf.reshape(buf.shape[0], dbatch, wb * n_heads, HEAD_DIM)
    og, new_k, new_v = _attend_prompt_b(q, k_p, v_p, gate, bias_p, batch, seq,
                                        (flat(buf_k), flat(buf_v), j, k_s.reshape(dbatch, t_new * n_heads, HEAD_DIM),
                                         v_s.reshape(dbatch, t_new * n_heads, HEAD_DIM)))

    q, gate = q_s, gate_s
    rows = -(-(n_dil * t_new) // 16) * 16
    pos = np.arange(wb + LANE)[None, :]
    codes = []
    for d, w in zip(DIL_RATES, DIL_WINDOWS):
        dist = wb + np.arange(t_new)[:, None] - pos
        codes.append(_bucket_codes(dist, (dist >= 0) & (dist % d == 0) & (dist // d <= w // d) & (pos < wb + t_new)))
    codes.append(np.full((rows - n_dil * t_new, wb + LANE), BUCKET_ZERO, np.int32))
    bias_s = _bias_table(rel_bias, np.concatenate(codes, axis=0))
    q16 = q.reshape(dbatch, t_new, n_dil, n_heads, HEAD_DIM).transpose(0, 3, 2, 1, 4)
    q16 = q16.reshape(dbatch, n_heads, n_dil * t_new, HEAD_DIM)
    q16 = jnp.pad(q16, ((0, 0), (0, 0), (0, rows - n_dil * t_new), (0, 0)))
    gate4 = gate.reshape(dbatch, t_new, n_heads, HEAD_DIM).transpose(0, 2, 1, 3)
    k_tail = _pad_rows(k_s.reshape(dbatch, t_new, branch), LANE)
    v_tail = _pad_rows(v_s.reshape(dbatch, t_new, branch), LANE)
    og4 = _attend_sample_b(q16, buf_k, buf_v, j, k_tail, v_tail, bias_s, gate4, t_new)
    og_s = og4.transpose(0, 2, 1, 3).reshape(dbatch * t_new, branch).astype(bf16)
    z, z_s = _matmul(og, og_s, w_out, j, 0, d_model, f32, "out_b", residual=(xp, xs))
    xp_new = _layernorm(z, ln_g[j], ln_b[j], not last)
    xs_new = _layernorm(z_s, ln_g[j], ln_b[j], not last)

    wp = min(W_MAX, seq)
    k_win = k_heads.reshape(batch, seq, n_heads, HEAD_DIM)[:, seq - wp:]
    v_win = v_heads.reshape(batch, seq, n_heads, HEAD_DIM)[:, seq - wp:]
    return xp_new, xs_new, (k_win, v_win, new_k.reshape(dbatch, wb, n_heads, HEAD_DIM),
                            new_v.reshape(dbatch, wb, n_heads, HEAD_DIM))


def kernel(x_prompt, x_sample, cache_k_a, cache_v_a, cache_kidx_a, cache_k_b, cache_v_b, page_table, rel_bias,
           w_in_a, w_out_a, ln_g_a, ln_b_a, w_in_b, w_out_b, ln_g_b, ln_b_b):
    batch, seq, d_model = x_prompt.shape
    dbatch, t_new, _ = x_sample.shape
    dims = (batch, seq, dbatch, t_new)
    with_bf16 = lambda a: (a, a.astype(bf16))
    xp = with_bf16(x_prompt.reshape(batch * seq, d_model))
    xs = with_bf16(x_sample.reshape(dbatch * t_new, d_model))
    outs_a, outs_b = [], []
    for layer in range(DEPTH):
        j = layer // 2
        if layer % 2 == 0:
            xp, xs, o = _layer_a(xp, xs, dims, j, layer == DEPTH - 1, cache_k_a, cache_v_a, cache_kidx_a, page_table,
                                 w_in_a, w_out_a, ln_g_a, ln_b_a, rel_bias)
            outs_a.append(o)
        else:
            xp, xs, o = _layer_b(xp, xs, dims, j, layer == DEPTH - 1, cache_k_b, cache_v_b,
                                 w_in_b, w_out_b, ln_g_b, ln_b_b, rel_bias)
            outs_b.append(o)
    stack = lambda group, idx: group[0][idx][None] if len(group) == 1 else jnp.stack([o[idx] for o in group])
    return (xp[0].reshape(batch, seq, d_model), xs[0].reshape(dbatch, t_new, d_model),
            *(stack(outs_a, n) for n in range(6)), *(stack(outs_b, n) for n in range(4)))
```

```python
import functools
import math

import jax
import jax.numpy as jnp
import numpy as np
from jax import lax
from jax.experimental import pallas as pl
from jax.experimental.pallas import tpu as pltpu

HEAD_DIM = 128
KV_HEADS_A = 8
IDX_HEADS = 64
IDX_DIM = 128
TOPK_MAX = 256
DIL_WINDOWS = (128, 512, 2048)
DIL_RATES = (1, 4, 16)
W_MAX = max(DIL_WINDOWS)
BLK = 128
N_BUCKETS = 32
REL_MAX_DIST = 2048
DEPTH = 2
ALPHA = (2 * DEPTH) ** 0.25
LN_EPS = 1e-5
PAGE_SIZE = 128

LANE = 128
NEG_INF = float("-inf")
INT_MIN = -(2 ** 31)
M_INIT = -1e30
BUCKET_ZERO = -1
BUCKET_MASKED = -2
VMEM_LIMIT = 56 * 1024 * 1024

f32 = jnp.float32
bf16 = jnp.bfloat16
i32 = jnp.int32


def _params(n_axes, vmem=VMEM_LIMIT):
    return pltpu.CompilerParams(dimension_semantics=("arbitrary",) * n_axes, vmem_limit_bytes=vmem)


def _nt_dot(a, b):
    return lax.dot_general(a, b, (((1,), (1,)), ((), ())), preferred_element_type=f32)


def _silu(x):
    return x * (1.0 / (1.0 + jnp.exp(-x)))


def _softmax_step(s, m_old, l_old):
    m_new = jnp.maximum(m_old, s.max(axis=1, keepdims=True))
    alpha = jnp.exp(m_old - m_new)
    p = jnp.exp(s - m_new)
    return m_new, alpha * l_old + p.sum(axis=1, keepdims=True), alpha, p


def _mm_kernel(x_ref, xs_ref, w_ref, *rest, chunk, transposed, residual):
    if residual:
        r_ref, rs_ref, o_ref, os_ref, wbf_ref = rest
    else:
        o_ref, os_ref, wbf_ref = rest
    mul = _nt_dot if transposed else functools.partial(jnp.dot, preferred_element_type=f32)

    @pl.when(pl.program_id(1) == 0)
    def _():
        def body(r, c):
            rows = pl.ds(pl.multiple_of(r * chunk, chunk), chunk)
            wbf_ref[rows, :] = w_ref[rows, :].astype(bf16)
            return c
        lax.fori_loop(0, w_ref.shape[0] // chunk, body, 0)
        ys = mul(xs_ref[...], wbf_ref[...])
        os_ref[...] = (ALPHA * rs_ref[...] + ys if residual else ys).astype(os_ref.dtype)

    y = mul(x_ref[...], wbf_ref[...])
    o_ref[...] = (ALPHA * r_ref[...] + y if residual else y).astype(o_ref.dtype)


def _pick_tile(n, candidates):
    for c in candidates:
        if n % c == 0:
            return c
    raise ValueError(f"no tile for {n}")


def _matmul(x_bf, xs_bf, w, layer, col_off, ncols, out_dtype, name, transposed=False, residual=None):
    m, k = x_bf.shape
    ms = xs_bf.shape[0]
    tn = _pick_tile(math.gcd(col_off, ncols) if col_off else ncols, (512, 256, 128))
    tm = _pick_tile(m, (1024, 512, 256, 128))
    off_blocks = col_off // tn
    if transposed:
        wblock, wspec, chunk = (tn, k), pl.BlockSpec((None, tn, k), lambda n, i: (layer, n + off_blocks, 0)), LANE
    else:
        wblock, wspec = (k, tn), pl.BlockSpec((None, k, tn), lambda n, i: (layer, 0, n + off_blocks))
        chunk = _pick_tile(k, (256, 128))
    ospecs = [pl.BlockSpec((tm, tn), lambda n, i: (i, n)), pl.BlockSpec((ms, tn), lambda n, i: (0, n))]
    return pl.pallas_call(
        functools.partial(_mm_kernel, chunk=chunk, transposed=transposed, residual=residual is not None), name=name,
        grid=(ncols // tn, m // tm),
        in_specs=[pl.BlockSpec((tm, k), lambda n, i: (i, 0)),
                  pl.BlockSpec((ms, k), lambda n, i: (0, 0)),
                  wspec] + (ospecs if residual is not None else []),
        out_specs=ospecs,
        out_shape=[jax.ShapeDtypeStruct((m, ncols), out_dtype), jax.ShapeDtypeStruct((ms, ncols), out_dtype)],
        scratch_shapes=[pltpu.VMEM(wblock, bf16)],
        compiler_params=_params(2),
    )(x_bf, xs_bf, w, *(residual or ()))


def _mm_heads_kernel(x_ref, xs_ref, w_ref, o_ref, os_ref, oh_ref, wbf_ref, *, k_chunk):
    @pl.when(pl.program_id(1) == 0)
    def _():
        def body(r, c):
            rows = pl.ds(pl.multiple_of(r * k_chunk, k_chunk), k_chunk)
            wbf_ref[rows, :] = w_ref[rows, :].astype(bf16)
            return c
        lax.fori_loop(0, w_ref.shape[0] // k_chunk, body, 0)
        os_ref[...] = jnp.dot(xs_ref[...], wbf_ref[...], preferred_element_type=f32)

    res = jnp.dot(x_ref[...], wbf_ref[...], preferred_element_type=f32)
    o_ref[...] = res
    tm, nh, dh = oh_ref.shape
    flat = oh_ref.reshape(tm * nh, dh)
    for hh in range(nh):
        flat[pl.ds(hh, tm, stride=nh), :] = res[:, hh * dh:(hh + 1) * dh]


def _matmul_heads(x_bf, xs_bf, w, layer, col_off, n_heads, name):
    m, k = x_bf.shape
    ms = xs_bf.shape[0]
    nh = 8 if n_heads % 8 == 0 else n_heads
    tn = nh * HEAD_DIM
    ncols = n_heads * HEAD_DIM
    tm = _pick_tile(m, (512, 256, 128))
    off_blocks = col_off // tn
    k_chunk = _pick_tile(k, (256, 128))
    return pl.pallas_call(
        functools.partial(_mm_heads_kernel, k_chunk=k_chunk), name=name,
        grid=(ncols // tn, m // tm),
        in_specs=[pl.BlockSpec((tm, k), lambda n, i: (i, 0)),
                  pl.BlockSpec((ms, k), lambda n, i: (0, 0)),
                  pl.BlockSpec((None, k, tn), lambda n, i: (layer, 0, n + off_blocks),
                               pipeline_mode=pl.Buffered(1))],
        out_specs=[pl.BlockSpec((tm, tn), lambda n, i: (i, n)),
                   pl.BlockSpec((ms, tn), lambda n, i: (0, n)),
                   pl.BlockSpec((tm, nh, HEAD_DIM), lambda n, i: (i, n, 0))],
        out_shape=[jax.ShapeDtypeStruct((m, ncols), f32), jax.ShapeDtypeStruct((ms, ncols), f32),
                   jax.ShapeDtypeStruct((m, n_heads, HEAD_DIM), f32)],
        scratch_shapes=[pltpu.VMEM((k, tn), bf16)],
        compiler_params=_params(2),
    )(x_bf, xs_bf, w)


def _ln_kernel(z_ref, g_ref, b_ref, o_ref, *obf_ref):
    z = z_ref[...]
    mu = jnp.mean(z, axis=-1, keepdims=True)
    zc = z - mu
    var = jnp.mean(zc * zc, axis=-1, keepdims=True)
    out = zc * lax.rsqrt(var + LN_EPS) * g_ref[...] + b_ref[...]
    o_ref[...] = out
    for ref in obf_ref:
        ref[...] = out.astype(bf16)


def _layernorm(z, g, b, with_bf16):
    m, d = z.shape
    tm = _pick_tile(m, (128, 64, 32, 16, 8))
    row = pl.BlockSpec((tm, d), lambda i: (i, 0))
    vec = pl.BlockSpec((1, d), lambda i: (0, 0))
    n_out = 2 if with_bf16 else 1
    res = pl.pallas_call(
        _ln_kernel, grid=(m // tm,), name="layernorm",
        in_specs=[row, vec, vec], out_specs=[row] * n_out,
        out_shape=[jax.ShapeDtypeStruct((m, d), f32), jax.ShapeDtypeStruct((m, d), bf16)][:n_out],
        compiler_params=_params(1),
    )(z, g.reshape(1, d), b.reshape(1, d))
    return (res[0], res[1] if with_bf16 else None)


def _rel_bucket_np(n):
    max_exact = N_BUCKETS // 2
    nf = np.maximum(n, 1).astype(np.float64)
    large = max_exact + (np.log(nf / max_exact) / math.log(REL_MAX_DIST / max_exact)
                         * (N_BUCKETS - max_exact)).astype(np.int64)
    large = np.minimum(large, N_BUCKETS - 1)
    return np.where(n < max_exact, n, large).astype(np.int32)


def _bucket_codes(dist, valid=None):
    code = np.where(dist >= 0, _rel_bucket_np(np.maximum(dist, 0)), BUCKET_ZERO)
    if valid is not None:
        code = np.where(valid, code, BUCKET_MASKED)
    return code.astype(np.int32)


def _bias_kernel(tbl_ref, bm_ref, o_ref, *, cw):
    h = pl.program_id(0)
    row = tbl_ref[pl.ds(h, 1), :]

    def body(c, carry):
        for t in range(cw // LANE):
            cols = pl.ds(pl.multiple_of(c * cw + t * LANE, LANE), LANE)
            code = bm_ref[:, cols]
            val = jnp.take_along_axis(jnp.broadcast_to(row, code.shape), jnp.maximum(code, 0), axis=1)
            o_ref[0, :, cols] = jnp.where(code >= 0, val, jnp.where(code == BUCKET_MASKED, NEG_INF, 0.0))
        return carry

    lax.fori_loop(0, bm_ref.shape[1] // cw, body, 0, unroll=8)


def _bias_table(rel_bias, codes):
    r, c = codes.shape
    n_heads = rel_bias.shape[1]
    units = c // LANE
    per = max(1, 16 // -(-r // 8))
    cw = LANE * max(u for u in range(1, units + 1) if units % u == 0 and u <= per)
    return pl.pallas_call(
        functools.partial(_bias_kernel, cw=cw), name="bias_table",
        grid=(n_heads,),
        in_specs=[pl.BlockSpec((n_heads, LANE), lambda h: (0, 0)),
                  pl.BlockSpec((r, c), lambda h: (0, 0))],
        out_specs=pl.BlockSpec((1, r, c), lambda h: (h, 0, 0)),
        out_shape=jax.ShapeDtypeStruct((n_heads, r, c), f32),
        compiler_params=_params(1),
    )(jnp.pad(rel_bias.T, ((0, 0), (0, LANE - N_BUCKETS))), jnp.asarray(codes))


def _sortable_key(x):
    bits = pltpu.bitcast(x, i32)
    return bits ^ ((bits >> 31) & 0x7FFFFFFF)


def _kth_largest_key(count_ge, shape, k_sel):
    def body(it, t):
        cand = t + lax.shift_left(jnp.int32(1), 31 - it)
        return jnp.where(count_ge(cand) >= k_sel, cand, t)
    return lax.fori_loop(0, 32, body, jnp.full(shape, INT_MIN, i32))


def _select_prompt_kernel(qi_ref, ki_ref, wt_ref, m_ref, kibf_ref, key_ref, *, n_idx_heads, k_sel, ck, hb, cscale):
    i = pl.program_id(1)
    tq = qi_ref.shape[0]

    @pl.when(i == 0)
    def _():
        kibf_ref[...] = ki_ref[...].astype(bf16)

    nck = ((i + 1) * tq + ck - 1) // ck
    t_row = i * tq + lax.broadcasted_iota(i32, (1, tq), 1)

    def chunk_body(c, carry):
        koff = pl.multiple_of(c * ck, ck)
        kc = kibf_ref[pl.ds(koff, ck), :]

        def heads_body(hg, acc):
            hoff = hg * (hb * IDX_DIM)
            qstack = jnp.concatenate(
                [qi_ref[:, pl.ds(pl.multiple_of(hoff + j * IDX_DIM, IDX_DIM), IDX_DIM)] for j in range(hb)], axis=0)
            s = _nt_dot(kc, qstack)
            w = wt_ref[pl.ds(pl.multiple_of(hg * hb, hb), hb), :] * cscale
            for j in range(hb):
                acc = acc + jnp.maximum(s[:, j * tq:(j + 1) * tq], 0.0) * w[j:j + 1, :]
            return acc

        acc = lax.fori_loop(0, n_idx_heads // hb, heads_body, jnp.zeros((ck, tq), f32), unroll=2)
        l_col = koff + lax.broadcasted_iota(i32, (ck, 1), 0)
        key_ref[pl.ds(koff, ck), :] = jnp.where(l_col <= t_row, _sortable_key(acc), INT_MIN)
        return carry

    lax.fori_loop(0, nck, chunk_body, 0)

    def count_ge(cand):
        def body(c, cnt):
            kc = key_ref[pl.ds(pl.multiple_of(c * ck, ck), ck), :]
            hit = jnp.where(kc >= cand, 1, 0).astype(i32)
            return cnt + hit.reshape(ck // 8, 8, tq).sum(axis=0)
        cnt = lax.fori_loop(0, nck, body, jnp.zeros((8, tq), i32))
        return cnt.sum(axis=0, keepdims=True)

    thr = _kth_largest_key(count_ge, (1, tq), k_sel)
    thr = jnp.maximum(thr, INT_MIN + 1)
    n_ge = count_ge(thr)
    ties = jnp.max(n_ge) > k_sel

    m_ref[...] = jnp.full(m_ref.shape, NEG_INF, m_ref.dtype)

    def write_chunk(koff, sel):
        m_ref[:, pl.ds(koff, ck)] = jnp.where(sel, 0.0, NEG_INF).astype(f32).T.astype(m_ref.dtype)

    @pl.when(jnp.logical_not(ties))
    def _():
        def out_body(c, carry):
            koff = pl.multiple_of(c * ck, ck)
            write_chunk(koff, key_ref[pl.ds(koff, ck), :] >= thr)
            return carry
        lax.fori_loop(0, nck, out_body, 0)

    @pl.when(ties)
    def _():
        quota = (k_sel - count_ge(thr + 1)).astype(f32)
        tri = (lax.broadcasted_iota(i32, (ck, ck), 0) >= lax.broadcasted_iota(i32, (ck, ck), 1)).astype(bf16)

        def out_body(c, seen):
            koff = pl.multiple_of(c * ck, ck)
            key = key_ref[pl.ds(koff, ck), :]
            eq = key == thr
            rank = jnp.dot(tri, jnp.where(eq, 1.0, 0.0).astype(bf16), preferred_element_type=f32) + seen
            write_chunk(koff, jnp.where(eq, jnp.where(rank <= quota, thr, thr - 1), key) >= thr)
            return rank[ck - 1:ck, :]
        lax.fori_loop(0, nck, out_body, jnp.zeros((1, tq), f32))


SELECT_KEY_CHUNK = 256
SELECT_HEADS_PER_STEP = 16


def _select_prompt(qi, ki, wi_t, batch, seq, k_sel):
    n_idx_heads = wi_t.shape[0]
    tq, ck = BLK, SELECT_KEY_CHUNK
    hb = _pick_tile(n_idx_heads, tuple(SELECT_HEADS_PER_STEP >> s for s in range(SELECT_HEADS_PER_STEP.bit_length())))
    nblk = seq // tq
    cscale = (IDX_DIM ** -0.5) * (n_idx_heads ** -0.5)
    return pl.pallas_call(
        functools.partial(_select_prompt_kernel, n_idx_heads=n_idx_heads, k_sel=k_sel, ck=ck, hb=hb, cscale=cscale),
        name="select_prompt", grid=(batch, nblk),
        in_specs=[pl.BlockSpec((tq, n_idx_heads * IDX_DIM), lambda b, i: (b * nblk + i, 0)),
                  pl.BlockSpec((seq, IDX_DIM), lambda b, i: (b, 0)),
                  pl.BlockSpec((n_idx_heads, tq), lambda b, i: (0, b * nblk + i))],
        out_specs=pl.BlockSpec((tq, seq), lambda b, i: (b * nblk + i, 0)),
        out_shape=jax.ShapeDtypeStruct((batch * seq, seq), bf16),
        scratch_shapes=[pltpu.VMEM((seq, IDX_DIM), bf16), pltpu.VMEM((seq, tq), i32)],
        compiler_params=_params(2),
    )(qi, ki, wi_t)


def _attend_prompt_kernel(q_ref, k_ref, v_ref, mk_ref, g_ref, gate_ref, o_ref,
                          kbf_ref, vbf_ref, s_ref, mp_ref, l_ref, acc_ref, *, group, ck, sub, scale):
    i = pl.program_id(2)
    nblk = pl.num_programs(2)
    tq = q_ref.shape[0]
    rows = group * tq

    @pl.when(i == 0)
    def _():
        kbf_ref[...] = k_ref[...].astype(bf16)
        vbf_ref[...] = v_ref[...].astype(bf16)

    nck = ((i + 1) * tq + ck - 1) // ck
    goff0 = (nblk - 1 - i) * tq
    qs = jnp.concatenate([q_ref[:, g * HEAD_DIM:(g + 1) * HEAD_DIM] for g in range(group)], axis=0)

    mp_ref[...] = jnp.full(mp_ref.shape, M_INIT, f32)

    def logits_body(c, carry):
        for u in range(ck // sub):
            koff = pl.multiple_of(c * ck + u * sub, sub)
            goff = pl.multiple_of(goff0 + koff, LANE)
            s = _nt_dot(qs, kbf_ref[pl.ds(koff, sub), :]) * scale
            bias = jnp.concatenate([g_ref[g, :, pl.ds(goff, sub)] for g in range(group)], axis=0)
            sel = mk_ref[:, pl.ds(koff, sub)].astype(f32)
            s = s + bias + jnp.concatenate([sel] * group, axis=0)
            s_ref[:, pl.ds(koff, sub)] = s
            mp_ref[...] = functools.reduce(jnp.maximum, [mp_ref[...]] + [s[:, t * LANE:(t + 1) * LANE]
                                                                         for t in range(sub // LANE)])
        return carry

    lax.fori_loop(0, nck, logits_body, 0)
    m_all = jnp.broadcast_to(mp_ref[...].max(axis=1, keepdims=True), (rows, LANE))

    l_ref[...] = jnp.zeros(l_ref.shape, f32)
    acc_ref[...] = jnp.zeros(acc_ref.shape, f32)

    def pv_body(c, carry):
        for u in range(ck // sub):
            koff = pl.multiple_of(c * ck + u * sub, sub)
            ps = [jnp.exp(s_ref[:, pl.ds(pl.multiple_of(koff + t * LANE, LANE), LANE)] - m_all)
                  for t in range(sub // LANE)]
            l_ref[...] += functools.reduce(jnp.add, ps)
            p = jnp.concatenate(ps, axis=1).astype(bf16)
            acc_ref[...] += jnp.dot(p, vbf_ref[pl.ds(koff, sub), :], preferred_element_type=f32)
        return carry

    lax.fori_loop(0, nck, pv_body, 0)
    o = acc_ref[...] / l_ref[...].sum(axis=1, keepdims=True)
    for g in range(group):
        cols = slice(g * HEAD_DIM, (g + 1) * HEAD_DIM)
        o_ref[:, cols] = (o[g * tq:(g + 1) * tq] * _silu(gate_ref[:, cols])).astype(o_ref.dtype)


ATTN_A_CHUNK = 512
ATTN_A_SUB = 512
ATTN_A_TQ = 256


def _attend_prompt(q, k, v, maskadd, bias_tab, gate, batch, seq):
    kv = k.shape[1] // HEAD_DIM
    group = q.shape[1] // (kv * HEAD_DIM)
    tq, ck = ATTN_A_TQ, ATTN_A_CHUNK
    nblk = seq // tq
    gw = group * HEAD_DIM
    qspec = pl.BlockSpec((tq, gw), lambda kh, b, i: (b * nblk + i, kh))
    kvspec = pl.BlockSpec((seq, HEAD_DIM), lambda kh, b, i: (b, kh))
    return pl.pallas_call(
        functools.partial(_attend_prompt_kernel, group=group, ck=ck, sub=ATTN_A_SUB, scale=HEAD_DIM ** -0.5),
        name="attend_prompt_a", grid=(kv, batch, nblk),
        in_specs=[qspec, kvspec, kvspec,
                  pl.BlockSpec((tq, seq), lambda kh, b, i: (b * nblk + i, 0)),
                  pl.BlockSpec((group, tq, bias_tab.shape[2]), lambda kh, b, i: (kh, 0, 0)),
                  qspec],
        out_specs=qspec,
        out_shape=jax.ShapeDtypeStruct(q.shape, bf16),
        scratch_shapes=[pltpu.VMEM((seq, HEAD_DIM), bf16), pltpu.VMEM((seq, HEAD_DIM), bf16),
                        pltpu.VMEM((group * tq, seq), f32), pltpu.VMEM((group * tq, LANE), f32),
                        pltpu.VMEM((group * tq, LANE), f32), pltpu.VMEM((group * tq, HEAD_DIM), f32)],
        compiler_params=_params(3),
    )(q, k, v, maskadd, bias_tab, gate)


def _score_page(qs, wb, kpage, t_new):
    s = _nt_dot(qs, kpage.astype(bf16))
    x = jnp.maximum(s, 0.0) * wb
    return x.reshape(t_new, x.shape[0] // t_new, x.shape[1]).sum(axis=1)


def _sample_scores_kernel(pt_ref, qs_ref, wb_ref, *refs, pg, t_new):
    o_ref = refs[pg]
    for r in range(pg):
        o_ref[0, :, r * PAGE_SIZE:(r + 1) * PAGE_SIZE] = _score_page(qs_ref[0], wb_ref[0], refs[r][...], t_new)


def _sample_scores_tail_kernel(qs_ref, wb_ref, kt_ref, o_ref, *, t_new):
    o_ref[0] = _score_page(qs_ref[0], wb_ref[0], kt_ref[0], t_new)


def _sample_scores(page_table, qs, wb, cache_ki, layer, ki_tail, t_new):
    dbatch, rows, _ = qs.shape
    n_pages = page_table.shape[1]
    pg = _pick_tile(n_pages, (16, 8, 4, 2, 1))
    qspec = pl.BlockSpec((1, rows, IDX_DIM), lambda b, s, pt: (b, 0, 0))
    page_specs = [pl.BlockSpec((None, None, PAGE_SIZE, IDX_DIM),
                               lambda b, s, pt, r=r: (layer, pt[b, s * pg + r], 0, 0)) for r in range(pg)]
    main = pl.pallas_call(
        functools.partial(_sample_scores_kernel, pg=pg, t_new=t_new), name="sample_scores",
        grid_spec=pltpu.PrefetchScalarGridSpec(
            num_scalar_prefetch=1, grid=(dbatch, n_pages // pg),
            in_specs=[qspec, qspec] + page_specs,
            out_specs=pl.BlockSpec((1, t_new, pg * PAGE_SIZE), lambda b, s, pt: (b, 0, s))),
        out_shape=jax.ShapeDtypeStruct((dbatch, t_new, n_pages * PAGE_SIZE), f32),
        compiler_params=_params(2),
    )(page_table, qs, wb, *([cache_ki] * pg))
    spec3 = lambda d1, d2: pl.BlockSpec((1, d1, d2), lambda b: (b, 0, 0))
    tail = pl.pallas_call(
        functools.partial(_sample_scores_tail_kernel, t_new=t_new), name="sample_scores_tail",
        grid=(dbatch,),
        in_specs=[spec3(rows, IDX_DIM), spec3(rows, IDX_DIM), spec3(PAGE_SIZE, IDX_DIM)],
        out_specs=spec3(t_new, PAGE_SIZE),
        out_shape=jax.ShapeDtypeStruct((dbatch, t_new, PAGE_SIZE), f32),
        compiler_params=_params(1),
    )(qs, wb, ki_tail)
    return jnp.concatenate([main, tail], axis=2)


def _select_sample_kernel(s_ref, o_ref, key_ref, *, past, t_new, k_sel):
    rows, n = s_ref.shape
    t = past + lax.broadcasted_iota(i32, (rows, 1), 0) % t_new
    l = lax.broadcasted_iota(i32, (1, n), 1)
    key_ref[...] = jnp.where(l <= t, _sortable_key(s_ref[...]), INT_MIN)

    def count_ge(cand):
        return jnp.where(key_ref[...] >= cand, 1, 0).astype(i32).sum(axis=1, keepdims=True)

    thr = jnp.maximum(_kth_largest_key(count_ge, (rows, 1), k_sel), INT_MIN + 1)
    quota = (k_sel - count_ge(thr + 1)).astype(f32)
    tri = (lax.broadcasted_iota(i32, (LANE, LANE), 0) <= lax.broadcasted_iota(i32, (LANE, LANE), 1)).astype(bf16)

    def tile_body(c, seen):
        lanes = pl.ds(pl.multiple_of(c * LANE, LANE), LANE)
        key = key_ref[:, lanes]
        eq = key == thr
        rank = jnp.dot(jnp.where(eq, 1.0, 0.0).astype(bf16), tri, preferred_element_type=f32) + seen
        kept = jnp.where(eq, jnp.where(rank <= quota, thr, thr - 1), key) >= thr
        o_ref[:, lanes] = jnp.where(kept, 0.0, NEG_INF).astype(f32)
        return rank.max(axis=1, keepdims=True)

    lax.fori_loop(0, n // LANE, tile_body, jnp.zeros((rows, 1), f32), unroll=3)


def _select_sample(scores, past, t_new, k_sel):
    return pl.pallas_call(
        functools.partial(_select_sample_kernel, past=past, t_new=t_new, k_sel=k_sel), name="select_sample",
        out_shape=jax.ShapeDtypeStruct(scores.shape, f32),
        scratch_shapes=[pltpu.VMEM(scores.shape, i32)],
        compiler_params=pltpu.CompilerParams(vmem_limit_bytes=VMEM_LIMIT),
    )(scores)


def _attend_sample_a_kernel(pt_ref, q_ref, bias_ref, mask_ref, biast_ref, maskt_ref, gate_ref, *refs, pg, kv, scale):
    kpages, vpages = refs[:pg], refs[pg:2 * pg]
    kt_ref, vt_ref, o_ref, m_ref, l_ref, acc_ref = refs[2 * pg:]
    s_id = pl.program_id(1)
    rows = q_ref.shape[1] // kv

    @pl.when(s_id == 0)
    def _():
        m_ref[...] = jnp.full(m_ref.shape, M_INIT, f32)
        l_ref[...] = jnp.zeros(l_ref.shape, f32)
        acc_ref[...] = jnp.zeros(acc_ref.shape, f32)

    def head_rows(page_ref, kh):
        return page_ref[pl.ds(kh, PAGE_SIZE, stride=kv), :].astype(bf16)

    def update(k_refs, v_refs, bias, mask):
        s = jnp.concatenate(
            [jnp.concatenate([_nt_dot(q_ref[0, kh * rows:(kh + 1) * rows, :], head_rows(kr, kh)) for kr in k_refs],
                             axis=1) for kh in range(kv)], axis=0)
        s = s * scale + bias + jnp.concatenate([mask] * kv, axis=0)
        m_new, l_new, alpha, p = _softmax_step(s, m_ref[...], l_ref[...])
        p = p.astype(bf16)
        pv = []
        for kh in range(kv):
            pk = p[kh * rows:(kh + 1) * rows]
            pv.append(sum(jnp.dot(pk[:, r * PAGE_SIZE:(r + 1) * PAGE_SIZE], head_rows(vr, kh),
                                  preferred_element_type=f32) for r, vr in enumerate(v_refs)))
        acc_ref[...] = alpha * acc_ref[...] + jnp.concatenate(pv, axis=0)
        m_ref[...] = m_new
        l_ref[...] = l_new

    update(kpages, vpages, bias_ref[...], mask_ref[0])

    @pl.when(s_id == pl.num_programs(1) - 1)
    def _():
        update([kt_ref.at[0]], [vt_ref.at[0]], biast_ref[...], maskt_ref[0])
        o_ref[0] = acc_ref[...] / l_ref[...] * _silu(gate_ref[0])


def _attend_sample_a(page_table, q2, bias_tab, mask, cache_k, cache_v, layer, k_tail, v_tail, gate2):
    dbatch, qrows, _ = q2.shape
    kv = cache_k.shape[2] // PAGE_SIZE
    rows = qrows // kv
    n_pages = page_table.shape[1]
    past = n_pages * PAGE_SIZE
    pg = _pick_tile(n_pages, (16, 8, 4, 2, 1))
    qspec = pl.BlockSpec((1, qrows, HEAD_DIM), lambda b, s, pt: (b, 0, 0))
    kspecs = [pl.BlockSpec((None, None, PAGE_SIZE * kv, HEAD_DIM),
                           lambda b, s, pt, r=r: (layer, pt[b, s * pg + r], 0, 0)) for r in range(pg)]
    tailspec = pl.BlockSpec((1, PAGE_SIZE * kv, HEAD_DIM), lambda b, s, pt: (b, 0, 0))
    return pl.pallas_call(
        functools.partial(_attend_sample_a_kernel, pg=pg, kv=kv, scale=HEAD_DIM ** -0.5), name="attend_sample_a",
        grid_spec=pltpu.PrefetchScalarGridSpec(
            num_scalar_prefetch=1, grid=(dbatch, n_pages // pg),
            in_specs=[qspec,
                      pl.BlockSpec((qrows, pg * PAGE_SIZE), lambda b, s, pt: (0, s)),
                      pl.BlockSpec((1, rows, pg * PAGE_SIZE), lambda b, s, pt: (b, 0, s)),
                      pl.BlockSpec((qrows, PAGE_SIZE), lambda b, s, pt: (0, past // PAGE_SIZE)),
                      pl.BlockSpec((1, rows, PAGE_SIZE), lambda b, s, pt: (b, 0, past // PAGE_SIZE)),
                      qspec] + kspecs + kspecs + [tailspec, tailspec],
            out_specs=qspec,
            scratch_shapes=[pltpu.VMEM((qrows, 1), f32), pltpu.VMEM((qrows, 1), f32),
                            pltpu.VMEM((qrows, HEAD_DIM), f32)]),
        out_shape=jax.ShapeDtypeStruct(q2.shape, f32),
        compiler_params=_params(2),
    )(page_table, q2, bias_tab, mask, bias_tab, mask, gate2,
      *([cache_k] * pg), *([cache_v] * pg), k_tail, v_tail)


def _attend_prompt_b_kernel(q0_ref, q1_ref, q2_ref, k_ref, v_ref, gate_ref, gb_ref, *rest, rates, scale, unroll, shift):
    if shift is None:
        o_ref, qf_ref, og_ref, lse_ref = rest
    else:
        bk_ref, bv_ref, nk_ref, nv_ref, o_ref, ok_ref, ov_ref, qf_ref, og_ref, lse_ref, *ring_refs = rest
        step = pl.program_id(0) * pl.num_programs(1) + pl.program_id(1)
        _shift_ring_step(step, pl.num_programs(0) * pl.num_programs(1), (bk_ref, bv_ref), (nk_ref, nv_ref),
                         (ok_ref, ov_ref), *ring_refs, **shift)
    seq = k_ref.shape[0]
    col = lax.broadcasted_iota(i32, (BLK, 2 * BLK), 1)

    for g, (q_ref, d) in enumerate(zip((q0_ref, q1_ref, q2_ref), rates)):
        qf_ref[...] = q_ref[...].astype(f32)
        nb = seq // d // BLK
        bias = gb_ref[0, :, g * 2 * BLK:(g + 1) * 2 * BLK]

        def blk_body(it, carry, g=g, d=d, nb=nb, bias=bias):
            r = it // nb
            blk = it % nb
            start = r + blk * (BLK * d)
            start_prev = r + jnp.maximum(blk - 1, 0) * (BLK * d)
            cur = pl.ds(start, BLK, stride=d)
            prev = pl.ds(start_prev, BLK, stride=d)
            qb = qf_ref[cur, :].astype(bf16)
            kcat = jnp.concatenate([k_ref[prev, :], k_ref[cur, :]], axis=0).astype(bf16)
            vcat = jnp.concatenate([v_ref[prev, :], v_ref[cur, :]], axis=0).astype(bf16)
            s = _nt_dot(qb, kcat) * scale + bias
            s = jnp.where((col >= BLK) | (blk > 0), s, NEG_INF)
            m = s.max(axis=1, keepdims=True)
            e = jnp.exp(s - m)
            ssum = e.sum(axis=1, keepdims=True)
            og_ref[g, cur, :] = jnp.dot(e.astype(bf16), vcat, preferred_element_type=f32) / ssum
            lse_ref[g, cur, :] = jnp.broadcast_to(m + jnp.log(ssum), (BLK, HEAD_DIM))
            return carry

        lax.fori_loop(0, d * nb, blk_body, 0, unroll=unroll)

    def merge_body(cb, carry):
        rows = pl.ds(pl.multiple_of(cb * BLK, BLK), BLK)
        lses = [lse_ref[g, rows, :] for g in range(len(rates))]
        top = functools.reduce(jnp.maximum, lses)
        wts = [jnp.exp(x - top) for x in lses]
        num = sum(w * og_ref[g, rows, :] for g, w in enumerate(wts))
        gt = gate_ref[rows, :]
        o_ref[rows, :] = (num / sum(wts) * _silu(gt)).astype(o_ref.dtype)
        return carry

    lax.fori_loop(0, seq // BLK, merge_body, 0)


DIL_BLOCK_UNROLL = 16


def _attend_prompt_b(q, k, v, gate, bias_tab, batch, seq, buffers=None):
    n_heads = k.shape[1] // HEAD_DIM
    n_dil = len(DIL_RATES)
    blk = (seq, HEAD_DIM)
    qspecs = [pl.BlockSpec(blk, lambda b, h, g=g: (b, g * n_heads + h)) for g in range(n_dil)]
    hspec = pl.BlockSpec(blk, lambda b, h: (b, h))
    in_specs = qspecs + [hspec, hspec, hspec, pl.BlockSpec((1, BLK, bias_tab.shape[2]), lambda b, h: (h, 0, 0))]
    out_specs, out_shape = [hspec], [jax.ShapeDtypeStruct(k.shape, bf16)]
    scratch = [pltpu.VMEM(blk, f32), pltpu.VMEM((n_dil,) + blk, f32), pltpu.VMEM((n_dil,) + blk, f32)]
    args, shift = [q, q, q, k, v, gate, bias_tab], None
    if buffers is not None:
        buf_k, buf_v, layer, new_k, new_v = buffers
        plan = _shift_plan(buf_k, new_k)
        if 2 * plan["dbatch"] * ((plan["rows"] - plan["shift"]) // plan["chunk"]) + SHIFT_RING <= batch * n_heads:
            shift = dict(plan, layer=layer)
            anyspec = pl.BlockSpec(memory_space=pl.ANY)
            out = jax.ShapeDtypeStruct((plan["dbatch"], plan["rows"], buf_k.shape[3]), buf_k.dtype)
            in_specs += [anyspec] * 4
            out_specs += [anyspec] * 2
            out_shape += [out, out]
            scratch += _shift_scratch(plan, buf_k)
            args += [buf_k, buf_v, new_k, new_v]
    res = pl.pallas_call(
        functools.partial(_attend_prompt_b_kernel, rates=DIL_RATES, scale=HEAD_DIM ** -0.5, unroll=DIL_BLOCK_UNROLL, shift=shift),
        name="attend_prompt_b", grid=(batch, n_heads),
        in_specs=in_specs, out_specs=out_specs, out_shape=out_shape, scratch_shapes=scratch,
        compiler_params=_params(2),
    )(*args)
    if shift is not None:
        return tuple(res)
    if buffers is not None:
        return (res[0],) + tuple(_shift_buffers(*buffers))
    return res[0], None, None


def _attend_sample_b_kernel(q_ref, k_ref, v_ref, kt_ref, vt_ref, bias_ref, gate_ref, o_ref, *, n_dil, t_new, scale):
    wb, hgs, _ = k_ref.shape
    k2 = k_ref.reshape(wb * hgs, HEAD_DIM)
    v2 = v_ref.reshape(wb * hgs, HEAD_DIM)
    for hl in range(hgs):
        cols = slice(hl * HEAD_DIM, (hl + 1) * HEAD_DIM)
        head = pl.ds(hl, wb, stride=hgs)
        q = q_ref[0, hl]
        s_main = _nt_dot(q, k2[head, :].astype(bf16)) * scale + bias_ref[hl, :, :wb]
        s_tail = _nt_dot(q, kt_ref[0, :, cols].astype(bf16)) * scale + bias_ref[hl, :, wb:]
        m = jnp.maximum(s_main.max(axis=1, keepdims=True), s_tail.max(axis=1, keepdims=True))
        e_main = jnp.exp(s_main - m)
        e_tail = jnp.exp(s_tail - m)
        ssum = e_main.sum(axis=1, keepdims=True) + e_tail.sum(axis=1, keepdims=True)
        o = (jnp.dot(e_main.astype(bf16), v2[head, :].astype(bf16), preferred_element_type=f32)
             + jnp.dot(e_tail.astype(bf16), vt_ref[0, :, cols].astype(bf16), preferred_element_type=f32)) / ssum
        lse = m + jnp.log(ssum)
        lses = [lse[g * t_new:(g + 1) * t_new] for g in range(n_dil)]
        top = functools.reduce(jnp.maximum, lses)
        wts = [jnp.exp(x - top) for x in lses]
        num = sum(w * o[g * t_new:(g + 1) * t_new] for g, w in enumerate(wts))
        o_ref[0, hl] = num / sum(wts) * _silu(gate_ref[0, hl])


def _attend_sample_b(q16, buf_k, buf_v, layer, k_tail, v_tail, bias_tab, gate4, t_new):
    dbatch, n_heads, rows, _ = q16.shape
    wb = buf_k.shape[2]
    hgs = 8 if n_heads % 8 == 0 else n_heads
    grouped = lambda a: a.reshape(a.shape[0], dbatch, wb, n_heads // hgs, hgs, HEAD_DIM)
    bufspec = pl.BlockSpec((None, None, wb, None, hgs, HEAD_DIM), lambda b, hg: (layer, b, 0, hg, 0, 0))
    tailspec = pl.BlockSpec((1, LANE, hgs * HEAD_DIM), lambda b, hg: (b, 0, hg))
    hspec = lambda r: pl.BlockSpec((1, hgs, r, HEAD_DIM), lambda b, hg: (b, hg, 0, 0))
    return pl.pallas_call(
        functools.partial(_attend_sample_b_kernel, n_dil=len(DIL_RATES), t_new=t_new, scale=HEAD_DIM ** -0.5),
        name="attend_sample_b", grid=(dbatch, n_heads // hgs),
        in_specs=[hspec(rows), bufspec, bufspec, tailspec, tailspec,
                  pl.BlockSpec((hgs, rows, wb + LANE), lambda b, hg: (hg, 0, 0)),
                  hspec(t_new)],
        out_specs=hspec(t_new),
        out_shape=jax.ShapeDtypeStruct((dbatch, n_heads, t_new, HEAD_DIM), f32),
        compiler_params=_params(2),
    )(q16, grouped(buf_k), grouped(buf_v), k_tail, v_tail, bias_tab, gate4)


SHIFT_RING = 4
SHIFT_LAG = 2
SHIFT_CHUNK_BYTES = 5 << 20


def _shift_kernel(bk_ref, bv_ref, nk_ref, nv_ref, ok_ref, ov_ref, ring, in_sem, out_sem, new_sem,
                  *, layer, dbatch, rows, shift, chunk):
    reads, writes, tails = [], [], []
    n_chunks = (rows - shift) // chunk
    for a, (src, new, dst) in enumerate(((bk_ref, nk_ref, ok_ref), (bv_ref, nv_ref, ov_ref))):
        for b in range(dbatch):
            for c in range(n_chunks):
                slot = len(reads) % SHIFT_RING
                reads.append(pltpu.make_async_copy(src.at[layer, b, pl.ds(shift + c * chunk, chunk)],
                                                   ring.at[slot], in_sem.at[slot]))
                writes.append(pltpu.make_async_copy(ring.at[slot], dst.at[b, pl.ds(c * chunk, chunk)],
                                                    out_sem.at[slot]))
            tails.append(pltpu.make_async_copy(new.at[b], dst.at[b, pl.ds(rows - shift, shift)], new_sem.at[a, b]))
    for t in tails:
        t.start()
    n = len(reads)
    for i in range(n + SHIFT_LAG):
        if i < n:
            if i >= SHIFT_RING:
                writes[i - SHIFT_RING].wait()
            reads[i].start()
        if 0 <= i - SHIFT_LAG < n:
            reads[i - SHIFT_LAG].wait()
            writes[i - SHIFT_LAG].start()
    for i in range(max(0, n - SHIFT_RING), n):
        writes[i].wait()
    for t in tails:
        t.wait()


def _shift_plan(buf, new):
    _, dbatch, rows, width = buf.shape
    shift = new.shape[1]
    body = (rows - shift) // shift
    per = max(u for u in range(1, body + 1)
              if body % u == 0 and u * shift * width * buf.dtype.itemsize <= SHIFT_CHUNK_BYTES)
    return dict(dbatch=dbatch, rows=rows, shift=shift, chunk=per * shift)


def _shift_scratch(plan, buf):
    return [pltpu.VMEM((SHIFT_RING, plan["chunk"], buf.shape[3]), buf.dtype),
            pltpu.SemaphoreType.DMA((SHIFT_RING,)), pltpu.SemaphoreType.DMA((SHIFT_RING,)),
            pltpu.SemaphoreType.DMA((2, plan["dbatch"]))]


def _shift_ring_step(t, n_steps, bufs, news, outs, ring, in_sem, out_sem, new_sem, *, layer, dbatch, rows, shift, chunk):
    per_b = (rows - shift) // chunk
    per_arr = dbatch * per_b
    n = len(bufs) * per_arr

    def for_chunk(i, read, action):
        slot = i % SHIFT_RING
        b, c = (i % per_arr) // per_b, i % per_b
        for a in range(len(bufs)):
            @pl.when(i // per_arr == a)
            def _(a=a):
                if read:
                    action(pltpu.make_async_copy(bufs[a].at[layer, b, pl.ds(shift + c * chunk, chunk)],
                                                 ring.at[slot], in_sem.at[slot]))
                else:
                    action(pltpu.make_async_copy(ring.at[slot], outs[a].at[b, pl.ds(c * chunk, chunk)],
                                                 out_sem.at[slot]))

    tails = [pltpu.make_async_copy(news[a].at[b], outs[a].at[b, pl.ds(rows - shift, shift)], new_sem.at[a, b])
             for a in range(len(bufs)) for b in range(dbatch)]

    @pl.when(t == 0)
    def _():
        for cp in tails:
            cp.start()

    @pl.when((t >= SHIFT_RING) & (t < n + SHIFT_RING))
    def _():
        for_chunk(t - SHIFT_RING, False, lambda cp: cp.wait())

    @pl.when(t < n)
    def _():
        for_chunk(t, True, lambda cp: cp.start())

    @pl.when((t >= SHIFT_LAG) & (t < n + SHIFT_LAG))
    def _():
        for_chunk(t - SHIFT_LAG, True, lambda cp: cp.wait())
        for_chunk(t - SHIFT_LAG, False, lambda cp: cp.start())

    @pl.when(t == n_steps - 1)
    def _():
        for cp in tails:
            cp.wait()


def _shift_buffers(buf_k, buf_v, layer, new_k, new_v):
    plan = _shift_plan(buf_k, new_k)
    anyspec = pl.BlockSpec(memory_space=pl.ANY)
    out = jax.ShapeDtypeStruct((plan["dbatch"], plan["rows"], buf_k.shape[3]), buf_k.dtype)
    return pl.pallas_call(
        functools.partial(_shift_kernel, layer=layer, **plan), name="shift_buffers",
        in_specs=[anyspec] * 4, out_specs=[anyspec] * 2, out_shape=[out, out],
        scratch_shapes=_shift_scratch(plan, buf_k),
        compiler_params=pltpu.CompilerParams(vmem_limit_bytes=VMEM_LIMIT),
    )(buf_k, buf_v, new_k, new_v)


def _offsets(sizes):
    return [int(x) for x in np.cumsum((0,) + tuple(sizes))[:-1]]


def _pad_rows(a, rows):
    return jnp.pad(a, ((0, 0), (0, rows - a.shape[1])) + ((0, 0),) * (a.ndim - 2))


def _layer_a(xp, xs, dims, j, last, cache_k, cache_v, cache_ki, page_table, w_in, w_out, ln_g, ln_b, rel_bias):
    batch, seq, dbatch, t_new = dims
    (xp, xpb), (xs, xsb) = xp, xs
    d_model = xp.shape[1]
    n_heads = d_model // HEAD_DIM
    kv, hi = KV_HEADS_A, IDX_HEADS
    group = n_heads // kv
    branch = n_heads * HEAD_DIM
    sizes = (branch, kv * HEAD_DIM, kv * HEAD_DIM, branch, hi * IDX_DIM, IDX_DIM, hi)
    offs = _offsets(sizes)
    tail_w = 2 * LANE
    w_t = w_in.shape[2] % LANE != 0
    w_use = jnp.swapaxes(w_in, 1, 2) if w_t else w_in
    tail_pad = ((0, 0), (0, tail_w - IDX_DIM - hi), (0, 0)) if w_t else ((0, 0), (0, 0), (0, tail_w - IDX_DIM - hi))
    w_tail = jnp.pad(w_use[j:j + 1, offs[5]:, :] if w_t else w_use[j:j + 1, :, offs[5]:], tail_pad)

    mm = lambda slot, dt, nm: _matmul(xpb, xsb, w_use, j, offs[slot], sizes[slot], dt, f"proj_a_{nm}", w_t)
    (q, q_s), (k_p, k_s), (v_p, v_s) = mm(0, bf16, "q"), mm(1, f32, "k"), mm(2, f32, "v")
    (gate, gate_s), (qi, qi_s) = mm(3, f32, "gate"), mm(4, bf16, "qi")
    kw, kw_s = _matmul(xpb, xsb, w_tail, 0, 0, tail_w, f32, "proj_a_kiwi", w_t)
    ki_p, wi = kw[:, :IDX_DIM], kw[:, IDX_DIM:IDX_DIM + hi]
    ki_s, wi_s = kw_s[:, :IDX_DIM], kw_s[:, IDX_DIM:IDX_DIM + hi]

    dist = (np.arange(ATTN_A_TQ)[:, None] + seq - ATTN_A_TQ) - np.arange(seq + ATTN_A_CHUNK - ATTN_A_TQ)[None, :]
    bias_p = _bias_table(rel_bias, _bucket_codes(dist))
    maskadd = _select_prompt(qi, ki_p, wi.T, batch, seq, min(TOPK_MAX, seq // 4))
    og = _attend_prompt(q, k_p, v_p, maskadd, bias_p, gate, batch, seq)

    q, gate, qi, wi = q_s, gate_s, qi_s, wi_s
    n_pages = page_table.shape[1]
    past = n_pages * PAGE_SIZE
    n_keys = past + t_new
    n_lanes = past + PAGE_SIZE
    cscale = (IDX_DIM ** -0.5) * (hi ** -0.5)
    qs = qi.reshape(dbatch, t_new * hi, IDX_DIM)
    wbc = jnp.broadcast_to((wi * cscale).reshape(dbatch, t_new * hi, 1), (dbatch, t_new * hi, LANE))
    ki_tail = _pad_rows(ki_s.reshape(dbatch, t_new, IDX_DIM), PAGE_SIZE)
    scores = _sample_scores(page_table, qs, wbc, cache_ki, j, ki_tail, t_new)
    mask_s = _select_sample(scores.reshape(dbatch * t_new, n_lanes), past, t_new, min(TOPK_MAX, n_keys // 4))

    def to_rows(a):
        a = a.reshape(dbatch, t_new, kv, group, HEAD_DIM)
        return a.transpose(0, 2, 3, 1, 4).reshape(dbatch, kv * group * t_new, HEAD_DIM)

    def to_page(a):
        return _pad_rows(a.reshape(dbatch, t_new, kv, HEAD_DIM), PAGE_SIZE).reshape(dbatch, PAGE_SIZE * kv, HEAD_DIM)

    rows = group * t_new
    dist = (past + np.arange(t_new)[:, None]) - np.arange(n_lanes)[None, :]
    bias_s = _bias_table(rel_bias, _bucket_codes(dist)).reshape(kv * rows, n_lanes)
    mask_g = jnp.broadcast_to(mask_s.reshape(dbatch, 1, t_new, n_lanes), (dbatch, group, t_new, n_lanes))
    mask_g = mask_g.reshape(dbatch, rows, n_lanes)
    pool = cache_k.shape[1]
    paged = lambda c: c.reshape(c.shape[0], pool, PAGE_SIZE * kv, HEAD_DIM)
    og2 = _attend_sample_a(page_table, to_rows(q), bias_s, mask_g, paged(cache_k), paged(cache_v), j,
                           to_page(k_s), to_page(v_s), to_rows(gate))
    og_s = og2.reshape(dbatch, kv, group, t_new, HEAD_DIM).transpose(0, 3, 1, 2, 4)
    og_s = og_s.reshape(dbatch * t_new, branch).astype(bf16)
    z, z_s = _matmul(og, og_s, w_out, j, 0, d_model, f32, "out_a", residual=(xp, xs))
    xp_new = _layernorm(z, ln_g[j], ln_b[j], not last)
    xs_new = _layernorm(z_s, ln_g[j], ln_b[j], not last)

    shp = lambda a, b_, t_, *rest: a.reshape(b_, t_, *rest)
    outs = (shp(k_p, batch, seq, kv, HEAD_DIM), shp(v_p, batch, seq, kv, HEAD_DIM), shp(ki_p, batch, seq, IDX_DIM),
            shp(k_s, dbatch, t_new, kv, HEAD_DIM), shp(v_s, dbatch, t_new, kv, HEAD_DIM),
            shp(ki_s, dbatch, t_new, IDX_DIM))
    return xp_new, xs_new, outs


def _layer_b(xp, xs, dims, j, last, buf_k, buf_v, w_in, w_out, ln_g, ln_b, rel_bias):
    batch, seq, dbatch, t_new = dims
    (xp, xpb), (xs, xsb) = xp, xs
    d_model = xp.shape[1]
    n_heads = d_model // HEAD_DIM
    branch = n_heads * HEAD_DIM
    n_dil = len(DIL_RATES)
    sizes = (n_dil * branch, branch, branch, branch)
    offs = _offsets(sizes)

    mm = lambda slot, dt, nm: _matmul(xpb, xsb, w_in, j, offs[slot], sizes[slot], dt, f"proj_b_{nm}")
    (q, q_s), (gate, gate_s) = mm(0, bf16, "q"), mm(3, f32, "gate")
    k_p, k_s, k_heads = _matmul_heads(xpb, xsb, w_in, j, offs[1], n_heads, "proj_b_k")
    v_p, v_s, v_heads = _matmul_heads(xpb, xsb, w_in, j, offs[2], n_heads, "proj_b_v")

    m =(np.arange(BLK)[:, None] + BLK) - np.arange(2 * BLK)[None, :]
    codes = [_bucket_codes(d * np.clip(m, 0, w // d), (m >= 0) & (m <= w // d))
             for d, w in zip(DIL_RATES, DIL_WINDOWS)]
    bias_p = _bias_table(rel_bias, np.concatenate(codes, axis=1))
    wb = buf_k.shape[2]
    flat = lambda buf: buf.reshape(buf.shape[0], dbatch, wb * n_heads, HEAD_DIM)
    og, new_k, new_v = _attend_prompt_b(q, k_p, v_p, gate, bias_p, batch, seq,
                                        (flat(buf_k), flat(buf_v), j, k_s.reshape(dbatch, t_new * n_heads, HEAD_DIM),
                                         v_s.reshape(dbatch, t_new * n_heads, HEAD_DIM)))

    q, gate = q_s, gate_s
    rows = -(-(n_dil * t_new) // 16) * 16
    pos = np.arange(wb + LANE)[None, :]
    codes = []
    for d, w in zip(DIL_RATES, DIL_WINDOWS):
        dist = wb + np.arange(t_new)[:, None] - pos
        codes.append(_bucket_codes(dist, (dist >= 0) & (dist % d == 0) & (dist // d <= w // d) & (pos < wb + t_new)))
    codes.append(np.full((rows - n_dil * t_new, wb + LANE), BUCKET_ZERO, np.int32))
    bias_s = _bias_table(rel_bias, np.concatenate(codes, axis=0))
    q16 = q.reshape(dbatch, t_new, n_dil, n_heads, HEAD_DIM).transpose(0, 3, 2, 1, 4)
    q16 = q16.reshape(dbatch, n_heads, n_dil * t_new, HEAD_DIM)
    q16 = jnp.pad(q16, ((0, 0), (0, 0), (0, rows - n_dil * t_new), (0, 0)))
    gate4 = gate.reshape(dbatch, t_new, n_heads, HEAD_DIM).transpose(0, 2, 1, 3)
    k_tail = _pad_rows(k_s.reshape(dbatch, t_new, branch), LANE)
    v_tail = _pad_rows(v_s.reshape(dbatch, t_new, branch), LANE)
    og4 = _attend_sample_b(q16, buf_k, buf_v, j, k_tail, v_tail, bias_s, gate4, t_new)
    og_s = og4.transpose(0, 2, 1, 3).reshape(dbatch * t_new, branch).astype(bf16)
    z, z_s = _matmul(og, og_s, w_out, j, 0, d_model, f32, "out_b", residual=(xp, xs))
    xp_new = _layernorm(z, ln_g[j], ln_b[j], not last)
    xs_new = _layernorm(z_s, ln_g[j], ln_b[j], not last)

    wp = min(W_MAX, seq)
    k_win = k_heads.reshape(batch, seq, n_heads, HEAD_DIM)[:, seq - wp:]
    v_win = v_heads.reshape(batch, seq, n_heads, HEAD_DIM)[:, seq - wp:]
    return xp_new, xs_new, (k_win, v_win, new_k.reshape(dbatch, wb, n_heads, HEAD_DIM),
                            new_v.reshape(dbatch, wb, n_heads, HEAD_DIM))


def kernel(x_prompt, x_sample, cache_k_a, cache_v_a, cache_kidx_a, cache_k_b, cache_v_b, page_table, rel_bias,
           w_in_a, w_out_a, ln_g_a, ln_b_a, w_in_b, w_out_b, ln_g_b, ln_b_b):
    batch, seq, d_model = x_prompt.shape
    dbatch, t_new, _ = x_sample.shape
    dims = (batch, seq, dbatch, t_new)
    with_bf16 = lambda a: (a, a.astype(bf16))
    xp = with_bf16(x_prompt.reshape(batch * seq, d_model))
    xs = with_bf16(x_sample.reshape(dbatch * t_new, d_model))
    outs_a, outs_b = [], []
    for layer in range(DEPTH):
        j = layer // 2
        if layer % 2 == 0:
            xp, xs, o = _layer_a(xp, xs, dims, j, layer == DEPTH - 1, cache_k_a, cache_v_a, cache_kidx_a, page_table,
                                 w_in_a, w_out_a, ln_g_a, ln_b_a, rel_bias)
            outs_a.append(o)
        else:
            xp, xs, o = _layer_b(xp, xs, dims, j, layer == DEPTH - 1, cache_k_b, cache_v_b,
                                 w_in_b, w_out_b, ln_g_b, ln_b_b, rel_bias)
            outs_b.append(o)
    stack = lambda group, idx: group[0][idx][None] if len(group) == 1 else jnp.stack([o[idx] for o in group])
    return (xp[0].reshape(batch, seq, d_model), xs[0].reshape(dbatch, t_new, d_model),
            *(stack(outs_a, n) for n in range(6)), *(stack(outs_b, n) for n in range(4)))
```

```python
import functools
import math

import jax
import jax.numpy as jnp
import numpy as np
from jax import lax
from jax.experimental import pallas as pl
from jax.experimental.pallas import tpu as pltpu

HEAD_DIM = 128
KV_HEADS_A = 8
IDX_HEADS = 64
IDX_DIM = 128
TOPK_MAX = 256
DIL_WINDOWS = (128, 512, 2048)
DIL_RATES = (1, 4, 16)
W_MAX = max(DIL_WINDOWS)
BLK = 128
N_BUCKETS = 32
REL_MAX_DIST = 2048
DEPTH = 2
ALPHA = (2 * DEPTH) ** 0.25
LN_EPS = 1e-5
PAGE_SIZE = 128

LANE = 128
NEG_INF = float("-inf")
INT_MIN = -(2 ** 31)
M_INIT = -1e30
BUCKET_ZERO = -1
BUCKET_MASKED = -2
VMEM_LIMIT = 56 * 1024 * 1024

f32 = jnp.float32
bf16 = jnp.bfloat16
i32 = jnp.int32


def _params(n_axes, vmem=VMEM_LIMIT):
    return pltpu.CompilerParams(dimension_semantics=("arbitrary",) * n_axes, vmem_limit_bytes=vmem)


def _nt_dot(a, b):
    return lax.dot_general(a, b, (((1,), (1,)), ((), ())), preferred_element_type=f32)


def _silu(x):
    return x * (1.0 / (1.0 + jnp.exp(-x)))


def _softmax_step(s, m_old, l_old):
    m_new = jnp.maximum(m_old, s.max(axis=1, keepdims=True))
    alpha = jnp.exp(m_old - m_new)
    p = jnp.exp(s - m_new)
    return m_new, alpha * l_old + p.sum(axis=1, keepdims=True), alpha, p


def _mm_kernel(x_ref, xs_ref, w_ref, *rest, chunk, transposed, residual):
    if residual:
        r_ref, rs_ref, o_ref, os_ref, wbf_ref = rest
    else:
        o_ref, os_ref, wbf_ref = rest
    mul = _nt_dot if transposed else functools.partial(jnp.dot, preferred_element_type=f32)

    @pl.when(pl.program_id(1) == 0)
    def _():
        def body(r, c):
            rows = pl.ds(pl.multiple_of(r * chunk, chunk), chunk)
            wbf_ref[rows, :] = w_ref[rows, :].astype(bf16)
            return c
        lax.fori_loop(0, w_ref.shape[0] // chunk, body, 0)
        ys = mul(xs_ref[...], wbf_ref[...])
        os_ref[...] = (ALPHA * rs_ref[...] + ys if residual else ys).astype(os_ref.dtype)

    y = mul(x_ref[...], wbf_ref[...])
    o_ref[...] = (ALPHA * r_ref[...] + y if residual else y).astype(o_ref.dtype)


def _pick_tile(n, candidates):
    for c in candidates:
        if n % c == 0:
            return c
    raise ValueError(f"no tile for {n}")


def _matmul(x_bf, xs_bf, w, layer, col_off, ncols, out_dtype, name, transposed=False, residual=None):
    m, k = x_bf.shape
    ms = xs_bf.shape[0]
    tn = _pick_tile(math.gcd(col_off, ncols) if col_off else ncols, (512, 256, 128))
    tm = _pick_tile(m, (1024, 512, 256, 128))
    off_blocks = col_off // tn
    if transposed:
        wblock, wspec, chunk = (tn, k), pl.BlockSpec((None, tn, k), lambda n, i: (layer, n + off_blocks, 0)), LANE
    else:
        wblock, wspec = (k, tn), pl.BlockSpec((None, k, tn), lambda n, i: (layer, 0, n + off_blocks))
        chunk = _pick_tile(k, (256, 128))
    ospecs = [pl.BlockSpec((tm, tn), lambda n, i: (i, n)), pl.BlockSpec((ms, tn), lambda n, i: (0, n))]
    return pl.pallas_call(
        functools.partial(_mm_kernel, chunk=chunk, transposed=transposed, residual=residual is not None), name=name,
        grid=(ncols // tn, m // tm),
        in_specs=[pl.BlockSpec((tm, k), lambda n, i: (i, 0)),
                  pl.BlockSpec((ms, k), lambda n, i: (0, 0)),
                  wspec] + (ospecs if residual is not None else []),
        out_specs=ospecs,
        out_shape=[jax.ShapeDtypeStruct((m, ncols), out_dtype), jax.ShapeDtypeStruct((ms, ncols), out_dtype)],
        scratch_shapes=[pltpu.VMEM(wblock, bf16)],
        compiler_params=_params(2),
    )(x_bf, xs_bf, w, *(residual or ()))


def _mm_heads_kernel(x_ref, xs_ref, w_ref, o_ref, os_ref, oh_ref, wbf_ref, *, k_chunk):
    @pl.when(pl.program_id(1) == 0)
    def _():
        def body(r, c):
            rows = pl.ds(pl.multiple_of(r * k_chunk, k_chunk), k_chunk)
            wbf_ref[rows, :] = w_ref[rows, :].astype(bf16)
            return c
        lax.fori_loop(0, w_ref.shape[0] // k_chunk, body, 0)
        os_ref[...] = jnp.dot(xs_ref[...], wbf_ref[...], preferred_element_type=f32)

    res = jnp.dot(x_ref[...], wbf_ref[...], preferred_element_type=f32)
    o_ref[...] = res
    tm, nh, dh = oh_ref.shape
    flat = oh_ref.reshape(tm * nh, dh)
    for hh in range(nh):
        flat[pl.ds(hh, tm, stride=nh), :] = res[:, hh * dh:(hh + 1) * dh]


def _matmul_heads(x_bf, xs_bf, w, layer, col_off, n_heads, name):
    m, k = x_bf.shape
    ms = xs_bf.shape[0]
    nh = 8 if n_heads % 8 == 0 else n_heads
    tn = nh * HEAD_DIM
    ncols = n_heads * HEAD_DIM
    tm = _pick_tile(m, (512, 256, 128))
    off_blocks = col_off // tn
    k_chunk = _pick_tile(k, (256, 128))
    return pl.pallas_call(
        functools.partial(_mm_heads_kernel, k_chunk=k_chunk), name=name,
        grid=(ncols // tn, m // tm),
        in_specs=[pl.BlockSpec((tm, k), lambda n, i: (i, 0)),
                  pl.BlockSpec((ms, k), lambda n, i: (0, 0)),
                  pl.BlockSpec((None, k, tn), lambda n, i: (layer, 0, n + off_blocks),
                               pipeline_mode=pl.Buffered(1))],
        out_specs=[pl.BlockSpec((tm, tn), lambda n, i: (i, n)),
                   pl.BlockSpec((ms, tn), lambda n, i: (0, n)),
                   pl.BlockSpec((tm, nh, HEAD_DIM), lambda n, i: (i, n, 0))],
        out_shape=[jax.ShapeDtypeStruct((m, ncols), f32), jax.ShapeDtypeStruct((ms, ncols), f32),
                   jax.ShapeDtypeStruct((m, n_heads, HEAD_DIM), f32)],
        scratch_shapes=[pltpu.VMEM((k, tn), bf16)],
        compiler_params=_params(2),
    )(x_bf, xs_bf, w)


def _ln_kernel(z_ref, g_ref, b_ref, o_ref, *obf_ref):
    z = z_ref[...]
    mu = jnp.mean(z, axis=-1, keepdims=True)
    zc = z - mu
    var = jnp.mean(zc * zc, axis=-1, keepdims=True)
    out = zc * lax.rsqrt(var + LN_EPS) * g_ref[...] + b_ref[...]
    o_ref[...] = out
    for ref in obf_ref:
        ref[...] = out.astype(bf16)


def _layernorm(z, g, b, with_bf16):
    m, d = z.shape
    tm = _pick_tile(m, (128, 64, 32, 16, 8))
    row = pl.BlockSpec((tm, d), lambda i: (i, 0))
    vec = pl.BlockSpec((1, d), lambda i: (0, 0))
    n_out = 2 if with_bf16 else 1
    res = pl.pallas_call(
        _ln_kernel, grid=(m // tm,), name="layernorm",
        in_specs=[row, vec, vec], out_specs=[row] * n_out,
        out_shape=[jax.ShapeDtypeStruct((m, d), f32), jax.ShapeDtypeStruct((m, d), bf16)][:n_out],
        compiler_params=_params(1),
    )(z, g.reshape(1, d), b.reshape(1, d))
    return (res[0], res[1] if with_bf16 else None)


def _rel_bucket_np(n):
    max_exact = N_BUCKETS // 2
    nf = np.maximum(n, 1).astype(np.float64)
    large = max_exact + (np.log(nf / max_exact) / math.log(REL_MAX_DIST / max_exact)
                         * (N_BUCKETS - max_exact)).astype(np.int64)
    large = np.minimum(large, N_BUCKETS - 1)
    return np.where(n < max_exact, n, large).astype(np.int32)


def _bucket_codes(dist, valid=None):
    code = np.where(dist >= 0, _rel_bucket_np(np.maximum(dist, 0)), BUCKET_ZERO)
    if valid is not None:
        code = np.where(valid, code, BUCKET_MASKED)
    return code.astype(np.int32)


def _bias_kernel(tbl_ref, bm_ref, o_ref, *, cw):
    h = pl.program_id(0)
    row = tbl_ref[pl.ds(h, 1), :]

    def body(c, carry):
        for t in range(cw // LANE):
            cols = pl.ds(pl.multiple_of(c * cw + t * LANE, LANE), LANE)
            code = bm_ref[:, cols]
            val = jnp.take_along_axis(jnp.broadcast_to(row, code.shape), jnp.maximum(code, 0), axis=1)
            o_ref[0, :, cols] = jnp.where(code >= 0, val, jnp.where(code == BUCKET_MASKED, NEG_INF, 0.0))
        return carry

    lax.fori_loop(0, bm_ref.shape[1] // cw, body, 0, unroll=8)


def _bias_table(rel_bias, codes):
    r, c = codes.shape
    n_heads = rel_bias.shape[1]
    units = c // LANE
    per = max(1, 16 // -(-r // 8))
    cw = LANE * max(u for u in range(1, units + 1) if units % u == 0 and u <= per)
    return pl.pallas_call(
        functools.partial(_bias_kernel, cw=cw), name="bias_table",
        grid=(n_heads,),
        in_specs=[pl.BlockSpec((n_heads, LANE), lambda h: (0, 0)),
                  pl.BlockSpec((r, c), lambda h: (0, 0))],
        out_specs=pl.BlockSpec((1, r, c), lambda h: (h, 0, 0)),
        out_shape=jax.ShapeDtypeStruct((n_heads, r, c), f32),
        compiler_params=_params(1),
    )(jnp.pad(rel_bias.T, ((0, 0), (0, LANE - N_BUCKETS))), jnp.asarray(codes))


def _sortable_key(x):
    bits = pltpu.bitcast(x, i32)
    return bits ^ ((bits >> 31) & 0x7FFFFFFF)


def _kth_largest_key(count_ge, shape, k_sel):
    def body(it, t):
        cand = t + lax.shift_left(jnp.int32(1), 31 - it)
        return jnp.where(count_ge(cand) >= k_sel, cand, t)
    return lax.fori_loop(0, 32, body, jnp.full(shape, INT_MIN, i32))


def _select_prompt_kernel(qi_ref, ki_ref, wt_ref, m_ref, kibf_ref, key_ref, *, n_idx_heads, k_sel, ck, hb, cscale):
    i = pl.program_id(1)
    tq = qi_ref.shape[0]

    @pl.when(i == 0)
    def _():
        kibf_ref[...] = ki_ref[...].astype(bf16)

    nck = ((i + 1) * tq + ck - 1) // ck
    t_row = i * tq + lax.broadcasted_iota(i32, (1, tq), 1)

    def chunk_body(c, carry):
        koff = pl.multiple_of(c * ck, ck)
        kc = kibf_ref[pl.ds(koff, ck), :]

        def heads_body(hg, acc):
            hoff = hg * (hb * IDX_DIM)
            qstack = jnp.concatenate(
                [qi_ref[:, pl.ds(pl.multiple_of(hoff + j * IDX_DIM, IDX_DIM), IDX_DIM)] for j in range(hb)], axis=0)
            s = _nt_dot(kc, qstack)
            w = wt_ref[pl.ds(pl.multiple_of(hg * hb, hb), hb), :] * cscale
            for j in range(hb):
                acc = acc + jnp.maximum(s[:, j * tq:(j + 1) * tq], 0.0) * w[j:j + 1, :]
            return acc

        acc = lax.fori_loop(0, n_idx_heads // hb, heads_body, jnp.zeros((ck, tq), f32), unroll=4)
        l_col = koff + lax.broadcasted_iota(i32, (ck, 1), 0)
        key_ref[pl.ds(koff, ck), :] = jnp.where(l_col <= t_row, _sortable_key(acc), INT_MIN)
        return carry

    lax.fori_loop(0, nck, chunk_body, 0)

    def count_ge(cand):
        def body(c, cnt):
            kc = key_ref[pl.ds(pl.multiple_of(c * ck, ck), ck), :]
            hit = jnp.where(kc >= cand, 1, 0).astype(i32)
            return cnt + hit.reshape(ck // 8, 8, tq).sum(axis=0)
        cnt = lax.fori_loop(0, nck, body, jnp.zeros((8, tq), i32))
        return cnt.sum(axis=0, keepdims=True)

    thr = _kth_largest_key(count_ge, (1, tq), k_sel)
    thr = jnp.maximum(thr, INT_MIN + 1)
    n_ge = count_ge(thr)
    ties = jnp.max(n_ge) > k_sel

    m_ref[...] = jnp.full(m_ref.shape, NEG_INF, m_ref.dtype)

    def write_chunk(koff, sel):
        m_ref[:, pl.ds(koff, ck)] = jnp.where(sel, 0.0, NEG_INF).astype(f32).T.astype(m_ref.dtype)

    @pl.when(jnp.logical_not(ties))
    def _():
        def out_body(c, carry):
            koff = pl.multiple_of(c * ck, ck)
            write_chunk(koff, key_ref[pl.ds(koff, ck), :] >= thr)
            return carry
        lax.fori_loop(0, nck, out_body, 0)

    @pl.when(ties)
    def _():
        quota = (k_sel - count_ge(thr + 1)).astype(f32)
        tri = (lax.broadcasted_iota(i32, (ck, ck), 0) >= lax.broadcasted_iota(i32, (ck, ck), 1)).astype(bf16)

        def out_body(c, seen):
            koff = pl.multiple_of(c * ck, ck)
            key = key_ref[pl.ds(koff, ck), :]
            eq = key == thr
            rank = jnp.dot(tri, jnp.where(eq, 1.0, 0.0).astype(bf16), preferred_element_type=f32) + seen
            write_chunk(koff, jnp.where(eq, jnp.where(rank <= quota, thr, thr - 1), key) >= thr)
            return rank[ck - 1:ck, :]
        lax.fori_loop(0, nck, out_body, jnp.zeros((1, tq), f32))


SELECT_KEY_CHUNK = 256
SELECT_HEADS_PER_STEP = 16


def _select_prompt(qi, ki, wi_t, batch, seq, k_sel):
    n_idx_heads = wi_t.shape[0]
    tq, ck = BLK, SELECT_KEY_CHUNK
    hb = _pick_tile(n_idx_heads, tuple(SELECT_HEADS_PER_STEP >> s for s in range(SELECT_HEADS_PER_STEP.bit_length())))
    nblk = seq // tq
    cscale = (IDX_DIM ** -0.5) * (n_idx_heads ** -0.5)
    return pl.pallas_call(
        functools.partial(_select_prompt_kernel, n_idx_heads=n_idx_heads, k_sel=k_sel, ck=ck, hb=hb, cscale=cscale),
        name="select_prompt", grid=(batch, nblk),
        in_specs=[pl.BlockSpec((tq, n_idx_heads * IDX_DIM), lambda b, i: (b * nblk + i, 0)),
                  pl.BlockSpec((seq, IDX_DIM), lambda b, i: (b, 0)),
                  pl.BlockSpec((n_idx_heads, tq), lambda b, i: (0, b * nblk + i))],
        out_specs=pl.BlockSpec((tq, seq), lambda b, i: (b * nblk + i, 0)),
        out_shape=jax.ShapeDtypeStruct((batch * seq, seq), bf16),
        scratch_shapes=[pltpu.VMEM((seq, IDX_DIM), bf16), pltpu.VMEM((seq, tq), i32)],
        compiler_params=_params(2),
    )(qi, ki, wi_t)


def _attend_prompt_kernel(q_ref, k_ref, v_ref, mk_ref, g_ref, gate_ref, o_ref,
                          kbf_ref, vbf_ref, s_ref, mp_ref, l_ref, acc_ref, *, group, ck, sub, scale):
    i = pl.program_id(2)
    nblk = pl.num_programs(2)
    tq = q_ref.shape[0]
    rows = group * tq

    @pl.when(i == 0)
    def _():
        kbf_ref[...] = k_ref[...].astype(bf16)
        vbf_ref[...] = v_ref[...].astype(bf16)

    nck = ((i + 1) * tq + ck - 1) // ck
    goff0 = (nblk - 1 - i) * tq
    qs = jnp.concatenate([q_ref[:, g * HEAD_DIM:(g + 1) * HEAD_DIM] for g in range(group)], axis=0)

    mp_ref[...] = jnp.full(mp_ref.shape, M_INIT, f32)

    def logits_body(c, carry):
        for u in range(ck // sub):
            koff = pl.multiple_of(c * ck + u * sub, sub)
            goff = pl.multiple_of(goff0 + koff, LANE)
            s = _nt_dot(qs, kbf_ref[pl.ds(koff, sub), :]) * scale
            bias = jnp.concatenate([g_ref[g, :, pl.ds(goff, sub)] for g in range(group)], axis=0)
            sel = mk_ref[:, pl.ds(koff, sub)].astype(f32)
            s = s + bias + jnp.concatenate([sel] * group, axis=0)
            s_ref[:, pl.ds(koff, sub)] = s
            mp_ref[...] = functools.reduce(jnp.maximum, [mp_ref[...]] + [s[:, t * LANE:(t + 1) * LANE]
                                                                         for t in range(sub // LANE)])
        return carry

    lax.fori_loop(0, nck, logits_body, 0)
    m_all = jnp.broadcast_to(mp_ref[...].max(axis=1, keepdims=True), (rows, LANE))

    l_ref[...] = jnp.zeros(l_ref.shape, f32)
    acc_ref[...] = jnp.zeros(acc_ref.shape, f32)

    def pv_body(c, carry):
        for u in range(ck // sub):
            koff = pl.multiple_of(c * ck + u * sub, sub)
            ps = [jnp.exp(s_ref[:, pl.ds(pl.multiple_of(koff + t * LANE, LANE), LANE)] - m_all)
                  for t in range(sub // LANE)]
            l_ref[...] += functools.reduce(jnp.add, ps)
            p = jnp.concatenate(ps, axis=1).astype(bf16)
            acc_ref[...] += jnp.dot(p, vbf_ref[pl.ds(koff, sub), :], preferred_element_type=f32)
        return carry

    lax.fori_loop(0, nck, pv_body, 0)
    o = acc_ref[...] / l_ref[...].sum(axis=1, keepdims=True)
    for g in range(group):
        cols = slice(g * HEAD_DIM, (g + 1) * HEAD_DIM)
        o_ref[:, cols] = (o[g * tq:(g + 1) * tq] * _silu(gate_ref[:, cols])).astype(o_ref.dtype)


ATTN_A_CHUNK = 512
ATTN_A_SUB = 512
ATTN_A_TQ = 256


def _attend_prompt(q, k, v, maskadd, bias_tab, gate, batch, seq):
    kv = k.shape[1] // HEAD_DIM
    group = q.shape[1] // (kv * HEAD_DIM)
    tq, ck = ATTN_A_TQ, ATTN_A_CHUNK
    nblk = seq // tq
    gw = group * HEAD_DIM
    qspec = pl.BlockSpec((tq, gw), lambda kh, b, i: (b * nblk + i, kh))
    kvspec = pl.BlockSpec((seq, HEAD_DIM), lambda kh, b, i: (b, kh))
    return pl.pallas_call(
        functools.partial(_attend_prompt_kernel, group=group, ck=ck, sub=ATTN_A_SUB, scale=HEAD_DIM ** -0.5),
        name="attend_prompt_a", grid=(kv, batch, nblk),
        in_specs=[qspec, kvspec, kvspec,
                  pl.BlockSpec((tq, seq), lambda kh, b, i: (b * nblk + i, 0)),
                  pl.BlockSpec((group, tq, bias_tab.shape[2]), lambda kh, b, i: (kh, 0, 0)),
                  qspec],
        out_specs=qspec,
        out_shape=jax.ShapeDtypeStruct(q.shape, bf16),
        scratch_shapes=[pltpu.VMEM((seq, HEAD_DIM), bf16), pltpu.VMEM((seq, HEAD_DIM), bf16),
                        pltpu.VMEM((group * tq, seq), f32), pltpu.VMEM((group * tq, LANE), f32),
                        pltpu.VMEM((group * tq, LANE), f32), pltpu.VMEM((group * tq, HEAD_DIM), f32)],
        compiler_params=_params(3),
    )(q, k, v, maskadd, bias_tab, gate)


def _score_page(qs, wb, kpage, t_new):
    s = _nt_dot(qs, kpage.astype(bf16))
    x = jnp.maximum(s, 0.0) * wb
    return x.reshape(t_new, x.shape[0] // t_new, x.shape[1]).sum(axis=1)


def _sample_scores_kernel(pt_ref, qs_ref, wb_ref, *refs, pg, t_new):
    o_ref = refs[pg]
    for r in range(pg):
        o_ref[0, :, r * PAGE_SIZE:(r + 1) * PAGE_SIZE] = _score_page(qs_ref[0], wb_ref[0], refs[r][...], t_new)


def _sample_scores_tail_kernel(qs_ref, wb_ref, kt_ref, o_ref, *, t_new):
    o_ref[0] = _score_page(qs_ref[0], wb_ref[0], kt_ref[0], t_new)


def _sample_scores(page_table, qs, wb, cache_ki, layer, ki_tail, t_new):
    dbatch, rows, _ = qs.shape
    n_pages = page_table.shape[1]
    pg = _pick_tile(n_pages, (16, 8, 4, 2, 1))
    qspec = pl.BlockSpec((1, rows, IDX_DIM), lambda b, s, pt: (b, 0, 0))
    page_specs = [pl.BlockSpec((None, None, PAGE_SIZE, IDX_DIM),
                               lambda b, s, pt, r=r: (layer, pt[b, s * pg + r], 0, 0)) for r in range(pg)]
    main = pl.pallas_call(
        functools.partial(_sample_scores_kernel, pg=pg, t_new=t_new), name="sample_scores",
        grid_spec=pltpu.PrefetchScalarGridSpec(
            num_scalar_prefetch=1, grid=(dbatch, n_pages // pg),
            in_specs=[qspec, qspec] + page_specs,
            out_specs=pl.BlockSpec((1, t_new, pg * PAGE_SIZE), lambda b, s, pt: (b, 0, s))),
        out_shape=jax.ShapeDtypeStruct((dbatch, t_new, n_pages * PAGE_SIZE), f32),
        compiler_params=_params(2),
    )(page_table, qs, wb, *([cache_ki] * pg))
    spec3 = lambda d1, d2: pl.BlockSpec((1, d1, d2), lambda b: (b, 0, 0))
    tail = pl.pallas_call(
        functools.partial(_sample_scores_tail_kernel, t_new=t_new), name="sample_scores_tail",
        grid=(dbatch,),
        in_specs=[spec3(rows, IDX_DIM), spec3(rows, IDX_DIM), spec3(PAGE_SIZE, IDX_DIM)],
        out_specs=spec3(t_new, PAGE_SIZE),
        out_shape=jax.ShapeDtypeStruct((dbatch, t_new, PAGE_SIZE), f32),
        compiler_params=_params(1),
    )(qs, wb, ki_tail)
    return jnp.concatenate([main, tail], axis=2)


def _select_sample_kernel(s_ref, o_ref, key_ref, *, past, t_new, k_sel):
    rows, n = s_ref.shape
    t = past + lax.broadcasted_iota(i32, (rows, 1), 0) % t_new
    l = lax.broadcasted_iota(i32, (1, n), 1)
    key_ref[...] = jnp.where(l <= t, _sortable_key(s_ref[...]), INT_MIN)

    def count_ge(cand):
        return jnp.where(key_ref[...] >= cand, 1, 0).astype(i32).sum(axis=1, keepdims=True)

    thr = jnp.maximum(_kth_largest_key(count_ge, (rows, 1), k_sel), INT_MIN + 1)
    quota = (k_sel - count_ge(thr + 1)).astype(f32)
    tri = (lax.broadcasted_iota(i32, (LANE, LANE), 0) <= lax.broadcasted_iota(i32, (LANE, LANE), 1)).astype(bf16)

    def tile_body(c, seen):
        lanes = pl.ds(pl.multiple_of(c * LANE, LANE), LANE)
        key = key_ref[:, lanes]
        eq = key == thr
        rank = jnp.dot(jnp.where(eq, 1.0, 0.0).astype(bf16), tri, preferred_element_type=f32) + seen
        kept = jnp.where(eq, jnp.where(rank <= quota, thr, thr - 1), key) >= thr
        o_ref[:, lanes] = jnp.where(kept, 0.0, NEG_INF).astype(f32)
        return rank.max(axis=1, keepdims=True)

    lax.fori_loop(0, n // LANE, tile_body, jnp.zeros((rows, 1), f32), unroll=3)


def _select_sample(scores, past, t_new, k_sel):
    return pl.pallas_call(
        functools.partial(_select_sample_kernel, past=past, t_new=t_new, k_sel=k_sel), name="select_sample",
        out_shape=jax.ShapeDtypeStruct(scores.shape, f32),
        scratch_shapes=[pltpu.VMEM(scores.shape, i32)],
        compiler_params=pltpu.CompilerParams(vmem_limit_bytes=VMEM_LIMIT),
    )(scores)


def _attend_sample_a_kernel(pt_ref, q_ref, bias_ref, mask_ref, biast_ref, maskt_ref, gate_ref, *refs, pg, kv, scale):
    kpages, vpages = refs[:pg], refs[pg:2 * pg]
    kt_ref, vt_ref, o_ref, m_ref, l_ref, acc_ref = refs[2 * pg:]
    s_id = pl.program_id(1)
    rows = q_ref.shape[1] // kv

    @pl.when(s_id == 0)
    def _():
        m_ref[...] = jnp.full(m_ref.shape, M_INIT, f32)
        l_ref[...] = jnp.zeros(l_ref.shape, f32)
        acc_ref[...] = jnp.zeros(acc_ref.shape, f32)

    def head_rows(page_ref, kh):
        return page_ref[pl.ds(kh, PAGE_SIZE, stride=kv), :].astype(bf16)

    def update(k_refs, v_refs, bias, mask):
        s = jnp.concatenate(
            [jnp.concatenate([_nt_dot(q_ref[0, kh * rows:(kh + 1) * rows, :], head_rows(kr, kh)) for kr in k_refs],
                             axis=1) for kh in range(kv)], axis=0)
        s = s * scale + bias + jnp.concatenate([mask] * kv, axis=0)
        m_new, l_new, alpha, p = _softmax_step(s, m_ref[...], l_ref[...])
        p = p.astype(bf16)
        pv = []
        for kh in range(kv):
            pk = p[kh * rows:(kh + 1) * rows]
            pv.append(sum(jnp.dot(pk[:, r * PAGE_SIZE:(r + 1) * PAGE_SIZE], head_rows(vr, kh),
                                  preferred_element_type=f32) for r, vr in enumerate(v_refs)))
        acc_ref[...] = alpha * acc_ref[...] + jnp.concatenate(pv, axis=0)
        m_ref[...] = m_new
        l_ref[...] = l_new

    update(kpages, vpages, bias_ref[...], mask_ref[0])

    @pl.when(s_id == pl.num_programs(1) - 1)
    def _():
        update([kt_ref.at[0]], [vt_ref.at[0]], biast_ref[...], maskt_ref[0])
        o_ref[0] = acc_ref[...] / l_ref[...] * _silu(gate_ref[0])


def _attend_sample_a(page_table, q2, bias_tab, mask, cache_k, cache_v, layer, k_tail, v_tail, gate2):
    dbatch, qrows, _ = q2.shape
    kv = cache_k.shape[2] // PAGE_SIZE
    rows = qrows // kv
    n_pages = page_table.shape[1]
    past = n_pages * PAGE_SIZE
    pg = _pick_tile(n_pages, (16, 8, 4, 2, 1))
    qspec = pl.BlockSpec((1, qrows, HEAD_DIM), lambda b, s, pt: (b, 0, 0))
    kspecs = [pl.BlockSpec((None, None, PAGE_SIZE * kv, HEAD_DIM),
                           lambda b, s, pt, r=r: (layer, pt[b, s * pg + r], 0, 0)) for r in range(pg)]
    tailspec = pl.BlockSpec((1, PAGE_SIZE * kv, HEAD_DIM), lambda b, s, pt: (b, 0, 0))
    return pl.pallas_call(
        functools.partial(_attend_sample_a_kernel, pg=pg, kv=kv, scale=HEAD_DIM ** -0.5), name="attend_sample_a",
        grid_spec=pltpu.PrefetchScalarGridSpec(
            num_scalar_prefetch=1, grid=(dbatch, n_pages // pg),
            in_specs=[qspec,
                      pl.BlockSpec((qrows, pg * PAGE_SIZE), lambda b, s, pt: (0, s)),
                      pl.BlockSpec((1, rows, pg * PAGE_SIZE), lambda b, s, pt: (b, 0, s)),
                      pl.BlockSpec((qrows, PAGE_SIZE), lambda b, s, pt: (0, past // PAGE_SIZE)),
                      pl.BlockSpec((1, rows, PAGE_SIZE), lambda b, s, pt: (b, 0, past // PAGE_SIZE)),
                      qspec] + kspecs + kspecs + [tailspec, tailspec],
            out_specs=qspec,
            scratch_shapes=[pltpu.VMEM((qrows, 1), f32), pltpu.VMEM((qrows, 1), f32),
                            pltpu.VMEM((qrows, HEAD_DIM), f32)]),
        out_shape=jax.ShapeDtypeStruct(q2.shape, f32),
        compiler_params=_params(2),
    )(page_table, q2, bias_tab, mask, bias_tab, mask, gate2,
      *([cache_k] * pg), *([cache_v] * pg), k_tail, v_tail)


def _attend_prompt_b_kernel(q0_ref, q1_ref, q2_ref, k_ref, v_ref, gate_ref, gb_ref, *rest, rates, scale, unroll, shift):
    if shift is None:
        o_ref, qf_ref, og_ref, lse_ref = rest
    else:
        bk_ref, bv_ref, nk_ref, nv_ref, o_ref, ok_ref, ov_ref, qf_ref, og_ref, lse_ref, *ring_refs = rest
        step = pl.program_id(0) * pl.num_programs(1) + pl.program_id(1)
        _shift_ring_step(step, pl.num_programs(0) * pl.num_programs(1), (bk_ref, bv_ref), (nk_ref, nv_ref),
                         (ok_ref, ov_ref), *ring_refs, **shift)
    seq = k_ref.shape[0]
    col = lax.broadcasted_iota(i32, (BLK, 2 * BLK), 1)

    for g, (q_ref, d) in enumerate(zip((q0_ref, q1_ref, q2_ref), rates)):
        qf_ref[...] = q_ref[...].astype(f32)
        nb = seq // d // BLK
        bias = gb_ref[0, :, g * 2 * BLK:(g + 1) * 2 * BLK]

        def blk_body(it, carry, g=g, d=d, nb=nb, bias=bias):
            r = it // nb
            blk = it % nb
            start = r + blk * (BLK * d)
            start_prev = r + jnp.maximum(blk - 1, 0) * (BLK * d)
            cur = pl.ds(start, BLK, stride=d)
            prev = pl.ds(start_prev, BLK, stride=d)
            qb = qf_ref[cur, :].astype(bf16)
            kcat = jnp.concatenate([k_ref[prev, :], k_ref[cur, :]], axis=0).astype(bf16)
            vcat = jnp.concatenate([v_ref[prev, :], v_ref[cur, :]], axis=0).astype(bf16)
            s = _nt_dot(qb, kcat) * scale + bias
            s = jnp.where((col >= BLK) | (blk > 0), s, NEG_INF)
            m = s.max(axis=1, keepdims=True)
            e = jnp.exp(s - m)
            ssum = e.sum(axis=1, keepdims=True)
            og_ref[g, cur, :] = jnp.dot(e.astype(bf16), vcat, preferred_element_type=f32) / ssum
            lse_ref[g, cur, :] = jnp.broadcast_to(m + jnp.log(ssum), (BLK, HEAD_DIM))
            return carry

        lax.fori_loop(0, d * nb, blk_body, 0, unroll=unroll)

    def merge_body(cb, carry):
        rows = pl.ds(pl.multiple_of(cb * BLK, BLK), BLK)
        lses = [lse_ref[g, rows, :] for g in range(len(rates))]
        top = functools.reduce(jnp.maximum, lses)
        wts = [jnp.exp(x - top) for x in lses]
        num = sum(w * og_ref[g, rows, :] for g, w in enumerate(wts))
        gt = gate_ref[rows, :]
        o_ref[rows, :] = (num / sum(wts) * _silu(gt)).astype(o_ref.dtype)
        return carry

    lax.fori_loop(0, seq // BLK, merge_body, 0)


DIL_BLOCK_UNROLL = 16


def _attend_prompt_b(q, k, v, gate, bias_tab, batch, seq, buffers=None):
    n_heads = k.shape[1] // HEAD_DIM
    n_dil = len(DIL_RATES)
    blk = (seq, HEAD_DIM)
    qspecs = [pl.BlockSpec(blk, lambda b, h, g=g: (b, g * n_heads + h)) for g in range(n_dil)]
    hspec = pl.BlockSpec(blk, lambda b, h: (b, h))
    in_specs = qspecs + [hspec, hspec, hspec, pl.BlockSpec((1, BLK, bias_tab.shape[2]), lambda b, h: (h, 0, 0))]
    out_specs, out_shape = [hspec], [jax.ShapeDtypeStruct(k.shape, bf16)]
    scratch = [pltpu.VMEM(blk, f32), pltpu.VMEM((n_dil,) + blk, f32), pltpu.VMEM((n_dil,) + blk, f32)]
    args, shift = [q, q, q, k, v, gate, bias_tab], None
    if buffers is not None:
        buf_k, buf_v, layer, new_k, new_v = buffers
        plan = _shift_plan(buf_k, new_k)
        if 2 * plan["dbatch"] * ((plan["rows"] - plan["shift"]) // plan["chunk"]) + SHIFT_RING <= batch * n_heads:
            shift = dict(plan, layer=layer)
            anyspec = pl.BlockSpec(memory_space=pl.ANY)
            out = jax.ShapeDtypeStruct((plan["dbatch"], plan["rows"], buf_k.shape[3]), buf_k.dtype)
            in_specs += [anyspec] * 4
            out_specs += [anyspec] * 2
            out_shape += [out, out]
            scratch += _shift_scratch(plan, buf_k)
            args += [buf_k, buf_v, new_k, new_v]
    res = pl.pallas_call(
        functools.partial(_attend_prompt_b_kernel, rates=DIL_RATES, scale=HEAD_DIM ** -0.5, unroll=DIL_BLOCK_UNROLL, shift=shift),
        name="attend_prompt_b", grid=(batch, n_heads),
        in_specs=in_specs, out_specs=out_specs, out_shape=out_shape, scratch_shapes=scratch,
        compiler_params=_params(2),
    )(*args)
    if shift is not None:
        return tuple(res)
    if buffers is not None:
        return (res[0],) + tuple(_shift_buffers(*buffers))
    return res[0], None, None


def _attend_sample_b_kernel(q_ref, k_ref, v_ref, kt_ref, vt_ref, bias_ref, gate_ref, o_ref, *, n_dil, t_new, scale):
    wb, hgs, _ = k_ref.shape
    k2 = k_ref.reshape(wb * hgs, HEAD_DIM)
    v2 = v_ref.reshape(wb * hgs, HEAD_DIM)
    for hl in range(hgs):
        cols = slice(hl * HEAD_DIM, (hl + 1) * HEAD_DIM)
        head = pl.ds(hl, wb, stride=hgs)
        q = q_ref[0, hl]
        s_main = _nt_dot(q, k2[head, :].astype(bf16)) * scale + bias_ref[hl, :, :wb]
        s_tail = _nt_dot(q, kt_ref[0, :, cols].astype(bf16)) * scale + bias_ref[hl, :, wb:]
        m = jnp.maximum(s_main.max(axis=1, keepdims=True), s_tail.max(axis=1, keepdims=True))
        e_main = jnp.exp(s_main - m)
        e_tail = jnp.exp(s_tail - m)
        ssum = e_main.sum(axis=1, keepdims=True) + e_tail.sum(axis=1, keepdims=True)
        o = (jnp.dot(e_main.astype(bf16), v2[head, :].astype(bf16), preferred_element_type=f32)
             + jnp.dot(e_tail.astype(bf16), vt_ref[0, :, cols].astype(bf16), preferred_element_type=f32)) / ssum
        lse = m + jnp.log(ssum)
        lses = [lse[g * t_new:(g + 1) * t_new] for g in range(n_dil)]
        top = functools.reduce(jnp.maximum, lses)
        wts = [jnp.exp(x - top) for x in lses]
        num = sum(w * o[g * t_new:(g + 1) * t_new] for g, w in enumerate(wts))
        o_ref[0, hl] = num / sum(wts) * _silu(gate_ref[0, hl])


def _attend_sample_b(q16, buf_k, buf_v, layer, k_tail, v_tail, bias_tab, gate4, t_new):
    dbatch, n_heads, rows, _ = q16.shape
    wb = buf_k.shape[2]
    hgs = 8 if n_heads % 8 == 0 else n_heads
    grouped = lambda a: a.reshape(a.shape[0], dbatch, wb, n_heads // hgs, hgs, HEAD_DIM)
    bufspec = pl.BlockSpec((None, None, wb, None, hgs, HEAD_DIM), lambda b, hg: (layer, b, 0, hg, 0, 0))
    tailspec = pl.BlockSpec((1, LANE, hgs * HEAD_DIM), lambda b, hg: (b, 0, hg))
    hspec = lambda r: pl.BlockSpec((1, hgs, r, HEAD_DIM), lambda b, hg: (b, hg, 0, 0))
    return pl.pallas_call(
        functools.partial(_attend_sample_b_kernel, n_dil=len(DIL_RATES), t_new=t_new, scale=HEAD_DIM ** -0.5),
        name="attend_sample_b", grid=(dbatch, n_heads // hgs),
        in_specs=[hspec(rows), bufspec, bufspec, tailspec, tailspec,
                  pl.BlockSpec((hgs, rows, wb + LANE), lambda b, hg: (hg, 0, 0)),
                  hspec(t_new)],
        out_specs=hspec(t_new),
        out_shape=jax.ShapeDtypeStruct((dbatch, n_heads, t_new, HEAD_DIM), f32),
        compiler_params=_params(2),
    )(q16, grouped(buf_k), grouped(buf_v), k_tail, v_tail, bias_tab, gate4)


SHIFT_RING = 4
SHIFT_LAG = 2
SHIFT_CHUNK_BYTES = 5 << 20


def _shift_kernel(bk_ref, bv_ref, nk_ref, nv_ref, ok_ref, ov_ref, ring, in_sem, out_sem, new_sem,
                  *, layer, dbatch, rows, shift, chunk):
    reads, writes, tails = [], [], []
    n_chunks = (rows - shift) // chunk
    for a, (src, new, dst) in enumerate(((bk_ref, nk_ref, ok_ref), (bv_ref, nv_ref, ov_ref))):
        for b in range(dbatch):
            for c in range(n_chunks):
                slot = len(reads) % SHIFT_RING
                reads.append(pltpu.make_async_copy(src.at[layer, b, pl.ds(shift + c * chunk, chunk)],
                                                   ring.at[slot], in_sem.at[slot]))
                writes.append(pltpu.make_async_copy(ring.at[slot], dst.at[b, pl.ds(c * chunk, chunk)],
                                                    out_sem.at[slot]))
            tails.append(pltpu.make_async_copy(new.at[b], dst.at[b, pl.ds(rows - shift, shift)], new_sem.at[a, b]))
    for t in tails:
        t.start()
    n = len(reads)
    for i in range(n + SHIFT_LAG):
        if i < n:
            if i >= SHIFT_RING:
                writes[i - SHIFT_RING].wait()
            reads[i].start()
        if 0 <= i - SHIFT_LAG < n:
            reads[i - SHIFT_LAG].wait()
            writes[i - SHIFT_LAG].start()
    for i in range(max(0, n - SHIFT_RING), n):
        writes[i].wait()
    for t in tails:
        t.wait()


def _shift_plan(buf, new):
    _, dbatch, rows, width = buf.shape
    shift = new.shape[1]
    body = (rows - shift) // shift
    per = max(u for u in range(1, body + 1)
              if body % u == 0 and u * shift * width * buf.dtype.itemsize <= SHIFT_CHUNK_BYTES)
    return dict(dbatch=dbatch, rows=rows, shift=shift, chunk=per * shift)


def _shift_scratch(plan, buf):
    return [pltpu.VMEM((SHIFT_RING, plan["chunk"], buf.shape[3]), buf.dtype),
            pltpu.SemaphoreType.DMA((SHIFT_RING,)), pltpu.SemaphoreType.DMA((SHIFT_RING,)),
            pltpu.SemaphoreType.DMA((2, plan["dbatch"]))]


def _shift_ring_step(t, n_steps, bufs, news, outs, ring, in_sem, out_sem, new_sem, *, layer, dbatch, rows, shift, chunk):
    per_b = (rows - shift) // chunk
    per_arr = dbatch * per_b
    n = len(bufs) * per_arr

    def for_chunk(i, read, action):
        slot = i % SHIFT_RING
        b, c = (i % per_arr) // per_b, i % per_b
        for a in range(len(bufs)):
            @pl.when(i // per_arr == a)
            def _(a=a):
                if read:
                    action(pltpu.make_async_copy(bufs[a].at[layer, b, pl.ds(shift + c * chunk, chunk)],
                                                 ring.at[slot], in_sem.at[slot]))
                else:
                    action(pltpu.make_async_copy(ring.at[slot], outs[a].at[b, pl.ds(c * chunk, chunk)],
                                                 out_sem.at[slot]))

    tails = [pltpu.make_async_copy(news[a].at[b], outs[a].at[b, pl.ds(rows - shift, shift)], new_sem.at[a, b])
             for a in range(len(bufs)) for b in range(dbatch)]

    @pl.when(t == 0)
    def _():
        for cp in tails:
            cp.start()

    @pl.when((t >= SHIFT_RING) & (t < n + SHIFT_RING))
    def _():
        for_chunk(t - SHIFT_RING, False, lambda cp: cp.wait())

    @pl.when(t < n)
    def _():
        for_chunk(t, True, lambda cp: cp.start())

    @pl.when((t >= SHIFT_LAG) & (t < n + SHIFT_LAG))
    def _():
        for_chunk(t - SHIFT_LAG, True, lambda cp: cp.wait())
        for_chunk(t - SHIFT_LAG, False, lambda cp: cp.start())

    @pl.when(t == n_steps - 1)
    def _():
        for cp in tails:
            cp.wait()


def _shift_buffers(buf_k, buf_v, layer, new_k, new_v):
    plan = _shift_plan(buf_k, new_k)
    anyspec = pl.BlockSpec(memory_space=pl.ANY)
    out = jax.ShapeDtypeStruct((plan["dbatch"], plan["rows"], buf_k.shape[3]), buf_k.dtype)
    return pl.pallas_call(
        functools.partial(_shift_kernel, layer=layer, **plan), name="shift_buffers",
        in_specs=[anyspec] * 4, out_specs=[anyspec] * 2, out_shape=[out, out],
        scratch_shapes=_shift_scratch(plan, buf_k),
        compiler_params=pltpu.CompilerParams(vmem_limit_bytes=VMEM_LIMIT),
    )(buf_k, buf_v, new_k, new_v)


def _offsets(sizes):
    return [int(x) for x in np.cumsum((0,) + tuple(sizes))[:-1]]


def _pad_rows(a, rows):
    return jnp.pad(a, ((0, 0), (0, rows - a.shape[1])) + ((0, 0),) * (a.ndim - 2))


def _layer_a(xp, xs, dims, j, last, cache_k, cache_v, cache_ki, page_table, w_in, w_out, ln_g, ln_b, rel_bias):
    batch, seq, dbatch, t_new = dims
    (xp, xpb), (xs, xsb) = xp, xs
    d_model = xp.shape[1]
    n_heads = d_model // HEAD_DIM
    kv, hi = KV_HEADS_A, IDX_HEADS
    group = n_heads // kv
    branch = n_heads * HEAD_DIM
    sizes = (branch, kv * HEAD_DIM, kv * HEAD_DIM, branch, hi * IDX_DIM, IDX_DIM, hi)
    offs = _offsets(sizes)
    tail_w = 2 * LANE
    w_t = w_in.shape[2] % LANE != 0
    w_use = jnp.swapaxes(w_in, 1, 2) if w_t else w_in
    tail_pad = ((0, 0), (0, tail_w - IDX_DIM - hi), (0, 0)) if w_t else ((0, 0), (0, 0), (0, tail_w - IDX_DIM - hi))
    w_tail = jnp.pad(w_use[j:j + 1, offs[5]:, :] if w_t else w_use[j:j + 1, :, offs[5]:], tail_pad)

    mm = lambda slot, dt, nm: _matmul(xpb, xsb, w_use, j, offs[slot], sizes[slot], dt, f"proj_a_{nm}", w_t)
    (q, q_s), (k_p, k_s), (v_p, v_s) = mm(0, bf16, "q"), mm(1, f32, "k"), mm(2, f32, "v")
    (gate, gate_s), (qi, qi_s) = mm(3, f32, "gate"), mm(4, bf16, "qi")
    kw, kw_s = _matmul(xpb, xsb, w_tail, 0, 0, tail_w, f32, "proj_a_kiwi", w_t)
    ki_p, wi = kw[:, :IDX_DIM], kw[:, IDX_DIM:IDX_DIM + hi]
    ki_s, wi_s = kw_s[:, :IDX_DIM], kw_s[:, IDX_DIM:IDX_DIM + hi]

    dist = (np.arange(ATTN_A_TQ)[:, None] + seq - ATTN_A_TQ) - np.arange(seq + ATTN_A_CHUNK - ATTN_A_TQ)[None, :]
    bias_p = _bias_table(rel_bias, _bucket_codes(dist))
    maskadd = _select_prompt(qi, ki_p, wi.T, batch, seq, min(TOPK_MAX, seq // 4))
    og = _attend_prompt(q, k_p, v_p, maskadd, bias_p, gate, batch, seq)

    q, gate, qi, wi = q_s, gate_s, qi_s, wi_s
    n_pages = page_table.shape[1]
    past = n_pages * PAGE_SIZE
    n_keys = past + t_new
    n_lanes = past + PAGE_SIZE
    cscale = (IDX_DIM ** -0.5) * (hi ** -0.5)
    qs = qi.reshape(dbatch, t_new * hi, IDX_DIM)
    wbc = jnp.broadcast_to((wi * cscale).reshape(dbatch, t_new * hi, 1), (dbatch, t_new * hi, LANE))
    ki_tail = _pad_rows(ki_s.reshape(dbatch, t_new, IDX_DIM), PAGE_SIZE)
    scores = _sample_scores(page_table, qs, wbc, cache_ki, j, ki_tail, t_new)
    mask_s = _select_sample(scores.reshape(dbatch * t_new, n_lanes), past, t_new, min(TOPK_MAX, n_keys // 4))

    def to_rows(a):
        a = a.reshape(dbatch, t_new, kv, group, HEAD_DIM)
        return a.transpose(0, 2, 3, 1, 4).reshape(dbatch, kv * group * t_new, HEAD_DIM)

    def to_page(a):
        return _pad_rows(a.reshape(dbatch, t_new, kv, HEAD_DIM), PAGE_SIZE).reshape(dbatch, PAGE_SIZE * kv, HEAD_DIM)

    rows = group * t_new
    dist = (past + np.arange(t_new)[:, None]) - np.arange(n_lanes)[None, :]
    bias_s = _bias_table(rel_bias, _bucket_codes(dist)).reshape(kv * rows, n_lanes)
    mask_g = jnp.broadcast_to(mask_s.reshape(dbatch, 1, t_new, n_lanes), (dbatch, group, t_new, n_lanes))
    mask_g = mask_g.reshape(dbatch, rows, n_lanes)
    pool = cache_k.shape[1]
    paged = lambda c: c.reshape(c.shape[0], pool, PAGE_SIZE * kv, HEAD_DIM)
    og2 = _attend_sample_a(page_table, to_rows(q), bias_s, mask_g, paged(cache_k), paged(cache_v), j,
                           to_page(k_s), to_page(v_s), to_rows(gate))
    og_s = og2.reshape(dbatch, kv, group, t_new, HEAD_DIM).transpose(0, 3, 1, 2, 4)
    og_s = og_s.reshape(dbatch * t_new, branch).astype(bf16)
    z, z_s = _matmul(og, og_s, w_out, j, 0, d_model, f32, "out_a", residual=(xp, xs))
    xp_new = _layernorm(z, ln_g[j], ln_b[j], not last)
    xs_new = _layernorm(z_s, ln_g[j], ln_b[j], not last)

    shp = lambda a, b_, t_, *rest: a.reshape(b_, t_, *rest)
    outs = (shp(k_p, batch, seq, kv, HEAD_DIM), shp(v_p, batch, seq, kv, HEAD_DIM), shp(ki_p, batch, seq, IDX_DIM),
            shp(k_s, dbatch, t_new, kv, HEAD_DIM), shp(v_s, dbatch, t_new, kv, HEAD_DIM),
            shp(ki_s, dbatch, t_new, IDX_DIM))
    return xp_new, xs_new, outs


def _layer_b(xp, xs, dims, j, last, buf_k, buf_v, w_in, w_out, ln_g, ln_b, rel_bias):
    batch, seq, dbatch, t_new = dims
    (xp, xpb), (xs, xsb) = xp, xs
    d_model = xp.shape[1]
    n_heads = d_model // HEAD_DIM
    branch = n_heads * HEAD_DIM
    n_dil = len(DIL_RATES)
    sizes = (n_dil * branch, branch, branch, branch)
    offs = _offsets(sizes)

    mm = lambda slot, dt, nm: _matmul(xpb, xsb, w_in, j, offs[slot], sizes[slot], dt, f"proj_b_{nm}")
    (q, q_s), (gate, gate_s) = mm(0, bf16, "q"), mm(3, f32, "gate")
    k_p, k_s, k_heads = _matmul_heads(xpb, xsb, w_in, j, offs[1], n_heads, "proj_b_k")
    v_p, v_s, v_heads = _matmul_heads(xpb, xsb, w_in, j, offs[2], n_heads, "proj_b_v")

    m = (np.arange(BLK)[:, None] + BLK) - np.arange(2 * BLK)[None, :]
    codes = [_bucket_codes(d * np.clip(m, 0, w // d), (m >= 0) & (m <= w // d))
             for d, w in zip(DIL_RATES, DIL_WINDOWS)]
    bias_p = _bias_table(rel_bias, np.concatenate(codes, axis=1))
    wb = buf_k.shape[2]
    flat = lambda buf: buf.reshape(buf.shape[0], dbatch, wb * n_heads, HEAD_DIM)
    og, new_k, new_v = _attend_prompt_b(q, k_p, v_p, gate, bias_p, batch, seq,
                                        (flat(buf_k), flat(buf_v), j, k_s.reshape(dbatch, t_new * n_heads, HEAD_DIM),
                                         v_s.reshape(dbatch, t_new * n_heads, HEAD_DIM)))

    q, gate = q_s, gate_s
    rows = -(-(n_dil * t_new) // 16) * 16
    pos = np.arange(wb + LANE)[None, :]
    codes = []
    for d, w in zip(DIL_RATES, DIL_WINDOWS):
        dist = wb + np.arange(t_new)[:, None] - pos
        codes.append(_bucket_codes(dist, (dist >= 0) & (dist % d == 0) & (dist // d <= w // d) & (pos < wb + t_new)))
    codes.append(np.full((rows - n_dil * t_new, wb + LANE), BUCKET_ZERO, np.int32))
    bias_s = _bias_table(rel_bias, np.concatenate(codes, axis=0))
    q16 = q.reshape(dbatch, t_new, n_dil, n_heads, HEAD_DIM).transpose(0, 3, 2, 1, 4)
    q16 = q16.reshape(dbatch, n_heads, n_dil * t_new, HEAD_DIM)
    q16 = jnp.pad(q16, ((0, 0), (0, 0), (0, rows - n_dil * t_new), (0, 0)))
    gate4 = gate.reshape(dbatch, t_new, n_heads, HEAD_DIM).transpose(0, 2, 1, 3)
    k_tail = _pad_rows(k_s.reshape(dbatch, t_new, branch), LANE)
    v_tail = _pad_rows(v_s.reshape(dbatch, t_new, branch), LANE)
    og4 = _attend_sample_b(q16, buf_k, buf_v, j, k_tail, v_tail, bias_s, gate4, t_new)
    og_s = og4.transpose(0, 2, 1, 3).reshape(dbatch * t_new, branch).astype(bf16)
    z, z_s = _matmul(og, og_s, w_out, j, 0, d_model, f32, "out_b", residual=(xp, xs))
    xp_new = _layernorm(z, ln_g[j], ln_b[j], not last)
    xs_new = _layernorm(z_s, ln_g[j], ln_b[j], not last)

    wp = min(W_MAX, seq)
    k_win = k_heads.reshape(batch, seq, n_heads, HEAD_DIM)[:, seq - wp:]
    v_win = v_heads.reshape(batch, seq, n_heads, HEAD_DIM)[:, seq - wp:]
    return xp_new, xs_new, (k_win, v_win, new_k.reshape(dbatch, wb, n_heads, HEAD_DIM),
                            new_v.reshape(dbatch, wb, n_heads, HEAD_DIM))


def kernel(x_prompt, x_sample, cache_k_a, cache_v_a, cache_kidx_a, cache_k_b, cache_v_b, page_table, rel_bias,
           w_in_a, w_out_a, ln_g_a, ln_b_a, w_in_b, w_out_b, ln_g_b, ln_b_b):
    batch, seq, d_model = x_prompt.shape
    dbatch, t_new, _ = x_sample.shape
    dims = (batch, seq, dbatch, t_new)
    with_bf16 = lambda a: (a, a.astype(bf16))
    xp = with_bf16(x_prompt.reshape(batch * seq, d_model))
    xs = with_bf16(x_sample.reshape(dbatch * t_new, d_model))
    outs_a, outs_b = [], []
    for layer in range(DEPTH):
        j = layer // 2
        if layer % 2 == 0:
            xp, xs, o = _layer_a(xp, xs, dims, j, layer == DEPTH - 1, cache_k_a, cache_v_a, cache_kidx_a, page_table,
                                 w_in_a, w_out_a, ln_g_a, ln_b_a, rel_bias)
            outs_a.append(o)
        else:
            xp, xs, o = _layer_b(xp, xs, dims, j, layer == DEPTH - 1, cache_k_b, cache_v_b,
                                 w_in_b, w_out_b, ln_g_b, ln_b_b, rel_bias)
            outs_b.append(o)
    stack = lambda group, idx: group[0][idx][None] if len(group) == 1 else jnp.stack([o[idx] for o in group])
    return (xp[0].reshape(batch, seq, d_model), xs[0].reshape(dbatch, t_new, d_model),
            *(stack(outs_a, n) for n in range(6)), *(stack(outs_b, n) for n in range(4)))
```

```python
import functools
import math

import jax
import jax.numpy as jnp
import numpy as np
from jax import lax
from jax.experimental import pallas as pl
from jax.experimental.pallas import tpu as pltpu

HEAD_DIM = 128
KV_HEADS_A = 8
IDX_HEADS = 64
IDX_DIM = 128
TOPK_MAX = 256
DIL_WINDOWS = (128, 512, 2048)
DIL_RATES = (1, 4, 16)
W_MAX = max(DIL_WINDOWS)
BLK = 128
N_BUCKETS = 32
REL_MAX_DIST = 2048
DEPTH = 2
ALPHA = (2 * DEPTH) ** 0.25
LN_EPS = 1e-5
PAGE_SIZE = 128

LANE = 128
NEG_INF = float("-inf")
INT_MIN = -(2 ** 31)
M_INIT = -1e30
BUCKET_ZERO = -1
BUCKET_MASKED = -2
VMEM_LIMIT = 56 * 1024 * 1024

f32 = jnp.float32
bf16 = jnp.bfloat16
i32 = jnp.int32


def _params(n_axes, vmem=VMEM_LIMIT):
    return pltpu.CompilerParams(dimension_semantics=("arbitrary",) * n_axes, vmem_limit_bytes=vmem)


def _nt_dot(a, b):
    return lax.dot_general(a, b, (((1,), (1,)), ((), ())), preferred_element_type=f32)


def _silu(x):
    return x * (1.0 / (1.0 + jnp.exp(-x)))


def _softmax_step(s, m_old, l_old):
    m_new = jnp.maximum(m_old, s.max(axis=1, keepdims=True))
    alpha = jnp.exp(m_old - m_new)
    p = jnp.exp(s - m_new)
    return m_new, alpha * l_old + p.sum(axis=1, keepdims=True), alpha, p


def _mm_kernel(x_ref, xs_ref, w_ref, *rest, chunk, transposed, residual):
    if residual:
        r_ref, rs_ref, o_ref, os_ref, wbf_ref = rest
    else:
        o_ref, os_ref, wbf_ref = rest
    mul = _nt_dot if transposed else functools.partial(jnp.dot, preferred_element_type=f32)

    @pl.when(pl.program_id(1) == 0)
    def _():
        def body(r, c):
            rows = pl.ds(pl.multiple_of(r * chunk, chunk), chunk)
            wbf_ref[rows, :] = w_ref[rows, :].astype(bf16)
            return c
        lax.fori_loop(0, w_ref.shape[0] // chunk, body, 0)
        ys = mul(xs_ref[...], wbf_ref[...])
        os_ref[...] = (ALPHA * rs_ref[...] + ys if residual else ys).astype(os_ref.dtype)

    y = mul(x_ref[...], wbf_ref[...])
    o_ref[...] = (ALPHA * r_ref[...] + y if residual else y).astype(o_ref.dtype)


def _pick_tile(n, candidates):
    for c in candidates:
        if n % c == 0:
            return c
    raise ValueError(f"no tile for {n}")


def _matmul(x_bf, xs_bf, w, layer, col_off, ncols, out_dtype, name, transposed=False, residual=None):
    m, k = x_bf.shape
    ms = xs_bf.shape[0]
    tn = _pick_tile(math.gcd(col_off, ncols) if col_off else ncols, (512, 256, 128))
    tm = _pick_tile(m, (1024, 512, 256, 128))
    off_blocks = col_off // tn
    if transposed:
        wblock, wspec, chunk = (tn, k), pl.BlockSpec((None, tn, k), lambda n, i: (layer, n + off_blocks, 0)), LANE
    else:
        wblock, wspec = (k, tn), pl.BlockSpec((None, k, tn), lambda n, i: (layer, 0, n + off_blocks))
        chunk = _pick_tile(k, (256, 128))
    ospecs = [pl.BlockSpec((tm, tn), lambda n, i: (i, n)), pl.BlockSpec((ms, tn), lambda n, i: (0, n))]
    return pl.pallas_call(
        functools.partial(_mm_kernel, chunk=chunk, transposed=transposed, residual=residual is not None), name=name,
        grid=(ncols // tn, m // tm),
        in_specs=[pl.BlockSpec((tm, k), lambda n, i: (i, 0)),
                  pl.BlockSpec((ms, k), lambda n, i: (0, 0)),
                  wspec] + (ospecs if residual is not None else []),
        out_specs=ospecs,
        out_shape=[jax.ShapeDtypeStruct((m, ncols), out_dtype), jax.ShapeDtypeStruct((ms, ncols), out_dtype)],
        scratch_shapes=[pltpu.VMEM(wblock, bf16)],
        compiler_params=_params(2),
    )(x_bf, xs_bf, w, *(residual or ()))


def _mm_heads_kernel(x_ref, xs_ref, w_ref, o_ref, os_ref, oh_ref, wbf_ref, *, k_chunk):
    @pl.when(pl.program_id(1) == 0)
    def _():
        def body(r, c):
            rows = pl.ds(pl.multiple_of(r * k_chunk, k_chunk), k_chunk)
            wbf_ref[rows, :] = w_ref[rows, :].astype(bf16)
            return c
        lax.fori_loop(0, w_ref.shape[0] // k_chunk, body, 0)
        os_ref[...] = jnp.dot(xs_ref[...], wbf_ref[...], preferred_element_type=f32)

    res = jnp.dot(x_ref[...], wbf_ref[...], preferred_element_type=f32)
    o_ref[...] = res
    tm, nh, dh = oh_ref.shape
    flat = oh_ref.reshape(tm * nh, dh)
    for hh in range(nh):
        flat[pl.ds(hh, tm, stride=nh), :] = res[:, hh * dh:(hh + 1) * dh]


def _matmul_heads(x_bf, xs_bf, w, layer, col_off, n_heads, name):
    m, k = x_bf.shape
    ms = xs_bf.shape[0]
    nh = 8 if n_heads % 8 == 0 else n_heads
    tn = nh * HEAD_DIM
    ncols = n_heads * HEAD_DIM
    tm = _pick_tile(m, (512, 256, 128))
    off_blocks = col_off // tn
    k_chunk = _pick_tile(k, (256, 128))
    return pl.pallas_call(
        functools.partial(_mm_heads_kernel, k_chunk=k_chunk), name=name,
        grid=(ncols // tn, m // tm),
        in_specs=[pl.BlockSpec((tm, k), lambda n, i: (i, 0)),
                  pl.BlockSpec((ms, k), lambda n, i: (0, 0)),
                  pl.BlockSpec((None, k, tn), lambda n, i: (layer, 0, n + off_blocks),
                               pipeline_mode=pl.Buffered(1))],
        out_specs=[pl.BlockSpec((tm, tn), lambda n, i: (i, n)),
                   pl.BlockSpec((ms, tn), lambda n, i: (0, n)),
                   pl.BlockSpec((tm, nh, HEAD_DIM), lambda n, i: (i, n, 0))],
        out_shape=[jax.ShapeDtypeStruct((m, ncols), f32), jax.ShapeDtypeStruct((ms, ncols), f32),
                   jax.ShapeDtypeStruct((m, n_heads, HEAD_DIM), f32)],
        scratch_shapes=[pltpu.VMEM((k, tn), bf16)],
        compiler_params=_params(2),
    )(x_bf, xs_bf, w)


def _ln_kernel(z_ref, g_ref, b_ref, o_ref, *obf_ref):
    z = z_ref[...]
    mu = jnp.mean(z, axis=-1, keepdims=True)
    zc = z - mu
    var = jnp.mean(zc * zc, axis=-1, keepdims=True)
    out = zc * lax.rsqrt(var + LN_EPS) * g_ref[...] + b_ref[...]
    o_ref[...] = out
    for ref in obf_ref:
        ref[...] = out.astype(bf16)


def _layernorm(z, g, b, with_bf16):
    m, d = z.shape
    tm = _pick_tile(m, (128, 64, 32, 16, 8))
    row = pl.BlockSpec((tm, d), lambda i: (i, 0))
    vec = pl.BlockSpec((1, d), lambda i: (0, 0))
    n_out = 2 if with_bf16 else 1
    res = pl.pallas_call(
        _ln_kernel, grid=(m // tm,), name="layernorm",
        in_specs=[row, vec, vec], out_specs=[row] * n_out,
        out_shape=[jax.ShapeDtypeStruct((m, d), f32), jax.ShapeDtypeStruct((m, d), bf16)][:n_out],
        compiler_params=_params(1),
    )(z, g.reshape(1, d), b.reshape(1, d))
    return (res[0], res[1] if with_bf16 else None)


def _rel_bucket_np(n):
    max_exact = N_BUCKETS // 2
    nf = np.maximum(n, 1).astype(np.float64)
    large = max_exact + (np.log(nf / max_exact) / math.log(REL_MAX_DIST / max_exact)
                         * (N_BUCKETS - max_exact)).astype(np.int64)
    large = np.minimum(large, N_BUCKETS - 1)
    return np.where(n < max_exact, n, large).astype(np.int32)


def _bucket_codes(dist, valid=None):
    code = np.where(dist >= 0, _rel_bucket_np(np.maximum(dist, 0)), BUCKET_ZERO)
    if valid is not None:
        code = np.where(valid, code, BUCKET_MASKED)
    return code.astype(np.int32)


def _bias_kernel(tbl_ref, bm_ref, o_ref, *, cw):
    h = pl.program_id(0)
    row = tbl_ref[pl.ds(h, 1), :]

    def body(c, carry):
        for t in range(cw // LANE):
            cols = pl.ds(pl.multiple_of(c * cw + t * LANE, LANE), LANE)
            code = bm_ref[:, cols]
            val = jnp.take_along_axis(jnp.broadcast_to(row, code.shape), jnp.maximum(code, 0), axis=1)
            o_ref[0, :, cols] = jnp.where(code >= 0, val, jnp.where(code == BUCKET_MASKED, NEG_INF, 0.0))
        return carry

    lax.fori_loop(0, bm_ref.shape[1] // cw, body, 0, unroll=8)


def _bias_table(rel_bias, codes):
    r, c = codes.shape
    n_heads = rel_bias.shape[1]
    units = c // LANE
    per = max(1, 16 // -(-r // 8))
    cw = LANE * max(u for u in range(1, units + 1) if units % u == 0 and u <= per)
    return pl.pallas_call(
        functools.partial(_bias_kernel, cw=cw), name="bias_table",
        grid=(n_heads,),
        in_specs=[pl.BlockSpec((n_heads, LANE), lambda h: (0, 0)),
                  pl.BlockSpec((r, c), lambda h: (0, 0))],
        out_specs=pl.BlockSpec((1, r, c), lambda h: (h, 0, 0)),
        out_shape=jax.ShapeDtypeStruct((n_heads, r, c), f32),
        compiler_params=_params(1),
    )(jnp.pad(rel_bias.T, ((0, 0), (0, LANE - N_BUCKETS))), jnp.asarray(codes))


def _sortable_key(x):
    bits = pltpu.bitcast(x, i32)
    return bits ^ ((bits >> 31) & 0x7FFFFFFF)


def _kth_largest_key(count_ge, shape, k_sel):
    def body(it, t):
        cand = t + lax.shift_left(jnp.int32(1), 31 - it)
        return jnp.where(count_ge(cand) >= k_sel, cand, t)
    return lax.fori_loop(0, 32, body, jnp.full(shape, INT_MIN, i32))


def _select_prompt_kernel(qi_ref, ki_ref, wt_ref, m_ref, kibf_ref, key_ref, *, n_idx_heads, k_sel, ck, hb, cscale):
    i = pl.program_id(1)
    tq = qi_ref.shape[0]

    @pl.when(i == 0)
    def _():
        kibf_ref[...] = ki_ref[...].astype(bf16)

    nck = ((i + 1) * tq + ck - 1) // ck
    t_row = i * tq + lax.broadcasted_iota(i32, (1, tq), 1)

    def chunk_body(c, carry):
        koff = pl.multiple_of(c * ck, ck)
        kc = kibf_ref[pl.ds(koff, ck), :]

        def heads_body(hg, acc):
            hoff = hg * (hb * IDX_DIM)
            qstack = jnp.concatenate(
                [qi_ref[:, pl.ds(pl.multiple_of(hoff + j * IDX_DIM, IDX_DIM), IDX_DIM)] for j in range(hb)], axis=0)
            s = _nt_dot(kc, qstack)
            w = wt_ref[pl.ds(pl.multiple_of(hg * hb, hb), hb), :] * cscale
            for j in range(hb):
                acc = acc + jnp.maximum(s[:, j * tq:(j + 1) * tq], 0.0) * w[j:j + 1, :]
            return acc

        acc = lax.fori_loop(0, n_idx_heads // hb, heads_body, jnp.zeros((ck, tq), f32), unroll=4)
        l_col = koff + lax.broadcasted_iota(i32, (ck, 1), 0)
        key_ref[pl.ds(koff, ck), :] = jnp.where(l_col <= t_row, _sortable_key(acc), INT_MIN)
        return carry

    lax.fori_loop(0, nck, chunk_body, 0)

    def count_ge(cand):
        def body(c, cnt):
            kc = key_ref[pl.ds(pl.multiple_of(c * ck, ck), ck), :]
            hit = jnp.where(kc >= cand, 1, 0).astype(i32)
            return cnt + hit.reshape(ck // 8, 8, tq).sum(axis=0)
        cnt = lax.fori_loop(0, nck, body, jnp.zeros((8, tq), i32))
        return cnt.sum(axis=0, keepdims=True)

    thr = _kth_largest_key(count_ge, (1, tq), k_sel)
    thr = jnp.maximum(thr, INT_MIN + 1)
    n_ge = count_ge(thr)
    ties = jnp.max(n_ge) > k_sel

    m_ref[...] = jnp.full(m_ref.shape, NEG_INF, m_ref.dtype)

    def write_chunk(koff, sel):
        m_ref[:, pl.ds(koff, ck)] = jnp.where(sel, 0.0, NEG_INF).astype(f32).T.astype(m_ref.dtype)

    @pl.when(jnp.logical_not(ties))
    def _():
        def out_body(c, carry):
            koff = pl.multiple_of(c * ck, ck)
            write_chunk(koff, key_ref[pl.ds(koff, ck), :] >= thr)
            return carry
        lax.fori_loop(0, nck, out_body, 0)

    @pl.when(ties)
    def _():
        quota = (k_sel - count_ge(thr + 1)).astype(f32)
        tri = (lax.broadcasted_iota(i32, (ck, ck), 0) >= lax.broadcasted_iota(i32, (ck, ck), 1)).astype(bf16)

        def out_body(c, seen):
            koff = pl.multiple_of(c * ck, ck)
            key = key_ref[pl.ds(koff, ck), :]
            eq = key == thr
            rank = jnp.dot(tri, jnp.where(eq, 1.0, 0.0).astype(bf16), preferred_element_type=f32) + seen
            write_chunk(koff, jnp.where(eq, jnp.where(rank <= quota, thr, thr - 1), key) >= thr)
            return rank[ck - 1:ck, :]
        lax.fori_loop(0, nck, out_body, jnp.zeros((1, tq), f32))


SELECT_KEY_CHUNK = 256
SELECT_HEADS_PER_STEP = 16


def _select_prompt(qi, ki, wi_t, batch, seq, k_sel):
    n_idx_heads = wi_t.shape[0]
    tq, ck = BLK, SELECT_KEY_CHUNK
    hb = _pick_tile(n_idx_heads, tuple(SELECT_HEADS_PER_STEP >> s for s in range(SELECT_HEADS_PER_STEP.bit_length())))
    nblk = seq // tq
    cscale = (IDX_DIM ** -0.5) * (n_idx_heads ** -0.5)
    return pl.pallas_call(
        functools.partial(_select_prompt_kernel, n_idx_heads=n_idx_heads, k_sel=k_sel, ck=ck, hb=hb, cscale=cscale),
        name="select_prompt", grid=(batch, nblk),
        in_specs=[pl.BlockSpec((tq, n_idx_heads * IDX_DIM), lambda b, i: (b * nblk + i, 0)),
                  pl.BlockSpec((seq, IDX_DIM), lambda b, i: (b, 0)),
                  pl.BlockSpec((n_idx_heads, tq), lambda b, i: (0, b * nblk + i))],
        out_specs=pl.BlockSpec((tq, seq), lambda b, i: (b * nblk + i, 0)),
        out_shape=jax.ShapeDtypeStruct((batch * seq, seq), bf16),
        scratch_shapes=[pltpu.VMEM((seq, IDX_DIM), bf16), pltpu.VMEM((seq, tq), i32)],
        compiler_params=_params(2),
    )(qi, ki, wi_t)


def _attend_prompt_kernel(q_ref, k_ref, v_ref, mk_ref, g_ref, gate_ref, o_ref,
                          kbf_ref, vbf_ref, s_ref, mp_ref, l_ref, acc_ref, *, group, ck, sub, scale):
    i = pl.program_id(2)
    nblk = pl.num_programs(2)
    tq = q_ref.shape[0]
    rows = group * tq

    @pl.when(i == 0)
    def _():
        kbf_ref[...] = k_ref[...].astype(bf16)
        vbf_ref[...] = v_ref[...].astype(bf16)

    nck = ((i + 1) * tq + ck - 1) // ck
    goff0 = (nblk - 1 - i) * tq
    qs = jnp.concatenate([q_ref[:, g * HEAD_DIM:(g + 1) * HEAD_DIM] for g in range(group)], axis=0)

    mp_ref[...] = jnp.full(mp_ref.shape, M_INIT, f32)

    def logits_body(c, carry):
        for u in range(ck // sub):
            koff = pl.multiple_of(c * ck + u * sub, sub)
            goff = pl.multiple_of(goff0 + koff, LANE)
            s = _nt_dot(qs, kbf_ref[pl.ds(koff, sub), :]) * scale
            bias = jnp.concatenate([g_ref[g, :, pl.ds(goff, sub)] for g in range(group)], axis=0)
            sel = mk_ref[:, pl.ds(koff, sub)].astype(f32)
            s = s + bias + jnp.concatenate([sel] * group, axis=0)
            s_ref[:, pl.ds(koff, sub)] = s
            mp_ref[...] = functools.reduce(jnp.maximum, [mp_ref[...]] + [s[:, t * LANE:(t + 1) * LANE]
                                                                         for t in range(sub // LANE)])
        return carry

    lax.fori_loop(0, nck, logits_body, 0)
    m_all = jnp.broadcast_to(mp_ref[...].max(axis=1, keepdims=True), (rows, LANE))

    l_ref[...] = jnp.zeros(l_ref.shape, f32)
    acc_ref[...] = jnp.zeros(acc_ref.shape, f32)

    def pv_body(c, carry):
        for u in range(ck // sub):
            koff = pl.multiple_of(c * ck + u * sub, sub)
            ps = [jnp.exp(s_ref[:, pl.ds(pl.multiple_of(koff + t * LANE, LANE), LANE)] - m_all)
                  for t in range(sub // LANE)]
            l_ref[...] += functools.reduce(jnp.add, ps)
            p = jnp.concatenate(ps, axis=1).astype(bf16)
            acc_ref[...] += jnp.dot(p, vbf_ref[pl.ds(koff, sub), :], preferred_element_type=f32)
        return carry

    lax.fori_loop(0, nck, pv_body, 0)
    o = acc_ref[...] / l_ref[...].sum(axis=1, keepdims=True)
    for g in range(group):
        cols = slice(g * HEAD_DIM, (g + 1) * HEAD_DIM)
        o_ref[:, cols] = (o[g * tq:(g + 1) * tq] * _silu(gate_ref[:, cols])).astype(o_ref.dtype)


ATTN_A_CHUNK = 512
ATTN_A_SUB = 512
ATTN_A_TQ = 256


def _attend_prompt(q, k, v, maskadd, bias_tab, gate, batch, seq):
    kv = k.shape[1] // HEAD_DIM
    group = q.shape[1] // (kv * HEAD_DIM)
    tq, ck = ATTN_A_TQ, ATTN_A_CHUNK
    nblk = seq // tq
    gw = group * HEAD_DIM
    qspec = pl.BlockSpec((tq, gw), lambda kh, b, i: (b * nblk + i, kh))
    kvspec = pl.BlockSpec((seq, HEAD_DIM), lambda kh, b, i: (b, kh))
    return pl.pallas_call(
        functools.partial(_attend_prompt_kernel, group=group, ck=ck, sub=ATTN_A_SUB, scale=HEAD_DIM ** -0.5),
        name="attend_prompt_a", grid=(kv, batch, nblk),
        in_specs=[qspec, kvspec, kvspec,
                  pl.BlockSpec((tq, seq), lambda kh, b, i: (b * nblk + i, 0)),
                  pl.BlockSpec((group, tq, bias_tab.shape[2]), lambda kh, b, i: (kh, 0, 0)),
                  qspec],
        out_specs=qspec,
        out_shape=jax.ShapeDtypeStruct(q.shape, bf16),
        scratch_shapes=[pltpu.VMEM((seq, HEAD_DIM), bf16), pltpu.VMEM((seq, HEAD_DIM), bf16),
                        pltpu.VMEM((group * tq, seq), f32), pltpu.VMEM((group * tq, LANE), f32),
                        pltpu.VMEM((group * tq, LANE), f32), pltpu.VMEM((group * tq, HEAD_DIM), f32)],
        compiler_params=_params(3),
    )(q, k, v, maskadd, bias_tab, gate)


def _score_page(qs, wb, kpage, t_new):
    s = _nt_dot(qs, kpage.astype(bf16))
    x = jnp.maximum(s, 0.0) * wb
    return x.reshape(t_new, x.shape[0] // t_new, x.shape[1]).sum(axis=1)


def _sample_scores_kernel(pt_ref, qs_ref, wb_ref, *refs, pg, t_new):
    o_ref = refs[pg]
    for r in range(pg):
        o_ref[0, :, r * PAGE_SIZE:(r + 1) * PAGE_SIZE] = _score_page(qs_ref[0], wb_ref[0], refs[r][...], t_new)


def _sample_scores_tail_kernel(qs_ref, wb_ref, kt_ref, o_ref, *, t_new):
    o_ref[0] = _score_page(qs_ref[0], wb_ref[0], kt_ref[0], t_new)


def _sample_scores(page_table, qs, wb, cache_ki, layer, ki_tail, t_new):
    dbatch, rows, _ = qs.shape
    n_pages = page_table.shape[1]
    pg = _pick_tile(n_pages, (16, 8, 4, 2, 1))
    qspec = pl.BlockSpec((1, rows, IDX_DIM), lambda b, s, pt: (b, 0, 0))
    page_specs = [pl.BlockSpec((None, None, PAGE_SIZE, IDX_DIM),
                               lambda b, s, pt, r=r: (layer, pt[b, s * pg + r], 0, 0)) for r in range(pg)]
    main = pl.pallas_call(
        functools.partial(_sample_scores_kernel, pg=pg, t_new=t_new), name="sample_scores",
        grid_spec=pltpu.PrefetchScalarGridSpec(
            num_scalar_prefetch=1, grid=(dbatch, n_pages // pg),
            in_specs=[qspec, qspec] + page_specs,
            out_specs=pl.BlockSpec((1, t_new, pg * PAGE_SIZE), lambda b, s, pt: (b, 0, s))),
        out_shape=jax.ShapeDtypeStruct((dbatch, t_new, n_pages * PAGE_SIZE), f32),
        compiler_params=_params(2),
    )(page_table, qs, wb, *([cache_ki] * pg))
    spec3 = lambda d1, d2: pl.BlockSpec((1, d1, d2), lambda b: (b, 0, 0))
    tail = pl.pallas_call(
        functools.partial(_sample_scores_tail_kernel, t_new=t_new), name="sample_scores_tail",
        grid=(dbatch,),
        in_specs=[spec3(rows, IDX_DIM), spec3(rows, IDX_DIM), spec3(PAGE_SIZE, IDX_DIM)],
        out_specs=spec3(t_new, PAGE_SIZE),
        out_shape=jax.ShapeDtypeStruct((dbatch, t_new, PAGE_SIZE), f32),
        compiler_params=_params(1),
    )(qs, wb, ki_tail)
    return jnp.concatenate([main, tail], axis=2)


def _select_sample_kernel(s_ref, o_ref, key_ref, *, past, t_new, k_sel):
    rows, n = s_ref.shape
    t = past + lax.broadcasted_iota(i32, (rows, 1), 0) % t_new
    l = lax.broadcasted_iota(i32, (1, n), 1)
    key_ref[...] = jnp.where(l <= t, _sortable_key(s_ref[...]), INT_MIN)

    def count_ge(cand):
        return jnp.where(key_ref[...] >= cand, 1, 0).astype(i32).sum(axis=1, keepdims=True)

    thr = jnp.maximum(_kth_largest_key(count_ge, (rows, 1), k_sel), INT_MIN + 1)
    quota = (k_sel - count_ge(thr + 1)).astype(f32)
    tri = (lax.broadcasted_iota(i32, (LANE, LANE), 0) <= lax.broadcasted_iota(i32, (LANE, LANE), 1)).astype(bf16)

    def tile_body(c, seen):
        lanes = pl.ds(pl.multiple_of(c * LANE, LANE), LANE)
        key = key_ref[:, lanes]
        eq = key == thr
        rank = jnp.dot(jnp.where(eq, 1.0, 0.0).astype(bf16), tri, preferred_element_type=f32) + seen
        kept = jnp.where(eq, jnp.where(rank <= quota, thr, thr - 1), key) >= thr
        o_ref[:, lanes] = jnp.where(kept, 0.0, NEG_INF).astype(f32)
        return rank.max(axis=1, keepdims=True)

    lax.fori_loop(0, n // LANE, tile_body, jnp.zeros((rows, 1), f32), unroll=3)


def _select_sample(scores, past, t_new, k_sel):
    return pl.pallas_call(
        functools.partial(_select_sample_kernel, past=past, t_new=t_new, k_sel=k_sel), name="select_sample",
        out_shape=jax.ShapeDtypeStruct(scores.shape, f32),
        scratch_shapes=[pltpu.VMEM(scores.shape, i32)],
        compiler_params=pltpu.CompilerParams(vmem_limit_bytes=VMEM_LIMIT),
    )(scores)


def _attend_sample_a_kernel(pt_ref, q_ref, bias_ref, mask_ref, biast_ref, maskt_ref, gate_ref, *refs, pg, kv, scale):
    kpages, vpages = refs[:pg], refs[pg:2 * pg]
    kt_ref, vt_ref, o_ref, m_ref, l_ref, acc_ref = refs[2 * pg:]
    s_id = pl.program_id(1)
    rows = q_ref.shape[1] // kv

    @pl.when(s_id == 0)
    def _():
        m_ref[...] = jnp.full(m_ref.shape, M_INIT, f32)
        l_ref[...] = jnp.zeros(l_ref.shape, f32)
        acc_ref[...] = jnp.zeros(acc_ref.shape, f32)

    def head_rows(page_ref, kh):
        return page_ref[pl.ds(kh, PAGE_SIZE, stride=kv), :].astype(bf16)

    def update(k_refs, v_refs, bias, mask):
        s = jnp.concatenate(
            [jnp.concatenate([_nt_dot(q_ref[0, kh * rows:(kh + 1) * rows, :], head_rows(kr, kh)) for kr in k_refs],
                             axis=1) for kh in range(kv)], axis=0)
        s = s * scale + bias + jnp.concatenate([mask] * kv, axis=0)
        m_new, l_new, alpha, p = _softmax_step(s, m_ref[...], l_ref[...])
        p = p.astype(bf16)
        pv = []
        for kh in range(kv):
            pk = p[kh * rows:(kh + 1) * rows]
            pv.append(sum(jnp.dot(pk[:, r * PAGE_SIZE:(r + 1) * PAGE_SIZE], head_rows(vr, kh),
                                  preferred_element_type=f32) for r, vr in enumerate(v_refs)))
        acc_ref[...] = alpha * acc_ref[...] + jnp.concatenate(pv, axis=0)
        m_ref[...] = m_new
        l_ref[...] = l_new

    update(kpages, vpages, bias_ref[...], mask_ref[0])

    @pl.when(s_id == pl.num_programs(1) - 1)
    def _():
        update([kt_ref.at[0]], [vt_ref.at[0]], biast_ref[...], maskt_ref[0])
        o_ref[0] = acc_ref[...] / l_ref[...] * _silu(gate_ref[0])


def _attend_sample_a(page_table, q2, bias_tab, mask, cache_k, cache_v, layer, k_tail, v_tail, gate2):
    dbatch, qrows, _ = q2.shape
    kv = cache_k.shape[2] // PAGE_SIZE
    rows = qrows // kv
    n_pages = page_table.shape[1]
    past = n_pages * PAGE_SIZE
    pg = _pick_tile(n_pages, (16, 8, 4, 2, 1))
    qspec = pl.BlockSpec((1, qrows, HEAD_DIM), lambda b, s, pt: (b, 0, 0))
    kspecs = [pl.BlockSpec((None, None, PAGE_SIZE * kv, HEAD_DIM),
                           lambda b, s, pt, r=r: (layer, pt[b, s * pg + r], 0, 0)) for r in range(pg)]
    tailspec = pl.BlockSpec((1, PAGE_SIZE * kv, HEAD_DIM), lambda b, s, pt: (b, 0, 0))
    return pl.pallas_call(
        functools.partial(_attend_sample_a_kernel, pg=pg, kv=kv, scale=HEAD_DIM ** -0.5), name="attend_sample_a",
        grid_spec=pltpu.PrefetchScalarGridSpec(
            num_scalar_prefetch=1, grid=(dbatch, n_pages // pg),
            in_specs=[qspec,
                      pl.BlockSpec((qrows, pg * PAGE_SIZE), lambda b, s, pt: (0, s)),
                      pl.BlockSpec((1, rows, pg * PAGE_SIZE), lambda b, s, pt: (b, 0, s)),
                      pl.BlockSpec((qrows, PAGE_SIZE), lambda b, s, pt: (0, past // PAGE_SIZE)),
                      pl.BlockSpec((1, rows, PAGE_SIZE), lambda b, s, pt: (b, 0, past // PAGE_SIZE)),
                      qspec] + kspecs + kspecs + [tailspec, tailspec],
            out_specs=qspec,
            scratch_shapes=[pltpu.VMEM((qrows, 1), f32), pltpu.VMEM((qrows, 1), f32),
                            pltpu.VMEM((qrows, HEAD_DIM), f32)]),
        out_shape=jax.ShapeDtypeStruct(q2.shape, f32),
        compiler_params=_params(2),
    )(page_table, q2, bias_tab, mask, bias_tab, mask, gate2,
      *([cache_k] * pg), *([cache_v] * pg), k_tail, v_tail)


def _attend_prompt_b_kernel(q0_ref, q1_ref, q2_ref, k_ref, v_ref, gate_ref, gb_ref, *rest, rates, scale, unroll, shift):
    if shift is None:
        o_ref, qf_ref, og_ref, lse_ref = rest
    else:
        bk_ref, bv_ref, nk_ref, nv_ref, o_ref, ok_ref, ov_ref, qf_ref, og_ref, lse_ref, *ring_refs = rest
        step = pl.program_id(0) * pl.num_programs(1) + pl.program_id(1)
        _shift_ring_step(step, pl.num_programs(0) * pl.num_programs(1), (bk_ref, bv_ref), (nk_ref, nv_ref),
                         (ok_ref, ov_ref), *ring_refs, **shift)
    seq = k_ref.shape[0]
    col = lax.broadcasted_iota(i32, (BLK, 2 * BLK), 1)

    for g, (q_ref, d) in enumerate(zip((q0_ref, q1_ref, q2_ref), rates)):
        qf_ref[...] = q_ref[...].astype(f32)
        nb = seq // d // BLK
        bias = gb_ref[0, :, g * 2 * BLK:(g + 1) * 2 * BLK]

        def blk_body(it, carry, g=g, d=d, nb=nb, bias=bias):
            r = it // nb
            blk = it % nb
            start = r + blk * (BLK * d)
            start_prev = r + jnp.maximum(blk - 1, 0) * (BLK * d)
            cur = pl.ds(start, BLK, stride=d)
            prev = pl.ds(start_prev, BLK, stride=d)
            qb = qf_ref[cur, :].astype(bf16)
            kcat = jnp.concatenate([k_ref[prev, :], k_ref[cur, :]], axis=0).astype(bf16)
            vcat = jnp.concatenate([v_ref[prev, :], v_ref[cur, :]], axis=0).astype(bf16)
            s = _nt_dot(qb, kcat) * scale + bias
            s = jnp.where((col >= BLK) | (blk > 0), s, NEG_INF)
            m = s.max(axis=1, keepdims=True)
            e = jnp.exp(s - m)
            ssum = e.sum(axis=1, keepdims=True)
            og_ref[g, cur, :] = jnp.dot(e.astype(bf16), vcat, preferred_element_type=f32) / ssum
            lse_ref[g, cur, :] = jnp.broadcast_to(m + jnp.log(ssum), (BLK, HEAD_DIM))
            return carry

        lax.fori_loop(0, d * nb, blk_body, 0, unroll=unroll)

    def merge_body(cb, carry):
        rows = pl.ds(pl.multiple_of(cb * BLK, BLK), BLK)
        lses = [lse_ref[g, rows, :] for g in range(len(rates))]
        top = functools.reduce(jnp.maximum, lses)
        wts = [jnp.exp(x - top) for x in lses]
        num = sum(w * og_ref[g, rows, :] for g, w in enumerate(wts))
        gt = gate_ref[rows, :]
        o_ref[rows, :] = (num / sum(wts) * _silu(gt)).astype(o_ref.dtype)
        return carry

    lax.fori_loop(0, seq // BLK, merge_body, 0)


DIL_BLOCK_UNROLL = 16


def _attend_prompt_b(q, k, v, gate, bias_tab, batch, seq, buffers=None):
    n_heads = k.shape[1] // HEAD_DIM
    n_dil = len(DIL_RATES)
    blk = (seq, HEAD_DIM)
    qspecs = [pl.BlockSpec(blk, lambda b, h, g=g: (b, g * n_heads + h)) for g in range(n_dil)]
    hspec = pl.BlockSpec(blk, lambda b, h: (b, h))
    in_specs = qspecs + [hspec, hspec, hspec, pl.BlockSpec((1, BLK, bias_tab.shape[2]), lambda b, h: (h, 0, 0))]
    out_specs, out_shape = [hspec], [jax.ShapeDtypeStruct(k.shape, bf16)]
    scratch = [pltpu.VMEM(blk, f32), pltpu.VMEM((n_dil,) + blk, f32), pltpu.VMEM((n_dil,) + blk, f32)]
    args, shift = [q, q, q, k, v, gate, bias_tab], None
    if buffers is not None:
        buf_k, buf_v, layer, new_k, new_v = buffers
        plan = _shift_plan(buf_k, new_k)
        if 2 * plan["dbatch"] * ((plan["rows"] - plan["shift"]) // plan["chunk"]) + SHIFT_RING <= batch * n_heads:
            shift = dict(plan, layer=layer)
            anyspec = pl.BlockSpec(memory_space=pl.ANY)
            out = jax.ShapeDtypeStruct((plan["dbatch"], plan["rows"], buf_k.shape[3]), buf_k.dtype)
            in_specs += [anyspec] * 4
            out_specs += [anyspec] * 2
            out_shape += [out, out]
            scratch += _shift_scratch(plan, buf_k)
            args += [buf_k, buf_v, new_k, new_v]
    res = pl.pallas_call(
        functools.partial(_attend_prompt_b_kernel, rates=DIL_RATES, scale=HEAD_DIM ** -0.5, unroll=DIL_BLOCK_UNROLL, shift=shift),
        name="attend_prompt_b", grid=(batch, n_heads),
        in_specs=in_specs, out_specs=out_specs, out_shape=out_shape, scratch_shapes=scratch,
        compiler_params=_params(2),
    )(*args)
    if shift is not None:
        return tuple(res)
    if buffers is not None:
        return (res[0],) + tuple(_shift_buffers(*buffers))
    return res[0], None, None


def _attend_sample_b_kernel(q_ref, k_ref, v_ref, kt_ref, vt_ref, bias_ref, gate_ref, o_ref, *, n_dil, t_new, scale):
    wb, hgs, _ = k_ref.shape
    k2 = k_ref.reshape(wb * hgs, HEAD_DIM)
    v2 = v_ref.reshape(wb * hgs, HEAD_DIM)
    for hl in range(hgs):
        cols = slice(hl * HEAD_DIM, (hl + 1) * HEAD_DIM)
        head = pl.ds(hl, wb, stride=hgs)
        q = q_ref[0, hl]
        s_main = _nt_dot(q, k2[head, :].astype(bf16)) * scale + bias_ref[hl, :, :wb]
        s_tail = _nt_dot(q, kt_ref[0, :, cols].astype(bf16)) * scale + bias_ref[hl, :, wb:]
        m = jnp.maximum(s_main.max(axis=1, keepdims=True), s_tail.max(axis=1, keepdims=True))
        e_main = jnp.exp(s_main - m)
        e_tail = jnp.exp(s_tail - m)
        ssum = e_main.sum(axis=1, keepdims=True) + e_tail.sum(axis=1, keepdims=True)
        o = (jnp.dot(e_main.astype(bf16), v2[head, :].astype(bf16), preferred_element_type=f32)
             + jnp.dot(e_tail.astype(bf16), vt_ref[0, :, cols].astype(bf16), preferred_element_type=f32)) / ssum
        lse = m + jnp.log(ssum)
        lses = [lse[g * t_new:(g + 1) * t_new] for g in range(n_dil)]
        top = functools.reduce(jnp.maximum, lses)
        wts = [jnp.exp(x - top) for x in lses]
        num = sum(w * o[g * t_new:(g + 1) * t_new] for g, w in enumerate(wts))
        o_ref[0, hl] = num / sum(wts) * _silu(gate_ref[0, hl])


def _attend_sample_b(q16, buf_k, buf_v, layer, k_tail, v_tail, bias_tab, gate4, t_new):
    dbatch, n_heads, rows, _ = q16.shape
    wb = buf_k.shape[2]
    hgs = 8 if n_heads % 8 == 0 else n_heads
    grouped = lambda a: a.reshape(a.shape[0], dbatch, wb, n_heads // hgs, hgs, HEAD_DIM)
    bufspec = pl.BlockSpec((None, None, wb, None, hgs, HEAD_DIM), lambda b, hg: (layer, b, 0, hg, 0, 0))
    tailspec = pl.BlockSpec((1, LANE, hgs * HEAD_DIM), lambda b, hg: (b, 0, hg))
    hspec = lambda r: pl.BlockSpec((1, hgs, r, HEAD_DIM), lambda b, hg: (b, hg, 0, 0))
    return pl.pallas_call(
        functools.partial(_attend_sample_b_kernel, n_dil=len(DIL_RATES), t_new=t_new, scale=HEAD_DIM ** -0.5),
        name="attend_sample_b", grid=(dbatch, n_heads // hgs),
        in_specs=[hspec(rows), bufspec, bufspec, tailspec, tailspec,
                  pl.BlockSpec((hgs, rows, wb + LANE), lambda b, hg: (hg, 0, 0)),
                  hspec(t_new)],
        out_specs=hspec(t_new),
        out_shape=jax.ShapeDtypeStruct((dbatch, n_heads, t_new, HEAD_DIM), f32),
        compiler_params=_params(2),
    )(q16, grouped(buf_k), grouped(buf_v), k_tail, v_tail, bias_tab, gate4)


SHIFT_RING = 4
SHIFT_LAG = 2
SHIFT_CHUNK_BYTES = 5 << 20


def _shift_kernel(bk_ref, bv_ref, nk_ref, nv_ref, ok_ref, ov_ref, ring, in_sem, out_sem, new_sem,
                  *, layer, dbatch, rows, shift, chunk):
    reads, writes, tails = [], [], []
    n_chunks = (rows - shift) // chunk
    for a, (src, new, dst) in enumerate(((bk_ref, nk_ref, ok_ref), (bv_ref, nv_ref, ov_ref))):
        for b in range(dbatch):
            for c in range(n_chunks):
                slot = len(reads) % SHIFT_RING
                reads.append(pltpu.make_async_copy(src.at[layer, b, pl.ds(shift + c * chunk, chunk)],
                                                   ring.at[slot], in_sem.at[slot]))
                writes.append(pltpu.make_async_copy(ring.at[slot], dst.at[b, pl.ds(c * chunk, chunk)],
                                                    out_sem.at[slot]))
            tails.append(pltpu.make_async_copy(new.at[b], dst.at[b, pl.ds(rows - shift, shift)], new_sem.at[a, b]))
    for t in tails:
        t.start()
    n = len(reads)
    for i in range(n + SHIFT_LAG):
        if i < n:
            if i >= SHIFT_RING:
                writes[i - SHIFT_RING].wait()
            reads[i].start()
        if 0 <= i - SHIFT_LAG < n:
            reads[i - SHIFT_LAG].wait()
            writes[i - SHIFT_LAG].start()
    for i in range(max(0, n - SHIFT_RING), n):
        writes[i].wait()
    for t in tails:
        t.wait()


def _shift_plan(buf, new):
    _, dbatch, rows, width = buf.shape
    shift = new.shape[1]
    body = (rows - shift) // shift
    per = max(u for u in range(1, body + 1)
              if body % u == 0 and u * shift * width * buf.dtype.itemsize <= SHIFT_CHUNK_BYTES)
    return dict(dbatch=dbatch, rows=rows, shift=shift, chunk=per * shift)


def _shift_scratch(plan, buf):
    return [pltpu.VMEM((SHIFT_RING, plan["chunk"], buf.shape[3]), buf.dtype),
            pltpu.SemaphoreType.DMA((SHIFT_RING,)), pltpu.SemaphoreType.DMA((SHIFT_RING,)),
            pltpu.SemaphoreType.DMA((2, plan["dbatch"]))]


def _shift_ring_step(t, n_steps, bufs, news, outs, ring, in_sem, out_sem, new_sem, *, layer, dbatch, rows, shift, chunk):
    per_b = (rows - shift) // chunk
    per_arr = dbatch * per_b
    n = len(bufs) * per_arr

    def for_chunk(i, read, action):
        slot = i % SHIFT_RING
        b, c = (i % per_arr) // per_b, i % per_b
        for a in range(len(bufs)):
            @pl.when(i // per_arr == a)
            def _(a=a):
                if read:
                    action(pltpu.make_async_copy(bufs[a].at[layer, b, pl.ds(shift + c * chunk, chunk)],
                                                 ring.at[slot], in_sem.at[slot]))
                else:
                    action(pltpu.make_async_copy(ring.at[slot], outs[a].at[b, pl.ds(c * chunk, chunk)],
                                                 out_sem.at[slot]))

    tails = [pltpu.make_async_copy(news[a].at[b], outs[a].at[b, pl.ds(rows - shift, shift)], new_sem.at[a, b])
             for a in range(len(bufs)) for b in range(dbatch)]

    @pl.when(t == 0)
    def _():
        for cp in tails:
            cp.start()

    @pl.when((t >= SHIFT_RING) & (t < n + SHIFT_RING))
    def _():
        for_chunk(t - SHIFT_RING, False, lambda cp: cp.wait())

    @pl.when(t < n)
    def _():
        for_chunk(t, True, lambda cp: cp.start())

    @pl.when((t >= SHIFT_LAG) & (t < n + SHIFT_LAG))
    def _():
        for_chunk(t - SHIFT_LAG, True, lambda cp: cp.wait())
        for_chunk(t - SHIFT_LAG, False, lambda cp: cp.start(priority=1))

    @pl.when(t == n_steps - 1)
    def _():
        for cp in tails:
            cp.wait()


def _shift_buffers(buf_k, buf_v, layer, new_k, new_v):
    plan = _shift_plan(buf_k, new_k)
    anyspec = pl.BlockSpec(memory_space=pl.ANY)
    out = jax.ShapeDtypeStruct((plan["dbatch"], plan["rows"], buf_k.shape[3]), buf_k.dtype)
    return pl.pallas_call(
        functools.partial(_shift_kernel, layer=layer, **plan), name="shift_buffers",
        in_specs=[anyspec] * 4, out_specs=[anyspec] * 2, out_shape=[out, out],
        scratch_shapes=_shift_scratch(plan, buf_k),
        compiler_params=pltpu.CompilerParams(vmem_limit_bytes=VMEM_LIMIT),
    )(buf_k, buf_v, new_k, new_v)


def _offsets(sizes):
    return [int(x) for x in np.cumsum((0,) + tuple(sizes))[:-1]]


def _pad_rows(a, rows):
    return jnp.pad(a, ((0, 0), (0, rows - a.shape[1])) + ((0, 0),) * (a.ndim - 2))


def _layer_a(xp, xs, dims, j, last, cache_k, cache_v, cache_ki, page_table, w_in, w_out, ln_g, ln_b, rel_bias):
    batch, seq, dbatch, t_new = dims
    (xp, xpb), (xs, xsb) = xp, xs
    d_model = xp.shape[1]
    n_heads = d_model // HEAD_DIM
    kv, hi = KV_HEADS_A, IDX_HEADS
    group = n_heads // kv
    branch = n_heads * HEAD_DIM
    sizes = (branch, kv * HEAD_DIM, kv * HEAD_DIM, branch, hi * IDX_DIM, IDX_DIM, hi)
    offs = _offsets(sizes)
    tail_w = 2 * LANE
    w_t = w_in.shape[2] % LANE != 0
    w_use = jnp.swapaxes(w_in, 1, 2) if w_t else w_in
    tail_pad = ((0, 0), (0, tail_w - IDX_DIM - hi), (0, 0)) if w_t else ((0, 0), (0, 0), (0, tail_w - IDX_DIM - hi))
    w_tail = jnp.pad(w_use[j:j + 1, offs[5]:, :] if w_t else w_use[j:j + 1, :, offs[5]:], tail_pad)

    mm = lambda slot, dt, nm: _matmul(xpb, xsb, w_use, j, offs[slot], sizes[slot], dt, f"proj_a_{nm}", w_t)
    (q, q_s), (k_p, k_s), (v_p, v_s) = mm(0, bf16, "q"), mm(1, f32, "k"), mm(2, f32, "v")
    (gate, gate_s), (qi, qi_s) = mm(3, f32, "gate"), mm(4, bf16, "qi")
    kw, kw_s = _matmul(xpb, xsb, w_tail, 0, 0, tail_w, f32, "proj_a_kiwi", w_t)
    ki_p, wi = kw[:, :IDX_DIM], kw[:, IDX_DIM:IDX_DIM + hi]
    ki_s, wi_s = kw_s[:, :IDX_DIM], kw_s[:, IDX_DIM:IDX_DIM + hi]

    dist = (np.arange(ATTN_A_TQ)[:, None] + seq - ATTN_A_TQ) - np.arange(seq + ATTN_A_CHUNK - ATTN_A_TQ)[None, :]
    bias_p = _bias_table(rel_bias, _bucket_codes(dist))
    maskadd = _select_prompt(qi, ki_p, wi.T, batch, seq, min(TOPK_MAX, seq // 4))
    og = _attend_prompt(q, k_p, v_p, maskadd, bias_p, gate, batch, seq)

    q, gate, qi, wi = q_s, gate_s, qi_s, wi_s
    n_pages = page_table.shape[1]
    past = n_pages * PAGE_SIZE
    n_keys = past + t_new
    n_lanes = past + PAGE_SIZE
    cscale = (IDX_DIM ** -0.5) * (hi ** -0.5)
    qs = qi.reshape(dbatch, t_new * hi, IDX_DIM)
    wbc = jnp.broadcast_to((wi * cscale).reshape(dbatch, t_new * hi, 1), (dbatch, t_new * hi, LANE))
    ki_tail = _pad_rows(ki_s.reshape(dbatch, t_new, IDX_DIM), PAGE_SIZE)
    scores = _sample_scores(page_table, qs, wbc, cache_ki, j, ki_tail, t_new)
    mask_s = _select_sample(scores.reshape(dbatch * t_new, n_lanes), past, t_new, min(TOPK_MAX, n_keys // 4))

    def to_rows(a):
        a = a.reshape(dbatch, t_new, kv, group, HEAD_DIM)
        return a.transpose(0, 2, 3, 1, 4).reshape(dbatch, kv * group * t_new, HEAD_DIM)

    def to_page(a):
        return _pad_rows(a.reshape(dbatch, t_new, kv, HEAD_DIM), PAGE_SIZE).reshape(dbatch, PAGE_SIZE * kv, HEAD_DIM)

    rows = group * t_new
    dist = (past + np.arange(t_new)[:, None]) - np.arange(n_lanes)[None, :]
    bias_s = _bias_table(rel_bias, _bucket_codes(dist)).reshape(kv * rows, n_lanes)
    mask_g = jnp.broadcast_to(mask_s.reshape(dbatch, 1, t_new, n_lanes), (dbatch, group, t_new, n_lanes))
    mask_g = mask_g.reshape(dbatch, rows, n_lanes)
    pool = cache_k.shape[1]
    paged = lambda c: c.reshape(c.shape[0], pool, PAGE_SIZE * kv, HEAD_DIM)
    og2 = _attend_sample_a(page_table, to_rows(q), bias_s, mask_g, paged(cache_k), paged(cache_v), j,
                           to_page(k_s), to_page(v_s), to_rows(gate))
    og_s = og2.reshape(dbatch, kv, group, t_new, HEAD_DIM).transpose(0, 3, 1, 2, 4)
    og_s = og_s.reshape(dbatch * t_new, branch).astype(bf16)
    z, z_s = _matmul(og, og_s, w_out, j, 0, d_model, f32, "out_a", residual=(xp, xs))
    xp_new = _layernorm(z, ln_g[j], ln_b[j], not last)
    xs_new = _layernorm(z_s, ln_g[j], ln_b[j], not last)

    shp = lambda a, b_, t_, *rest: a.reshape(b_, t_, *rest)
    outs = (shp(k_p, batch, seq, kv, HEAD_DIM), shp(v_p, batch, seq, kv, HEAD_DIM), shp(ki_p, batch, seq, IDX_DIM),
            shp(k_s, dbatch, t_new, kv, HEAD_DIM), shp(v_s, dbatch, t_new, kv, HEAD_DIM),
            shp(ki_s, dbatch, t_new, IDX_DIM))
    return xp_new, xs_new, outs


def _layer_b(xp, xs, dims, j, last, buf_k, buf_v, w_in, w_out, ln_g, ln_b, rel_bias):
    batch, seq, dbatch, t_new = dims
    (xp, xpb), (xs, xsb) = xp, xs
    d_model = xp.shape[1]
    n_heads = d_model // HEAD_DIM
    branch = n_heads * HEAD_DIM
    n_dil = len(DIL_RATES)
    sizes = (n_dil * branch, branch, branch, branch)
    offs = _offsets(sizes)

    mm = lambda slot, dt, nm: _matmul(xpb, xsb, w_in, j, offs[slot], sizes[slot], dt, f"proj_b_{nm}")
    (q, q_s), (gate, gate_s) = mm(0, bf16, "q"), mm(3, f32, "gate")
    k_p, k_s, k_heads = _matmul_heads(xpb, xsb, w_in, j, offs[1], n_heads, "proj_b_k")
    v_p, v_s, v_heads = _matmul_heads(xpb, xsb, w_in, j, offs[2], n_heads, "proj_b_v")

    m = (np.arange(BLK)[:, None] + BLK) - np.arange(2 * BLK)[None, :]
    codes = [_bucket_codes(d * np.clip(m, 0, w // d), (m >= 0) & (m <= w // d))
             for d, w in zip(DIL_RATES, DIL_WINDOWS)]
    bias_p = _bias_table(rel_bias, np.concatenate(codes, axis=1))
    wb = buf_k.shape[2]
    flat = lambda buf: buf.reshape(buf.shape[0], dbatch, wb * n_heads, HEAD_DIM)
    og, new_k, new_v = _attend_prompt_b(q, k_p, v_p, gate, bias_p, batch, seq,
                                        (flat(buf_k), flat(buf_v), j, k_s.reshape(dbatch, t_new * n_heads, HEAD_DIM),
                                         v_s.reshape(dbatch, t_new * n_heads, HEAD_DIM)))

    q, gate = q_s, gate_s
    rows = -(-(n_dil * t_new) // 16) * 16
    pos = np.arange(wb + LANE)[None, :]
    codes = []
    for d, w in zip(DIL_RATES, DIL_WINDOWS):
        dist = wb + np.arange(t_new)[:, None] - pos
        codes.append(_bucket_codes(dist, (dist >= 0) & (dist % d == 0) & (dist // d <= w // d) & (pos < wb + t_new)))
    codes.append(np.full((rows - n_dil * t_new, wb + LANE), BUCKET_ZERO, np.int32))
    bias_s = _bias_table(rel_bias, np.concatenate(codes, axis=0))
    q16 = q.reshape(dbatch, t_new, n_dil, n_heads, HEAD_DIM).transpose(0, 3, 2, 1, 4)
    q16 = q16.reshape(dbatch, n_heads, n_dil * t_new, HEAD_DIM)
    q16 = jnp.pad(q16, ((0, 0), (0, 0), (0, rows - n_dil * t_new), (0, 0)))
    gate4 = gate.reshape(dbatch, t_new, n_heads, HEAD_DIM).transpose(0, 2, 1, 3)
    k_tail = _pad_rows(k_s.reshape(dbatch, t_new, branch), LANE)
    v_tail = _pad_rows(v_s.reshape(dbatch, t_new, branch), LANE)
    og4 = _attend_sample_b(q16, buf_k, buf_v, j, k_tail, v_tail, bias_s, gate4, t_new)
    og_s = og4.transpose(0, 2, 1, 3).reshape(dbatch * t_new, branch).astype(bf16)
    z, z_s = _matmul(og, og_s, w_out, j, 0, d_model, f32, "out_b", residual=(xp, xs))
    xp_new = _layernorm(z, ln_g[j], ln_b[j], not last)
    xs_new = _layernorm(z_s, ln_g[j], ln_b[j], not last)

    wp = min(W_MAX, seq)
    k_win = k_heads.reshape(batch, seq, n_heads, HEAD_DIM)[:, seq - wp:]
    v_win = v_heads.reshape(batch, seq, n_heads, HEAD_DIM)[:, seq - wp:]
    return xp_new, xs_new, (k_win, v_win, new_k.reshape(dbatch, wb, n_heads, HEAD_DIM),
                            new_v.reshape(dbatch, wb, n_heads, HEAD_DIM))


def kernel(x_prompt, x_sample, cache_k_a, cache_v_a, cache_kidx_a, cache_k_b, cache_v_b, page_table, rel_bias,
           w_in_a, w_out_a, ln_g_a, ln_b_a, w_in_b, w_out_b, ln_g_b, ln_b_b):
    batch, seq, d_model = x_prompt.shape
    dbatch, t_new, _ = x_sample.shape
    dims = (batch, seq, dbatch, t_new)
    with_bf16 = lambda a: (a, a.astype(bf16))
    xp = with_bf16(x_prompt.reshape(batch * seq, d_model))
    xs = with_bf16(x_sample.reshape(dbatch * t_new, d_model))
    outs_a, outs_b = [], []
    for layer in range(DEPTH):
        j = layer // 2
        if layer % 2 == 0:
            xp, xs, o = _layer_a(xp, xs, dims, j, layer == DEPTH - 1, cache_k_a, cache_v_a, cache_kidx_a, page_table,
                                 w_in_a, w_out_a, ln_g_a, ln_b_a, rel_bias)
            outs_a.append(o)
        else:
            xp, xs, o = _layer_b(xp, xs, dims, j, layer == DEPTH - 1, cache_k_b, cache_v_b,
                                 w_in_b, w_out_b, ln_g_b, ln_b_b, rel_bias)
            outs_b.append(o)
    stack = lambda group, idx: group[0][idx][None] if len(group) == 1 else jnp.stack([o[idx] for o in group])
    return (xp[0].reshape(batch, seq, d_model), xs[0].reshape(dbatch, t_new, d_model),
            *(stack(outs_a, n) for n in range(6)), *(stack(outs_b, n) for n in range(4)))
```
